```python
import jax
import jax.numpy as jnp
from jax import lax
import numpy as np

D_MODEL = 2048
BATCH = 4
SEQ = 4096
DEPTH = 2

GRID_W = 64
CTX_LEN = 256
MIX_W = D_MODEL
HEAD_DIM = 128
ATTN_W = MIX_W // 2
N_Q_HEADS = ATTN_W // HEAD_DIM
GQA_GROUP = 4
N_KV_HEADS = N_Q_HEADS // GQA_GROUP
KV_W = N_KV_HEADS * HEAD_DIM
Q_BLOCK = 128
ROPE_THETA = 10000.0
ROPE_FREQS = HEAD_DIM // 4
ATTN_SCALE = HEAD_DIM ** -0.5
POOL_WINDOWS = (2, 4, 8, 16)
N_POOL_GROUPS = len(POOL_WINDOWS)
POOL_W = MIX_W // 4
POOL_GROUP_W = POOL_W // N_POOL_GROUPS
N_SGU_HEADS = 4
SGU_W = MIX_W // 4
SGU_HEAD_W = SGU_W // N_SGU_HEADS
CHUNK = 128
IN_W = ATTN_W + 2 * KV_W + POOL_W + 2 * SGU_W
SPLIT_IDX = (ATTN_W, ATTN_W + KV_W, ATTN_W + 2 * KV_W, ATTN_W + 2 * KV_W + POOL_W, ATTN_W + 2 * KV_W + POOL_W + SGU_W)
N_EXPERTS = 32
TOP_K = 4
EXPERT_FF = D_MODEL
SWIGLU_LIMIT = 7.0
SWIGLU_ALPHA = 1.702
MOE_BLOCK = 128
EPS = 1e-6

kernel_name = 'hybrid_parallel_mixer_moe_dit'


def rms_norm(x, g):
    xf = x.astype(jnp.float32)
    y = xf * lax.rsqrt(jnp.mean(xf * xf, axis=-1, keepdims=True) + EPS)
    return (y * g.astype(jnp.float32)).astype(x.dtype)


def modulate(h, shift, scale):
    return h * (1.0 + scale) + shift


def gelu(x):
    return jax.nn.gelu(x, approximate=False)


def axial_rope_tables(rows):
    row = jnp.broadcast_to(jnp.arange(rows, dtype=jnp.float32)[:, None], (rows, GRID_W)).reshape(-1)
    col = jnp.broadcast_to(jnp.arange(GRID_W, dtype=jnp.float32)[None, :], (rows, GRID_W)).reshape(-1)
    inv_freq = ROPE_THETA ** (-jnp.arange(ROPE_FREQS, dtype=jnp.float32) / ROPE_FREQS)
    ang_r = (row[:, None] * inv_freq)[:, None, :]
    ang_c = (col[:, None] * inv_freq)[:, None, :]
    return (jnp.cos(ang_r), jnp.sin(ang_r), jnp.cos(ang_c), jnp.sin(ang_c))


def _rotate(x, cos, sin):
    x1, x2 = jnp.split(x, 2, axis=-1)
    return jnp.concatenate([x1 * cos - x2 * sin, x2 * cos + x1 * sin], axis=-1)


def apply_axial_rope(x, tables):
    cos_r, sin_r, cos_c, sin_c = tables
    x_r, x_c = jnp.split(x.astype(jnp.float32), 2, axis=-1)
    out = jnp.concatenate([_rotate(x_r, cos_r, sin_r), _rotate(x_c, cos_c, sin_c)], axis=-1)
    return out.astype(x.dtype)


def _attend(q, k, v):
    s = jnp.einsum('bqhgd,bkhd->bhgqk', q, k, preferred_element_type=jnp.float32) * ATTN_SCALE
    p = jax.nn.softmax(s, axis=-1).astype(v.dtype)
    return jnp.einsum('bhgqk,bkhd->bqhgd', p, v)


def latent_attention(q, k, v, ck, cv):
    B, S = q.shape[:2]
    k_all = jnp.concatenate([ck, k], axis=1)
    v_all = jnp.concatenate([cv, v], axis=1)
    qb = q.reshape(B, S // Q_BLOCK, Q_BLOCK, N_KV_HEADS, GQA_GROUP, HEAD_DIM).transpose(1, 0, 2, 3, 4, 5)
    o = lax.map(lambda qblk: _attend(qblk, k_all, v_all), qb)
    return o.transpose(1, 0, 2, 3, 4, 5).reshape(B, S, ATTN_W)


def context_attention(cq, ck, cv):
    B, C = cq.shape[:2]
    o = _attend(cq.reshape(B, C, N_KV_HEADS, GQA_GROUP, HEAD_DIM), ck, cv)
    return o.reshape(B, C, ATTN_W)


def multiscale_pool(p, pool_w, pool_scale):
    B, L, _ = p.shape
    pf = p.reshape(B, L, N_POOL_GROUPS, POOL_GROUP_W).astype(jnp.float32)
    cs = jnp.concatenate([jnp.zeros_like(pf[:, :1]), jnp.cumsum(pf, axis=1)], axis=1)
    t = jnp.arange(L)
    pooled = []
    for gi, w in enumerate(POOL_WINDOWS):
        lo = jnp.clip(t - w // 2, 0, L)
        hi = jnp.clip(t + w // 2, 0, L)
        cs_g = cs[:, :, gi]
        cnt = (hi - lo).astype(jnp.float32)[None, :, None]
        pooled.append((cs_g[:, hi] - cs_g[:, lo]) / cnt)
    mixed = (jnp.stack(pooled, axis=2) - pf).astype(p.dtype)
    y = jnp.einsum('blgc,gcd->blgd', mixed, pool_w).reshape(B, L, POOL_W)
    return y * pool_scale


def spatial_gating(u, v, sgu_norm_g, sgu_w, sgu_b):
    B, L, _ = u.shape
    vh = rms_norm(v.reshape(B, L, N_SGU_HEADS, SGU_HEAD_W), sgu_norm_g)
    vc = vh.reshape(B, L // CHUNK, CHUNK, N_SGU_HEADS, SGU_HEAD_W)
    mixed = jnp.einsum('hij,bnjhc->bnihc', sgu_w, vc) + sgu_b.T[:, :, None]
    return u * mixed.reshape(B, L, SGU_W)


def moe_ffn(h, router_w, router_b, w_gu, b_gu, w_down, b_down):
    T, D = h.shape
    logits = jnp.dot(h, router_w, preferred_element_type=jnp.float32) + router_b.astype(jnp.float32)
    top_val, top_idx = lax.top_k(logits, TOP_K)
    gates = jax.nn.softmax(top_val, axis=-1)
    A = T * TOP_K
    flat_e = top_idx.reshape(A).astype(jnp.int32)
    flat_tok = jnp.repeat(jnp.arange(T, dtype=jnp.int32), TOP_K)
    flat_gate = gates.reshape(A)
    order = jnp.argsort(flat_e)
    se = flat_e[order]
    st = flat_tok[order]
    sg = flat_gate[order]
    counts = jnp.bincount(flat_e, length=N_EXPERTS).astype(jnp.int32)
    padded = (counts + MOE_BLOCK - 1) // MOE_BLOCK * MOE_BLOCK
    padded_end = jnp.cumsum(padded)
    padded_start = padded_end - padded
    group_start = jnp.cumsum(counts) - counts
    dest = padded_start[se] + jnp.arange(A, dtype=jnp.int32) - group_start[se]
    n_blocks = -(-A // MOE_BLOCK) + N_EXPERTS
    P = n_blocks * MOE_BLOCK
    buf_tok = jnp.zeros((P,), jnp.int32).at[dest].set(st)
    buf_gate = jnp.zeros((P,), jnp.float32).at[dest].set(sg)
    block_start = jnp.arange(n_blocks, dtype=jnp.int32) * MOE_BLOCK
    block_expert = jnp.minimum(jnp.searchsorted(padded_end, block_start, side='right'), N_EXPERTS - 1)

    def expert_block(args):
        tok, e = args
        xb = h[tok]
        gu = jnp.dot(xb, w_gu[e]) + b_gu[e]
        g, u = jnp.split(gu, 2, axis=-1)
        g = jnp.minimum(g, SWIGLU_LIMIT)
        u = jnp.clip(u, -SWIGLU_LIMIT, SWIGLU_LIMIT)
        act = (u + 1.0) * (g * jax.nn.sigmoid(g * SWIGLU_ALPHA))
        return jnp.dot(act, w_down[e]) + b_down[e]

    ys = lax.map(expert_block, (buf_tok.reshape(n_blocks, MOE_BLOCK), block_expert))
    ys = ys.reshape(P, D) * buf_gate[:, None].astype(h.dtype)
    return jnp.zeros_like(h).at[buf_tok].add(ys)


def hybrid_mixer(h, hc, tables, last, w_in, q_norm_g, k_norm_g, pool_w, pool_scale, sgu_norm_g, sgu_w, sgu_b, w_out):
    B, S, _ = h.shape
    C = hc.shape[1]
    q, k, v, pin, su, sv = jnp.split(h @ w_in, SPLIT_IDX, axis=-1)
    q = apply_axial_rope(rms_norm(q.reshape(B, S, N_Q_HEADS, HEAD_DIM), q_norm_g), tables)
    k = apply_axial_rope(rms_norm(k.reshape(B, S, N_KV_HEADS, HEAD_DIM), k_norm_g), tables)
    v = v.reshape(B, S, N_KV_HEADS, HEAD_DIM)
    if last:
        ck, cv = jnp.split(hc @ w_in[:, ATTN_W:ATTN_W + 2 * KV_W], 2, axis=-1)
    else:
        cq, ck, cv, cpin, csu, csv = jnp.split(hc @ w_in, SPLIT_IDX, axis=-1)
    ck = rms_norm(ck.reshape(B, C, N_KV_HEADS, HEAD_DIM), k_norm_g)
    cv = cv.reshape(B, C, N_KV_HEADS, HEAD_DIM)
    y = jnp.concatenate([
        latent_attention(q, k, v, ck, cv),
        multiscale_pool(pin, pool_w, pool_scale),
        spatial_gating(gelu(su), gelu(sv), sgu_norm_g, sgu_w, sgu_b)], axis=-1) @ w_out
    if last:
        return y, None
    cq = rms_norm(cq.reshape(B, C, N_Q_HEADS, HEAD_DIM), q_norm_g)
    yc = jnp.concatenate([
        context_attention(cq, ck, cv),
        multiscale_pool(cpin, pool_w, pool_scale),
        spatial_gating(gelu(csu), gelu(csv), sgu_norm_g, sgu_w, sgu_b)], axis=-1) @ w_out
    return y, yc


def setup_inputs(seed: int = 0) -> dict:
    key = jax.random.key(seed)
    ks = jax.random.split(key, 24)
    f32 = jnp.float32
    D = D_MODEL
    L = DEPTH

    def nrm(k, shape, scale):
        return jax.random.normal(k, shape, f32) * scale

    return {
        'x': nrm(ks[0], (BATCH, SEQ, D), 1.0),
        'c': nrm(ks[1], (BATCH, D), 1.0),
        'ctx': nrm(ks[2], (BATCH, CTX_LEN, D), 1.0),
        'c_ctx': nrm(ks[3], (D,), 1.0),
        'ada_w': nrm(ks[4], (L, D, 6 * D), 0.5 * D ** -0.5),
        'ada_b': nrm(ks[5], (L, 6 * D), 0.02),
        'norm1_g': 1.0 + nrm(ks[6], (L, D), 0.05),
        'norm2_g': 1.0 + nrm(ks[7], (L, D), 0.05),
        'w_in': nrm(ks[8], (L, D, IN_W), D ** -0.5),
        'q_norm_g': 1.0 + nrm(ks[9], (L, HEAD_DIM), 0.05),
        'k_norm_g': 1.0 + nrm(ks[10], (L, HEAD_DIM), 0.05),
        'pool_w': nrm(ks[11], (L, N_POOL_GROUPS, POOL_GROUP_W, POOL_GROUP_W), POOL_GROUP_W ** -0.5),
        'pool_scale': 1.0 + nrm(ks[12], (L, POOL_W), 0.1),
        'sgu_norm_g': 1.0 + nrm(ks[13], (L, N_SGU_HEADS, SGU_HEAD_W), 0.05),
        'sgu_w': nrm(ks[14], (L, N_SGU_HEADS, CHUNK, CHUNK), CHUNK ** -0.5),
        'sgu_b': nrm(ks[15], (L, N_SGU_HEADS, CHUNK), 0.02),
        'w_out': nrm(ks[16], (L, MIX_W, D), MIX_W ** -0.5),
        'router_w': nrm(ks[17], (L, D, N_EXPERTS), D ** -0.5),
        'router_b': nrm(ks[18], (L, N_EXPERTS), 0.01),
        'w_gu': nrm(ks[19], (L, N_EXPERTS, D, 2 * EXPERT_FF), D ** -0.5),
        'b_gu': nrm(ks[20], (L, N_EXPERTS, 2 * EXPERT_FF), 0.02),
        'w_down': nrm(ks[21], (L, N_EXPERTS, EXPERT_FF, D), EXPERT_FF ** -0.5),
        'b_down': nrm(ks[22], (L, N_EXPERTS, D), 0.02),
    }


def reference(x, c, ctx, c_ctx, ada_w, ada_b, norm1_g, norm2_g, w_in, q_norm_g, k_norm_g, pool_w, pool_scale,
              sgu_norm_g, sgu_w, sgu_b, w_out, router_w, router_b, w_gu, b_gu, w_down, b_down):
    B, S, D = x.shape
    C = ctx.shape[1]
    rows = S // GRID_W
    tables = axial_rope_tables(rows)
    silu_c = jax.nn.silu(c)
    silu_cc = jax.nn.silu(c_ctx)
    for i in range(DEPTH):
        last = i == DEPTH - 1
        mod = silu_c @ ada_w[i] + ada_b[i]
        mod_c = silu_cc @ ada_w[i] + ada_b[i]
        sh1, sc1, g1, sh2, sc2, g2 = [m[:, None, :] for m in jnp.split(mod, 6, axis=-1)]
        csh1, csc1, cg1, csh2, csc2, cg2 = jnp.split(mod_c, 6, axis=-1)
        h = modulate(rms_norm(x, norm1_g[i]), sh1, sc1)
        hc = modulate(rms_norm(ctx, norm1_g[i]), csh1, csc1)
        y, yc = hybrid_mixer(h, hc, tables, last, w_in[i], q_norm_g[i], k_norm_g[i], pool_w[i], pool_scale[i],
                             sgu_norm_g[i], sgu_w[i], sgu_b[i], w_out[i])
        x = x + g1 * y
        h2 = modulate(rms_norm(x, norm2_g[i]), sh2, sc2)
        if last:
            x = x + g2 * moe_ffn(h2.reshape(B * S, D), router_w[i], router_b[i], w_gu[i], b_gu[i],
                                 w_down[i], b_down[i]).reshape(B, S, D)
        else:
            ctx = ctx + cg1 * yc
            hc2 = modulate(rms_norm(ctx, norm2_g[i]), csh2, csc2)
            tokens = jnp.concatenate([h2.reshape(B * S, D), hc2.reshape(B * C, D)], axis=0)
            out = moe_ffn(tokens, router_w[i], router_b[i], w_gu[i], b_gu[i], w_down[i], b_down[i])
            x = x + g2 * out[:B * S].reshape(B, S, D)
            ctx = ctx + cg2 * out[B * S:].reshape(B, C, D)
    return x
```

```python
import functools

import jax
import jax.numpy as jnp
from jax import lax
from jax.experimental import pallas as pl
from jax.experimental.pallas import tpu as pltpu

F32 = jnp.float32
BF16 = jnp.bfloat16

D_MODEL = 2048
BATCH = 4
SEQ = 4096
DEPTH = 2
GRID_W = 64
CTX_LEN = 256
HEAD_DIM = 128
ATTN_W = 1024
N_Q_HEADS = 8
GQA_GROUP = 4
N_KV_HEADS = 2
KV_W = 256
ROPE_THETA = 10000.0
ROPE_FREQS = 32
ATTN_SCALE = HEAD_DIM ** -0.5
POOL_WINDOWS = (2, 4, 8, 16)
POOL_W = 512
SGU_W = 512
N_SGU_HEADS = 4
CHUNK = 128
IN_W = 3072
N_EXPERTS = 32
TOP_K = 4
EXPERT_FF = 2048
SWIGLU_LIMIT = 7.0
SWIGLU_ALPHA = 1.702
EPS = 1e-6

N_LAT = BATCH * SEQ
N_CTX = BATCH * CTX_LEN
N_TOK = N_LAT + N_CTX
MOD_ROWS = 8
LANES = 128
POOL_HALO = 8
MIX_TM = 256
EXPERT_TM = 256
VMEM_LIMIT = 56 * 1024 * 1024


def _cparams(sem, vmem=VMEM_LIMIT):
    return pltpu.CompilerParams(dimension_semantics=sem, vmem_limit_bytes=vmem)


def _mod_row(row_tile, tm):
    return jnp.minimum(row_tile * tm // SEQ, BATCH)


def _rms(x, g):
    return x * lax.rsqrt(jnp.mean(x * x, axis=-1, keepdims=True) + EPS) * g


def _sigmoid(x):
    return 1.0 / (1.0 + jnp.exp(-x))


def _gelu(x):
    return 0.5 * x * (1.0 + lax.erf(x * 0.7071067811865476))


def _adaln_kernel(cc_ref, w_ref, b_ref, o_ref):
    cc = cc_ref[...]
    s = (cc * _sigmoid(cc)).astype(BF16)
    o_ref[...] = jnp.dot(s, w_ref[...].astype(BF16), preferred_element_type=F32) + b_ref[...]


def _adaln(cc, ada_w, ada_b):
    tn = 1024
    n = 6 * D_MODEL
    return pl.pallas_call(
        _adaln_kernel,
        grid=(DEPTH, n // tn),
        in_specs=[
            pl.BlockSpec((MOD_ROWS, D_MODEL), lambda l, j: (0, 0)),
            pl.BlockSpec((None, D_MODEL, tn), lambda l, j: (l, 0, j)),
            pl.BlockSpec((None, 1, tn), lambda l, j: (l, 0, j)),
        ],
        out_specs=pl.BlockSpec((None, MOD_ROWS, tn), lambda l, j: (l, 0, j)),
        out_shape=jax.ShapeDtypeStruct((DEPTH, MOD_ROWS, n), F32),
        compiler_params=_cparams(("arbitrary", "arbitrary")),
        name="adaln",
    )(cc, ada_w, ada_b.reshape(DEPTH, 1, n))


def _proj_kernel(x_ref, g_ref, sh_ref, sc_ref, w_ref, o_ref, h_ref):
    @pl.when(pl.program_id(1) == 0)
    def _():
        y = _rms(x_ref[...], g_ref[...])
        h_ref[...] = (y * (1.0 + sc_ref[...]) + sh_ref[...]).astype(BF16)

    o_ref[...] = jnp.dot(h_ref[...], w_ref[...], preferred_element_type=F32)


def _in_proj(tok, norm_g, mod3, w_bf):
    tm, tn = 512, 1024
    return pl.pallas_call(
        _proj_kernel,
        grid=(N_TOK // tm, IN_W // tn),
        in_specs=[
            pl.BlockSpec((tm, D_MODEL), lambda i, j: (i, 0)),
            pl.BlockSpec((1, D_MODEL), lambda i, j: (0, 0)),
            pl.BlockSpec((None, 1, D_MODEL), lambda i, j: (_mod_row(i, tm), 0, 0)),
            pl.BlockSpec((None, 1, D_MODEL), lambda i, j: (_mod_row(i, tm), 0, 1)),
            pl.BlockSpec((D_MODEL, tn), lambda i, j: (0, j)),
        ],
        out_specs=pl.BlockSpec((tm, tn), lambda i, j: (i, j)),
        out_shape=jax.ShapeDtypeStruct((N_TOK, IN_W), F32),
        scratch_shapes=[pltpu.VMEM((tm, D_MODEL), BF16)],
        compiler_params=_cparams(("arbitrary", "arbitrary")),
        name="in_proj",
    )(tok, norm_g.reshape(1, D_MODEL), mod3, mod3, w_bf)


def _rope(x, cos, sin):
    lane = lax.broadcasted_iota(jnp.int32, x.shape, 1)
    first = (lane % 64) < 32
    partner = jnp.where(first, pltpu.roll(x, 96, 1), pltpu.roll(x, 32, 1))
    return x * cos + partner * sin


def _softmax_pv(q, k, v):
    s = lax.dot_general(q, k, (((1,), (1,)), ((), ())), preferred_element_type=F32) * ATTN_SCALE
    m = jnp.max(s, axis=-1, keepdims=True)
    p = jnp.exp(s - m)
    l = jnp.sum(p, axis=-1, keepdims=True)
    return jnp.dot(p.astype(BF16), v, preferred_element_type=F32) / l


def _lat_attn_kernel(q_ref, kl_ref, vl_ref, kc_ref, vc_ref, cosq_ref, sinq_ref, cosk_ref, sink_ref,
                     qg_ref, kg_ref, o_ref, k_s, v_s):
    @pl.when(pl.program_id(2) == 0)
    def _():
        k_s[0:CTX_LEN, :] = _rms(kc_ref[...], kg_ref[...]).astype(BF16)
        kl = _rope(_rms(kl_ref[...], kg_ref[...]), cosk_ref[...], sink_ref[...])
        k_s[CTX_LEN:, :] = kl.astype(BF16)
        v_s[0:CTX_LEN, :] = vc_ref[...].astype(BF16)
        v_s[CTX_LEN:, :] = vl_ref[...].astype(BF16)

    for g in range(GQA_GROUP):
        lanes = slice(g * HEAD_DIM, (g + 1) * HEAD_DIM)
        q = _rope(_rms(q_ref[:, lanes], qg_ref[...]), cosq_ref[...], sinq_ref[...])
        o_ref[:, lanes] = _softmax_pv(q.astype(BF16), k_s[...], v_s[...]).astype(o_ref.dtype)


def _latent_attention(proj, cos, sin, q_g, k_g):
    tq = 128
    nq = SEQ // tq
    qw = GQA_GROUP * HEAD_DIM
    kcol = ATTN_W // HEAD_DIM
    vcol = (ATTN_W + KV_W) // HEAD_DIM
    ctx_blk = N_LAT // CTX_LEN
    return pl.pallas_call(
        _lat_attn_kernel,
        grid=(BATCH, N_KV_HEADS, nq),
        in_specs=[
            pl.BlockSpec((tq, qw), lambda b, h, i: (b * nq + i, h)),
            pl.BlockSpec((SEQ, HEAD_DIM), lambda b, h, i: (b, kcol + h)),
            pl.BlockSpec((SEQ, HEAD_DIM), lambda b, h, i: (b, vcol + h)),
            pl.BlockSpec((CTX_LEN, HEAD_DIM), lambda b, h, i: (ctx_blk + b, kcol + h)),
            pl.BlockSpec((CTX_LEN, HEAD_DIM), lambda b, h, i: (ctx_blk + b, vcol + h)),
            pl.BlockSpec((tq, HEAD_DIM), lambda b, h, i: (i, 0)),
            pl.BlockSpec((tq, HEAD_DIM), lambda b, h, i: (i, 0)),
            pl.BlockSpec((SEQ, HEAD_DIM), lambda b, h, i: (0, 0)),
            pl.BlockSpec((SEQ, HEAD_DIM), lambda b, h, i: (0, 0)),
            pl.BlockSpec((1, HEAD_DIM), lambda b, h, i: (0, 0)),
            pl.BlockSpec((1, HEAD_DIM), lambda b, h, i: (0, 0)),
        ],
        out_specs=pl.BlockSpec((tq, qw), lambda b, h, i: (b * nq + i, h)),
        out_shape=jax.ShapeDtypeStruct((N_LAT, ATTN_W), BF16),
        scratch_shapes=[pltpu.VMEM((CTX_LEN + SEQ, HEAD_DIM), BF16),
                        pltpu.VMEM((CTX_LEN + SEQ, HEAD_DIM), BF16)],
        compiler_params=_cparams(("arbitrary", "arbitrary", "arbitrary")),
        name="latent_attention",
    )(proj, proj, proj, proj, proj, cos, sin, cos, sin, q_g.reshape(1, HEAD_DIM), k_g.reshape(1, HEAD_DIM))


def _ctx_attn_kernel(q_ref, k_ref, v_ref, qg_ref, kg_ref, o_ref):
    k = _rms(k_ref[...], kg_ref[...]).astype(BF16)
    v = v_ref[...].astype(BF16)
    for g in range(GQA_GROUP):
        lanes = slice(g * HEAD_DIM, (g + 1) * HEAD_DIM)
        q = _rms(q_ref[:, lanes], qg_ref[...]).astype(BF16)
        o_ref[:, lanes] = _softmax_pv(q, k, v).astype(o_ref.dtype)


def _context_attention(proj, q_g, k_g):
    qw = GQA_GROUP * HEAD_DIM
    kcol = ATTN_W // HEAD_DIM
    vcol = (ATTN_W + KV_W) // HEAD_DIM
    ctx_blk = N_LAT // CTX_LEN
    return pl.pallas_call(
        _ctx_attn_kernel,
        grid=(BATCH, N_KV_HEADS),
        in_specs=[
            pl.BlockSpec((CTX_LEN, qw), lambda b, h: (ctx_blk + b, h)),
            pl.BlockSpec((CTX_LEN, HEAD_DIM), lambda b, h: (ctx_blk + b, kcol + h)),
            pl.BlockSpec((CTX_LEN, HEAD_DIM), lambda b, h: (ctx_blk + b, vcol + h)),
            pl.BlockSpec((1, HEAD_DIM), lambda b, h: (0, 0)),
            pl.BlockSpec((1, HEAD_DIM), lambda b, h: (0, 0)),
        ],
        out_specs=pl.BlockSpec((CTX_LEN, qw), lambda b, h: (b, h)),
        out_shape=jax.ShapeDtypeStruct((N_CTX, ATTN_W), BF16),
        compiler_params=_cparams(("arbitrary", "arbitrary")),
        name="context_attention",
    )(proj, proj, proj, q_g.reshape(1, HEAD_DIM), k_g.reshape(1, HEAD_DIM))


def _mixer_kernel(pin_ref, prev_ref, next_ref, su_ref, sv_ref, pw_ref, ps_ref, sg_ref, sw_ref, sb_ref,
                  o_ref, pad_ref):
    tm = MIX_TM
    i = pl.program_id(0)
    is_lat = i < N_LAT // tm
    pos0 = jnp.where(is_lat, (i % (SEQ // tm)) * tm, 0)
    seq_len = jnp.where(is_lat, SEQ, CTX_LEN)
    pad_ref[0:POOL_HALO, :] = jnp.where(pos0 == 0, 0.0, prev_ref[...])
    pad_ref[POOL_HALO:POOL_HALO + tm, :] = pin_ref[...]
    pad_ref[POOL_HALO + tm:, :] = jnp.where(pos0 + tm == seq_len, 0.0, next_ref[...])
    t = pos0 + lax.broadcasted_iota(jnp.int32, (tm, LANES), 0)
    for gi, w in enumerate(POOL_WINDOWS):
        lanes = slice(gi * LANES, (gi + 1) * LANES)
        acc = pad_ref[POOL_HALO - w // 2:POOL_HALO - w // 2 + tm, lanes]
        for d in range(-w // 2 + 1, w // 2):
            acc = acc + pad_ref[POOL_HALO + d:POOL_HALO + d + tm, lanes]
        cnt = (jnp.minimum(t + w // 2, seq_len) - jnp.maximum(t - w // 2, 0)).astype(F32)
        mixed = acc / cnt - pin_ref[:, lanes]
        y = jnp.dot(mixed.astype(BF16), pw_ref[gi], preferred_element_type=F32) * ps_ref[:, lanes]
        o_ref[:, lanes] = y.astype(o_ref.dtype)

    for h in range(N_SGU_HEADS):
        lanes = slice(h * LANES, (h + 1) * LANES)
        gu = _gelu(su_ref[:, lanes])
        vh = _rms(_gelu(sv_ref[:, lanes]), sg_ref[h:h + 1, :]).astype(BF16)
        for n in range(tm // CHUNK):
            rows = slice(n * CHUNK, (n + 1) * CHUNK)
            mixed = jnp.dot(sw_ref[h], vh[rows], preferred_element_type=F32) + sb_ref[h]
            o_ref[rows, POOL_W + h * LANES:POOL_W + (h + 1) * LANES] = (gu[rows] * mixed).astype(o_ref.dtype)


def _mixers(proj, pool_w_bf, pool_scale, sgu_norm_g, sgu_w_bf, sgu_b_full):
    tm = MIX_TM
    per_tile = tm // POOL_HALO
    last_halo = N_TOK // POOL_HALO - 1
    pcol = (ATTN_W + 2 * KV_W) // POOL_W
    return pl.pallas_call(
        _mixer_kernel,
        grid=(N_TOK // tm,),
        in_specs=[
            pl.BlockSpec((tm, POOL_W), lambda i: (i, pcol)),
            pl.BlockSpec((POOL_HALO, POOL_W), lambda i: (jnp.maximum(i * per_tile - 1, 0), pcol)),
            pl.BlockSpec((POOL_HALO, POOL_W), lambda i: (jnp.minimum((i + 1) * per_tile, last_halo), pcol)),
            pl.BlockSpec((tm, SGU_W), lambda i: (i, pcol + 1)),
            pl.BlockSpec((tm, SGU_W), lambda i: (i, pcol + 2)),
            pl.BlockSpec((len(POOL_WINDOWS), LANES, LANES), lambda i: (0, 0, 0)),
            pl.BlockSpec((1, POOL_W), lambda i: (0, 0)),
            pl.BlockSpec((N_SGU_HEADS, LANES), lambda i: (0, 0)),
            pl.BlockSpec((N_SGU_HEADS, CHUNK, CHUNK), lambda i: (0, 0, 0)),
            pl.BlockSpec((N_SGU_HEADS, CHUNK, LANES), lambda i: (0, 0, 0)),
        ],
        out_specs=pl.BlockSpec((tm, POOL_W + SGU_W), lambda i: (i, 0)),
        out_shape=jax.ShapeDtypeStruct((N_TOK, POOL_W + SGU_W), BF16),
        scratch_shapes=[pltpu.VMEM((tm + 2 * POOL_HALO, POOL_W), F32)],
        compiler_params=_cparams(("arbitrary",)),
        name="mixers",
    )(proj, proj, proj, proj, proj, pool_w_bf, pool_scale.reshape(1, POOL_W), sgu_norm_g, sgu_w_bf, sgu_b_full)


def _outproj_kernel(a_ref, m_ref, w1_ref, w2_ref, x_ref, g_ref, o_ref):
    y = jnp.dot(a_ref[...], w1_ref[...], preferred_element_type=F32)
    y = y + jnp.dot(m_ref[...], w2_ref[...], preferred_element_type=F32)
    o_ref[...] = x_ref[...] + g_ref[...] * y


def _out_proj(attn, mix, w_bf, tok, mod3, n_rows):
    tm, tn = 512, 1024
    nj = D_MODEL // tn
    return pl.pallas_call(
        _outproj_kernel,
        grid=(nj, n_rows // tm),
        in_specs=[
            pl.BlockSpec((tm, ATTN_W), lambda j, i: (i, 0)),
            pl.BlockSpec((tm, POOL_W + SGU_W), lambda j, i: (i, 0)),
            pl.BlockSpec((ATTN_W, tn), lambda j, i: (0, j)),
            pl.BlockSpec((POOL_W + SGU_W, tn), lambda j, i: (1, j)),
            pl.BlockSpec((tm, tn), lambda j, i: (i, j)),
            pl.BlockSpec((None, 1, tn), lambda j, i: (_mod_row(i, tm), 0, 2 * nj + j)),
        ],
        out_specs=pl.BlockSpec((tm, tn), lambda j, i: (i, j)),
        out_shape=jax.ShapeDtypeStruct((n_rows, D_MODEL), F32),
        compiler_params=_cparams(("arbitrary", "arbitrary")),
        name="out_proj",
    )(attn, mix, w_bf, w_bf, tok, mod3)


def _router_kernel(x_ref, g_ref, sh_ref, sc_ref, rw_ref, rb_ref, h_ref, idx_ref, gate_ref):
    h = _rms(x_ref[...], g_ref[...]) * (1.0 + sc_ref[...]) + sh_ref[...]
    h_ref[...] = h
    logits = jnp.dot(h, rw_ref[...], precision=lax.Precision.HIGHEST, preferred_element_type=F32)
    lt = (logits + rb_ref[...]).T[0:N_EXPERTS, :]
    expert = lax.broadcasted_iota(jnp.int32, lt.shape, 0).astype(F32)
    vals, idxs = [], []
    for _ in range(TOP_K):
        m = jnp.max(lt, axis=0, keepdims=True)
        idx = jnp.min(jnp.where(lt == m, expert, float(N_EXPERTS)), axis=0, keepdims=True)
        vals.append(m)
        idxs.append(idx)
        lt = jnp.where(expert == idx, -jnp.inf, lt)
    e = [jnp.exp(v - vals[0]) for v in vals]
    den = e[0] + e[1] + e[2] + e[3]
    for k in range(TOP_K):
        idx_ref[k:k + 1, :] = idxs[k].astype(jnp.int32)
        gate_ref[k:k + 1, :] = e[k] / den


def _router(tok, norm_g, mod3, rw_pad, rb_pad, n_rows):
    tm = 512
    return pl.pallas_call(
        _router_kernel,
        grid=(n_rows // tm,),
        in_specs=[
            pl.BlockSpec((tm, D_MODEL), lambda i: (i, 0)),
            pl.BlockSpec((1, D_MODEL), lambda i: (0, 0)),
            pl.BlockSpec((None, 1, D_MODEL), lambda i: (_mod_row(i, tm), 0, 3)),
            pl.BlockSpec((None, 1, D_MODEL), lambda i: (_mod_row(i, tm), 0, 4)),
            pl.BlockSpec((D_MODEL, LANES), lambda i: (0, 0)),
            pl.BlockSpec((1, LANES), lambda i: (0, 0)),
        ],
        out_specs=[
            pl.BlockSpec((tm, D_MODEL), lambda i: (i, 0)),
            pl.BlockSpec((TOP_K, tm), lambda i: (0, i)),
            pl.BlockSpec((TOP_K, tm), lambda i: (0, i)),
        ],
        out_shape=[
            jax.ShapeDtypeStruct((n_rows, D_MODEL), F32),
            jax.ShapeDtypeStruct((TOP_K, n_rows), jnp.int32),
            jax.ShapeDtypeStruct((TOP_K, n_rows), F32),
        ],
        compiler_params=_cparams(("arbitrary",)),
        name="router",
    )(tok, norm_g.reshape(1, D_MODEL), mod3, mod3, rw_pad, rb_pad)


def _gather_kernel(tok_ref, h_hbm, o_ref, buf, sem, *, rows):
    base = pl.program_id(0) * rows

    def issue(r, carry):
        t = tok_ref[base + r]
        pltpu.make_async_copy(h_hbm.at[pl.ds(t, 1), :], buf.at[pl.ds(r, 1), :], sem).start()
        return carry

    lax.fori_loop(0, rows, issue, 0)
    pltpu.make_async_copy(h_hbm.at[pl.ds(0, rows), :], buf, sem).wait()
    o_ref[...] = buf[...].astype(o_ref.dtype)


def _gather_rows(buf_tok, h2, n_pad):
    rows = 512
    return pl.pallas_call(
        functools.partial(_gather_kernel, rows=rows),
        grid_spec=pltpu.PrefetchScalarGridSpec(
            num_scalar_prefetch=1,
            grid=(n_pad // rows,),
            in_specs=[pl.BlockSpec(memory_space=pl.ANY)],
            out_specs=pl.BlockSpec((rows, D_MODEL), lambda i, tok: (i, 0)),
            scratch_shapes=[pltpu.VMEM((rows, D_MODEL), F32), pltpu.SemaphoreType.DMA(())],
        ),
        out_shape=jax.ShapeDtypeStruct((n_pad, D_MODEL), BF16),
        compiler_params=_cparams(("arbitrary",)),
        name="gather_rows",
    )(buf_tok, h2)


def _is_new_expert(meta_ref, i):
    prev = meta_ref[jnp.maximum(i - 1, 0)]
    return jnp.logical_or(i == 0, meta_ref[i] != prev)


def _expert_gu_kernel(meta_ref, x_ref, wg_ref, wu_ref, bg_ref, bu_ref, o_ref, wg_s, wu_s, *, n_blocks):
    i = pl.program_id(1)

    @pl.when(_is_new_expert(meta_ref, i))
    def _():
        wg_s[...] = wg_ref[...].astype(BF16)
        wu_s[...] = wu_ref[...].astype(BF16)

    active = i < meta_ref[n_blocks]

    @pl.when(active)
    def _():
        x = x_ref[...]
        g = jnp.dot(x, wg_s[...], preferred_element_type=F32) + bg_ref[...]
        u = jnp.dot(x, wu_s[...], preferred_element_type=F32) + bu_ref[...]
        g = jnp.minimum(g, SWIGLU_LIMIT)
        u = jnp.clip(u, -SWIGLU_LIMIT, SWIGLU_LIMIT)
        o_ref[...] = ((u + 1.0) * (g * _sigmoid(g * SWIGLU_ALPHA))).astype(o_ref.dtype)

    @pl.when(jnp.logical_not(active))
    def _():
        o_ref[...] = jnp.zeros_like(o_ref)


def _expert_gu(meta, xs, w_gu, b_gu, n_blocks):
    tm, tn = EXPERT_TM, 512
    nj = EXPERT_FF // tn
    return pl.pallas_call(
        functools.partial(_expert_gu_kernel, n_blocks=n_blocks),
        grid_spec=pltpu.PrefetchScalarGridSpec(
            num_scalar_prefetch=1,
            grid=(nj, n_blocks),
            in_specs=[
                pl.BlockSpec((tm, D_MODEL), lambda j, i, m: (i, 0)),
                pl.BlockSpec((None, D_MODEL, tn), lambda j, i, m: (m[i], 0, j)),
                pl.BlockSpec((None, D_MODEL, tn), lambda j, i, m: (m[i], 0, nj + j)),
                pl.BlockSpec((None, 1, tn), lambda j, i, m: (m[i], 0, j)),
                pl.BlockSpec((None, 1, tn), lambda j, i, m: (m[i], 0, nj + j)),
            ],
            out_specs=pl.BlockSpec((tm, tn), lambda j, i, m: (i, j)),
            scratch_shapes=[pltpu.VMEM((D_MODEL, tn), BF16), pltpu.VMEM((D_MODEL, tn), BF16)],
        ),
        out_shape=jax.ShapeDtypeStruct((n_blocks * tm, EXPERT_FF), BF16),
        compiler_params=_cparams(("arbitrary", "arbitrary")),
        name="expert_gu",
    )(meta, xs, w_gu, w_gu, b_gu.reshape(N_EXPERTS, 1, 2 * EXPERT_FF), b_gu.reshape(N_EXPERTS, 1, 2 * EXPERT_FF))


def _expert_down_kernel(meta_ref, a_ref, w_ref, b_ref, gate_ref, o_ref, w_s, *, n_blocks):
    i = pl.program_id(1)

    @pl.when(_is_new_expert(meta_ref, i))
    def _():
        w_s[...] = w_ref[...].astype(BF16)

    active = i < meta_ref[n_blocks]

    @pl.when(active)
    def _():
        y = jnp.dot(a_ref[...], w_s[...], preferred_element_type=F32) + b_ref[...]
        o_ref[...] = y * gate_ref[...]

    @pl.when(jnp.logical_not(active))
    def _():
        o_ref[...] = jnp.zeros_like(o_ref)


def _expert_down(meta, act, w_down, b_down, buf_gate, n_blocks):
    tm, tn = EXPERT_TM, 1024
    nj = D_MODEL // tn
    return pl.pallas_call(
        functools.partial(_expert_down_kernel, n_blocks=n_blocks),
        grid_spec=pltpu.PrefetchScalarGridSpec(
            num_scalar_prefetch=1,
            grid=(nj, n_blocks),
            in_specs=[
                pl.BlockSpec((tm, EXPERT_FF), lambda j, i, m: (i, 0)),
                pl.BlockSpec((None, EXPERT_FF, tn), lambda j, i, m: (m[i], 0, j)),
                pl.BlockSpec((None, 1, tn), lambda j, i, m: (m[i], 0, j)),
                pl.BlockSpec((tm, 1), lambda j, i, m: (i, 0)),
            ],
            out_specs=pl.BlockSpec((tm, tn), lambda j, i, m: (i, j)),
            scratch_shapes=[pltpu.VMEM((EXPERT_FF, tn), BF16)],
        ),
        out_shape=jax.ShapeDtypeStruct((n_blocks * tm, D_MODEL), F32),
        compiler_params=_cparams(("arbitrary", "arbitrary")),
        name="expert_down",
    )(meta, act, w_down, b_down.reshape(N_EXPERTS, 1, D_MODEL), buf_gate.reshape(-1, 1))


def _combine_kernel(dest_ref, y_hbm, x_ref, g_ref, o_ref, buf, sem, *, rows, n_rows):
    base = pl.program_id(0) * rows
    for k in range(TOP_K):
        def issue(r, carry, k=k):
            d = dest_ref[k * n_rows + base + r]
            pltpu.make_async_copy(y_hbm.at[pl.ds(d, 1), :], buf.at[k, pl.ds(r, 1), :], sem).start()
            return carry

        lax.fori_loop(0, rows, issue, 0)
    for k in range(TOP_K):
        pltpu.make_async_copy(y_hbm.at[pl.ds(0, rows), :], buf.at[k], sem).wait()
    moe = (buf[0] + buf[1]) + (buf[2] + buf[3])
    o_ref[...] = x_ref[...] + g_ref[...] * moe


def _combine(dest, ys, tok, mod3, n_rows):
    rows = 256
    return pl.pallas_call(
        functools.partial(_combine_kernel, rows=rows, n_rows=n_rows),
        grid_spec=pltpu.PrefetchScalarGridSpec(
            num_scalar_prefetch=1,
            grid=(n_rows // rows,),
            in_specs=[
                pl.BlockSpec(memory_space=pl.ANY),
                pl.BlockSpec((rows, D_MODEL), lambda i, d: (i, 0)),
                pl.BlockSpec((None, 1, D_MODEL), lambda i, d: (_mod_row(i, rows), 0, 5)),
            ],
            out_specs=pl.BlockSpec((rows, D_MODEL), lambda i, d: (i, 0)),
            scratch_shapes=[pltpu.VMEM((TOP_K, rows, D_MODEL), F32), pltpu.SemaphoreType.DMA(())],
        ),
        out_shape=jax.ShapeDtypeStruct((n_rows, D_MODEL), F32),
        compiler_params=_cparams(("arbitrary",)),
        name="combine",
    )(dest, ys, tok, mod3)


def _routing(top_idx, gates, n_rows):
    tm = EXPERT_TM
    n_assign = TOP_K * n_rows
    n_blocks = -(-n_assign // tm) + N_EXPERTS
    flat_e = top_idx.reshape(n_assign)
    flat_tok = jnp.tile(jnp.arange(n_rows, dtype=jnp.int32), TOP_K)
    onehot = (flat_e[:, None] == jnp.arange(N_EXPERTS, dtype=jnp.int32)[None, :]).astype(jnp.int32)
    cum = jnp.cumsum(onehot, axis=0)
    rank = jnp.take_along_axis(cum, flat_e[:, None], axis=1)[:, 0] - 1
    counts = cum[-1]
    padded = (counts + tm - 1) // tm * tm
    padded_end = jnp.cumsum(padded)
    dest = (padded_end - padded)[flat_e] + rank
    buf_tok = jnp.zeros((n_blocks * tm,), jnp.int32).at[dest].set(flat_tok)
    buf_gate = jnp.zeros((n_blocks * tm,), F32).at[dest].set(gates.reshape(n_assign))
    block_start = jnp.arange(n_blocks, dtype=jnp.int32) * tm
    block_expert = jnp.minimum(jnp.searchsorted(padded_end, block_start, side="right"), N_EXPERTS - 1)
    meta = jnp.concatenate([block_expert.astype(jnp.int32), (padded_end[-1:] // tm).astype(jnp.int32)])
    return dest.astype(jnp.int32), buf_tok, buf_gate, meta, n_blocks


def _rope_tables():
    rows = SEQ // GRID_W
    row = jnp.broadcast_to(jnp.arange(rows, dtype=F32)[:, None], (rows, GRID_W)).reshape(-1)
    col = jnp.broadcast_to(jnp.arange(GRID_W, dtype=F32)[None, :], (rows, GRID_W)).reshape(-1)
    inv_freq = ROPE_THETA ** (-jnp.arange(ROPE_FREQS, dtype=F32) / ROPE_FREQS)
    ang_r = row[:, None] * inv_freq
    ang_c = col[:, None] * inv_freq
    cos = jnp.concatenate([jnp.cos(ang_r), jnp.cos(ang_r), jnp.cos(ang_c), jnp.cos(ang_c)], axis=-1)
    sin = jnp.concatenate([-jnp.sin(ang_r), jnp.sin(ang_r), -jnp.sin(ang_c), jnp.sin(ang_c)], axis=-1)
    return cos, sin


def kernel(x, c, ctx, c_ctx, ada_w, ada_b, norm1_g, norm2_g, w_in, q_norm_g, k_norm_g, pool_w, pool_scale,
           sgu_norm_g, sgu_w, sgu_b, w_out, router_w, router_b, w_gu, b_gu, w_down, b_down):
    cos, sin = _rope_tables()
    tok = jnp.concatenate([x.reshape(N_LAT, D_MODEL), ctx.reshape(N_CTX, D_MODEL)], axis=0)
    cc = jnp.zeros((MOD_ROWS, D_MODEL), F32).at[:BATCH].set(c).at[BATCH].set(c_ctx)
    mod = _adaln(cc, ada_w, ada_b)
    for l in range(DEPTH):
        last = l == DEPTH - 1
        n_rows = N_LAT if last else N_TOK
        mod3 = mod[l].reshape(MOD_ROWS, 1, 6 * D_MODEL)
        proj = _in_proj(tok, norm1_g[l], mod3, w_in[l].astype(BF16))
        attn = _latent_attention(proj, cos, sin, q_norm_g[l], k_norm_g[l])
        if not last:
            attn = jnp.concatenate([attn, _context_attention(proj, q_norm_g[l], k_norm_g[l])], axis=0)
        sgu_b_full = jnp.broadcast_to(sgu_b[l][:, :, None], (N_SGU_HEADS, CHUNK, LANES))
        mix = _mixers(proj, pool_w[l].astype(BF16), pool_scale[l], sgu_norm_g[l], sgu_w[l].astype(BF16), sgu_b_full)
        tok = _out_proj(attn, mix, w_out[l].astype(BF16), tok, mod3, n_rows)
        rw_pad = jnp.zeros((D_MODEL, LANES), F32).at[:, :N_EXPERTS].set(router_w[l])
        rb_pad = jnp.zeros((1, LANES), F32).at[0, :N_EXPERTS].set(router_b[l])
        h2, top_idx, gates = _router(tok, norm2_g[l], mod3, rw_pad, rb_pad, n_rows)
        dest, buf_tok, buf_gate, meta, n_blocks = _routing(top_idx, gates, n_rows)
        xs = _gather_rows(buf_tok, h2, n_blocks * EXPERT_TM)
        act = _expert_gu(meta, xs, w_gu[l], b_gu[l], n_blocks)
        ys = _expert_down(meta, act, w_down[l], b_down[l], buf_gate, n_blocks)
        tok = _combine(dest, ys, tok, mod3, n_rows)
    return tok.reshape(BATCH, SEQ, D_MODEL)
```

```python
import functools

import jax
import jax.numpy as jnp
from jax import lax
from jax.experimental import pallas as pl
from jax.experimental.pallas import tpu as pltpu

F32 = jnp.float32
BF16 = jnp.bfloat16
U32 = jnp.uint32

D_MODEL = 2048
BATCH = 4
SEQ = 4096
DEPTH = 2
GRID_W = 64
CTX_LEN = 256
HEAD_DIM = 128
ATTN_W = 1024
N_Q_HEADS = 8
GQA_GROUP = 4
N_KV_HEADS = 2
KV_W = 256
ROPE_THETA = 10000.0
ROPE_FREQS = 32
ATTN_SCALE = HEAD_DIM ** -0.5
POOL_WINDOWS = (2, 4, 8, 16)
POOL_W = 512
SGU_W = 512
N_SGU_HEADS = 4
CHUNK = 128
IN_W = 3072
N_EXPERTS = 32
TOP_K = 4
EXPERT_FF = 2048
SWIGLU_LIMIT = 7.0
SWIGLU_ALPHA = 1.702
EPS = 1e-6

N_LAT = BATCH * SEQ
N_CTX = BATCH * CTX_LEN
N_TOK = N_LAT + N_CTX
MOD_ROWS = 8
LANES = 128
POOL_HALO = 8
MIX_TM = 256
EXPERT_TM = 256
ROUTER_TM = 512
HALF_D = D_MODEL // 2
VMEM_LIMIT = 56 * 1024 * 1024


def _cparams(sem, vmem=VMEM_LIMIT):
    return pltpu.CompilerParams(dimension_semantics=sem, vmem_limit_bytes=vmem)


def _mod_row(row_tile, tm):
    return jnp.minimum(row_tile * tm // SEQ, BATCH)


def _rms(x, g):
    return x * lax.rsqrt(jnp.mean(x * x, axis=-1, keepdims=True) + EPS) * g


def _sigmoid(x):
    return 1.0 / (1.0 + jnp.exp(-x))


def _gelu(x):
    return 0.5 * x * (1.0 + lax.erf(x * 0.7071067811865476))


def _adaln_kernel(cc_ref, w_ref, b_ref, o_ref):
    cc = cc_ref[...]
    s = (cc * _sigmoid(cc)).astype(BF16)
    o_ref[...] = jnp.dot(s, w_ref[...].astype(BF16), preferred_element_type=F32) + b_ref[...]


def _adaln(cc, ada_w, ada_b):
    tn = 1024
    n = 6 * D_MODEL
    return pl.pallas_call(
        _adaln_kernel,
        grid=(DEPTH, n // tn),
        in_specs=[
            pl.BlockSpec((MOD_ROWS, D_MODEL), lambda l, j: (0, 0)),
            pl.BlockSpec((None, D_MODEL, tn), lambda l, j: (l, 0, j)),
            pl.BlockSpec((None, 1, tn), lambda l, j: (l, 0, j)),
        ],
        out_specs=pl.BlockSpec((None, MOD_ROWS, tn), lambda l, j: (l, 0, j)),
        out_shape=jax.ShapeDtypeStruct((DEPTH, MOD_ROWS, n), F32),
        compiler_params=_cparams(("arbitrary", "arbitrary")),
        name="adaln",
    )(cc, ada_w, ada_b.reshape(DEPTH, 1, n))


def _proj_kernel(x_ref, g_ref, sh_ref, sc_ref, w_ref, o_ref, h_ref):
    @pl.when(pl.program_id(1) == 0)
    def _():
        y = _rms(x_ref[...], g_ref[...])
        h_ref[...] = (y * (1.0 + sc_ref[...]) + sh_ref[...]).astype(BF16)

    o_ref[...] = jnp.dot(h_ref[...], w_ref[...], preferred_element_type=F32)


def _in_proj(tok, norm_g, mod3, w_bf):
    tm, tn = 512, 1024
    return pl.pallas_call(
        _proj_kernel,
        grid=(N_TOK // tm, IN_W // tn),
        in_specs=[
            pl.BlockSpec((tm, D_MODEL), lambda i, j: (i, 0)),
            pl.BlockSpec((1, D_MODEL), lambda i, j: (0, 0)),
            pl.BlockSpec((None, 1, D_MODEL), lambda i, j: (_mod_row(i, tm), 0, 0)),
            pl.BlockSpec((None, 1, D_MODEL), lambda i, j: (_mod_row(i, tm), 0, 1)),
            pl.BlockSpec((D_MODEL, tn), lambda i, j: (0, j)),
        ],
        out_specs=pl.BlockSpec((tm, tn), lambda i, j: (i, j)),
        out_shape=jax.ShapeDtypeStruct((N_TOK, IN_W), F32),
        scratch_shapes=[pltpu.VMEM((tm, D_MODEL), BF16)],
        compiler_params=_cparams(("arbitrary", "arbitrary")),
        name="in_proj",
    )(tok, norm_g.reshape(1, D_MODEL), mod3, mod3, w_bf)


def _rope(x, cos, sin):
    lane = lax.broadcasted_iota(jnp.int32, x.shape, 1)
    first = (lane % 64) < 32
    partner = jnp.where(first, pltpu.roll(x, 96, 1), pltpu.roll(x, 32, 1))
    return x * cos + partner * sin


def _softmax_pv(q, k, v):
    s = lax.dot_general(q, k, (((1,), (1,)), ((), ())), preferred_element_type=F32) * ATTN_SCALE
    m = jnp.max(s, axis=-1, keepdims=True)
    p = jnp.exp(s - m)
    l = jnp.sum(p, axis=-1, keepdims=True)
    return jnp.dot(p.astype(BF16), v, preferred_element_type=F32) / l


def _lat_attn_kernel(q_ref, kl_ref, vl_ref, kc_ref, vc_ref, cosq_ref, sinq_ref, cosk_ref, sink_ref,
                     qg_ref, kg_ref, o_ref, k_s, v_s):
    @pl.when(pl.program_id(2) == 0)
    def _():
        k_s[0:CTX_LEN, :] = _rms(kc_ref[...], kg_ref[...]).astype(BF16)
        kl = _rope(_rms(kl_ref[...], kg_ref[...]), cosk_ref[...], sink_ref[...])
        k_s[CTX_LEN:, :] = kl.astype(BF16)
        v_s[0:CTX_LEN, :] = vc_ref[...].astype(BF16)
        v_s[CTX_LEN:, :] = vl_ref[...].astype(BF16)

    for g in range(GQA_GROUP):
        lanes = slice(g * HEAD_DIM, (g + 1) * HEAD_DIM)
        q = _rope(_rms(q_ref[:, lanes], qg_ref[...]), cosq_ref[...], sinq_ref[...])
        o_ref[:, lanes] = _softmax_pv(q.astype(BF16), k_s[...], v_s[...]).astype(o_ref.dtype)


def _latent_attention(proj, cos, sin, q_g, k_g):
    tq = 128
    nq = SEQ // tq
    qw = GQA_GROUP * HEAD_DIM
    kcol = ATTN_W // HEAD_DIM
    vcol = (ATTN_W + KV_W) // HEAD_DIM
    ctx_blk = N_LAT // CTX_LEN
    return pl.pallas_call(
        _lat_attn_kernel,
        grid=(BATCH, N_KV_HEADS, nq),
        in_specs=[
            pl.BlockSpec((tq, qw), lambda b, h, i: (b * nq + i, h)),
            pl.BlockSpec((SEQ, HEAD_DIM), lambda b, h, i: (b, kcol + h)),
            pl.BlockSpec((SEQ, HEAD_DIM), lambda b, h, i: (b, vcol + h)),
            pl.BlockSpec((CTX_LEN, HEAD_DIM), lambda b, h, i: (ctx_blk + b, kcol + h)),
            pl.BlockSpec((CTX_LEN, HEAD_DIM), lambda b, h, i: (ctx_blk + b, vcol + h)),
            pl.BlockSpec((tq, HEAD_DIM), lambda b, h, i: (i, 0)),
            pl.BlockSpec((tq, HEAD_DIM), lambda b, h, i: (i, 0)),
            pl.BlockSpec((SEQ, HEAD_DIM), lambda b, h, i: (0, 0)),
            pl.BlockSpec((SEQ, HEAD_DIM), lambda b, h, i: (0, 0)),
            pl.BlockSpec((1, HEAD_DIM), lambda b, h, i: (0, 0)),
            pl.BlockSpec((1, HEAD_DIM), lambda b, h, i: (0, 0)),
        ],
        out_specs=pl.BlockSpec((tq, qw), lambda b, h, i: (b * nq + i, h)),
        out_shape=jax.ShapeDtypeStruct((N_LAT, ATTN_W), BF16),
        scratch_shapes=[pltpu.VMEM((CTX_LEN + SEQ, HEAD_DIM), BF16),
                        pltpu.VMEM((CTX_LEN + SEQ, HEAD_DIM), BF16)],
        compiler_params=_cparams(("arbitrary", "arbitrary", "arbitrary")),
        name="latent_attention",
    )(proj, proj, proj, proj, proj, cos, sin, cos, sin, q_g.reshape(1, HEAD_DIM), k_g.reshape(1, HEAD_DIM))


def _ctx_attn_kernel(q_ref, k_ref, v_ref, qg_ref, kg_ref, o_ref):
    k = _rms(k_ref[...], kg_ref[...]).astype(BF16)
    v = v_ref[...].astype(BF16)
    for g in range(GQA_GROUP):
        lanes = slice(g * HEAD_DIM, (g + 1) * HEAD_DIM)
        q = _rms(q_ref[:, lanes], qg_ref[...]).astype(BF16)
        o_ref[:, lanes] = _softmax_pv(q, k, v).astype(o_ref.dtype)


def _context_attention(proj, q_g, k_g):
    qw = GQA_GROUP * HEAD_DIM
    kcol = ATTN_W // HEAD_DIM
    vcol = (ATTN_W + KV_W) // HEAD_DIM
    ctx_blk = N_LAT // CTX_LEN
    return pl.pallas_call(
        _ctx_attn_kernel,
        grid=(BATCH, N_KV_HEADS),
        in_specs=[
            pl.BlockSpec((CTX_LEN, qw), lambda b, h: (ctx_blk + b, h)),
            pl.BlockSpec((CTX_LEN, HEAD_DIM), lambda b, h: (ctx_blk + b, kcol + h)),
            pl.BlockSpec((CTX_LEN, HEAD_DIM), lambda b, h: (ctx_blk + b, vcol + h)),
            pl.BlockSpec((1, HEAD_DIM), lambda b, h: (0, 0)),
            pl.BlockSpec((1, HEAD_DIM), lambda b, h: (0, 0)),
        ],
        out_specs=pl.BlockSpec((CTX_LEN, qw), lambda b, h: (b, h)),
        out_shape=jax.ShapeDtypeStruct((N_CTX, ATTN_W), BF16),
        compiler_params=_cparams(("arbitrary", "arbitrary")),
        name="context_attention",
    )(proj, proj, proj, q_g.reshape(1, HEAD_DIM), k_g.reshape(1, HEAD_DIM))


def _mixer_kernel(pin_ref, prev_ref, next_ref, su_ref, sv_ref, pw_ref, ps_ref, sg_ref, sw_ref, sb_ref,
                  o_ref, pad_ref):
    tm = MIX_TM
    i = pl.program_id(0)
    is_lat = i < N_LAT // tm
    pos0 = jnp.where(is_lat, (i % (SEQ // tm)) * tm, 0)
    seq_len = jnp.where(is_lat, SEQ, CTX_LEN)
    pad_ref[0:POOL_HALO, :] = jnp.where(pos0 == 0, 0.0, prev_ref[...])
    pad_ref[POOL_HALO:POOL_HALO + tm, :] = pin_ref[...]
    pad_ref[POOL_HALO + tm:, :] = jnp.where(pos0 + tm == seq_len, 0.0, next_ref[...])
    t = pos0 + lax.broadcasted_iota(jnp.int32, (tm, LANES), 0)
    for gi, w in enumerate(POOL_WINDOWS):
        lanes = slice(gi * LANES, (gi + 1) * LANES)
        acc = pad_ref[POOL_HALO - w // 2:POOL_HALO - w // 2 + tm, lanes]
        for d in range(-w // 2 + 1, w // 2):
            acc = acc + pad_ref[POOL_HALO + d:POOL_HALO + d + tm, lanes]
        cnt = (jnp.minimum(t + w // 2, seq_len) - jnp.maximum(t - w // 2, 0)).astype(F32)
        mixed = acc / cnt - pin_ref[:, lanes]
        y = jnp.dot(mixed.astype(BF16), pw_ref[gi], preferred_element_type=F32) * ps_ref[:, lanes]
        o_ref[:, lanes] = y.astype(o_ref.dtype)

    for h in range(N_SGU_HEADS):
        lanes = slice(h * LANES, (h + 1) * LANES)
        gu = _gelu(su_ref[:, lanes])
        vh = _rms(_gelu(sv_ref[:, lanes]), sg_ref[h:h + 1, :]).astype(BF16)
        for n in range(tm // CHUNK):
            rows = slice(n * CHUNK, (n + 1) * CHUNK)
            mixed = jnp.dot(sw_ref[h], vh[rows], preferred_element_type=F32) + sb_ref[h]
            o_ref[rows, POOL_W + h * LANES:POOL_W + (h + 1) * LANES] = (gu[rows] * mixed).astype(o_ref.dtype)


def _mixers(proj, pool_w_bf, pool_scale, sgu_norm_g, sgu_w_bf, sgu_b_full):
    tm = MIX_TM
    per_tile = tm // POOL_HALO
    last_halo = N_TOK // POOL_HALO - 1
    pcol = (ATTN_W + 2 * KV_W) // POOL_W
    return pl.pallas_call(
        _mixer_kernel,
        grid=(N_TOK // tm,),
        in_specs=[
            pl.BlockSpec((tm, POOL_W), lambda i: (i, pcol)),
            pl.BlockSpec((POOL_HALO, POOL_W), lambda i: (jnp.maximum(i * per_tile - 1, 0), pcol)),
            pl.BlockSpec((POOL_HALO, POOL_W), lambda i: (jnp.minimum((i + 1) * per_tile, last_halo), pcol)),
            pl.BlockSpec((tm, SGU_W), lambda i: (i, pcol + 1)),
            pl.BlockSpec((tm, SGU_W), lambda i: (i, pcol + 2)),
            pl.BlockSpec((len(POOL_WINDOWS), LANES, LANES), lambda i: (0, 0, 0)),
            pl.BlockSpec((1, POOL_W), lambda i: (0, 0)),
            pl.BlockSpec((N_SGU_HEADS, LANES), lambda i: (0, 0)),
            pl.BlockSpec((N_SGU_HEADS, CHUNK, CHUNK), lambda i: (0, 0, 0)),
            pl.BlockSpec((N_SGU_HEADS, CHUNK, LANES), lambda i: (0, 0, 0)),
        ],
        out_specs=pl.BlockSpec((tm, POOL_W + SGU_W), lambda i: (i, 0)),
        out_shape=jax.ShapeDtypeStruct((N_TOK, POOL_W + SGU_W), BF16),
        scratch_shapes=[pltpu.VMEM((tm + 2 * POOL_HALO, POOL_W), F32)],
        compiler_params=_cparams(("arbitrary",)),
        name="mixers",
    )(proj, proj, proj, proj, proj, pool_w_bf, pool_scale.reshape(1, POOL_W), sgu_norm_g, sgu_w_bf, sgu_b_full)


def _outproj_kernel(a_ref, m_ref, w1_ref, w2_ref, x_ref, g_ref, o_ref):
    y = jnp.dot(a_ref[...], w1_ref[...], preferred_element_type=F32)
    y = y + jnp.dot(m_ref[...], w2_ref[...], preferred_element_type=F32)
    o_ref[...] = x_ref[...] + g_ref[...] * y


def _out_proj(attn, mix, w_bf, tok, mod3, n_rows):
    tm, tn = 512, 1024
    nj = D_MODEL // tn
    return pl.pallas_call(
        _outproj_kernel,
        grid=(nj, n_rows // tm),
        in_specs=[
            pl.BlockSpec((tm, ATTN_W), lambda j, i: (i, 0)),
            pl.BlockSpec((tm, POOL_W + SGU_W), lambda j, i: (i, 0)),
            pl.BlockSpec((ATTN_W, tn), lambda j, i: (0, j)),
            pl.BlockSpec((POOL_W + SGU_W, tn), lambda j, i: (1, j)),
            pl.BlockSpec((tm, tn), lambda j, i: (i, j)),
            pl.BlockSpec((None, 1, tn), lambda j, i: (_mod_row(i, tm), 0, 2 * nj + j)),
        ],
        out_specs=pl.BlockSpec((tm, tn), lambda j, i: (i, j)),
        out_shape=jax.ShapeDtypeStruct((n_rows, D_MODEL), F32),
        compiler_params=_cparams(("arbitrary", "arbitrary")),
        name="out_proj",
    )(attn, mix, w_bf, w_bf, tok, mod3)


def _router_kernel(x_ref, g_ref, sh_ref, sc_ref, rw_ref, rb_ref, tri_ref,
                   hp_ref, idx_ref, rank_ref, gate_ref, cnt_ref, run_s):
    @pl.when(pl.program_id(0) == 0)
    def _():
        run_s[...] = jnp.zeros_like(run_s)

    h = _rms(x_ref[...], g_ref[...]) * (1.0 + sc_ref[...]) + sh_ref[...]
    hb = h.astype(BF16).astype(F32)
    hi = lax.bitcast_convert_type(hb[:, :HALF_D], U32)
    lo = lax.bitcast_convert_type(hb[:, HALF_D:], U32)
    hp_ref[...] = hi | (lo >> 16)

    logits = jnp.dot(h, rw_ref[...], precision=lax.Precision.HIGHEST, preferred_element_type=F32)
    lt = (logits + rb_ref[...]).T[0:N_EXPERTS, :]
    expert = lax.broadcasted_iota(jnp.int32, lt.shape, 0).astype(F32)
    vals, idxs = [], []
    for _ in range(TOP_K):
        m = jnp.max(lt, axis=0, keepdims=True)
        idx = jnp.min(jnp.where(lt == m, expert, float(N_EXPERTS)), axis=0, keepdims=True)
        vals.append(m)
        idxs.append(idx)
        lt = jnp.where(expert == idx, -jnp.inf, lt)

    e = [jnp.exp(v - vals[0]) for v in vals]
    den = e[0] + e[1] + e[2] + e[3]
    slot = lax.broadcasted_iota(jnp.int32, (LANES, lt.shape[1]), 0)
    gates = jnp.zeros((LANES, lt.shape[1]), F32)
    for k in range(TOP_K):
        gates = jnp.where(slot == k, e[k] / den, gates)
    gate_ref[...] = gates.T

    base = run_s[...]
    for k in range(TOP_K):
        onehot = jnp.where(expert == idxs[k], 1.0, 0.0)
        before = jnp.dot(onehot.astype(BF16), tri_ref[...], preferred_element_type=F32)
        rank = jnp.sum(onehot * (before + base[:, 0:1]), axis=0, keepdims=True)
        idx_ref[k:k + 1, :] = idxs[k].astype(jnp.int32)
        rank_ref[k:k + 1, :] = rank.astype(jnp.int32)
        base = base + jnp.sum(onehot, axis=1, keepdims=True)
    run_s[...] = base
    cnt_ref[...] = base.astype(jnp.int32)


def _router(tok, norm_g, mod3, rw_pad, rb_pad, tri, n_rows):
    tm = ROUTER_TM
    return pl.pallas_call(
        _router_kernel,
        grid=(n_rows // tm,),
        in_specs=[
            pl.BlockSpec((tm, D_MODEL), lambda i: (i, 0)),
            pl.BlockSpec((1, D_MODEL), lambda i: (0, 0)),
            pl.BlockSpec((None, 1, D_MODEL), lambda i: (_mod_row(i, tm), 0, 3)),
            pl.BlockSpec((None, 1, D_MODEL), lambda i: (_mod_row(i, tm), 0, 4)),
            pl.BlockSpec((D_MODEL, LANES), lambda i: (0, 0)),
            pl.BlockSpec((1, LANES), lambda i: (0, 0)),
            pl.BlockSpec((tm, tm), lambda i: (0, 0)),
        ],
        out_specs=[
            pl.BlockSpec((tm, HALF_D), lambda i: (i, 0)),
            pl.BlockSpec((TOP_K, tm), lambda i: (0, i)),
            pl.BlockSpec((TOP_K, tm), lambda i: (0, i)),
            pl.BlockSpec((tm, LANES), lambda i: (i, 0)),
            pl.BlockSpec((N_EXPERTS, LANES), lambda i: (0, 0)),
        ],
        out_shape=[
            jax.ShapeDtypeStruct((n_rows, HALF_D), U32),
            jax.ShapeDtypeStruct((TOP_K, n_rows), jnp.int32),
            jax.ShapeDtypeStruct((TOP_K, n_rows), jnp.int32),
            jax.ShapeDtypeStruct((n_rows, LANES), F32),
            jax.ShapeDtypeStruct((N_EXPERTS, LANES), jnp.int32),
        ],
        scratch_shapes=[pltpu.VMEM((N_EXPERTS, LANES), F32)],
        compiler_params=_cparams(("arbitrary",)),
        name="router",
    )(tok, norm_g.reshape(1, D_MODEL), mod3, mod3, rw_pad, rb_pad, tri)


def _row_copy(src_hbm, s, dst_hbm, d, sem):
    return pltpu.make_async_copy(src_hbm.at[pl.ds(s, 1), :], dst_hbm.at[pl.ds(d, 1), :], sem)


def _dispatch_kernel(dest_ref, pad_ref, h_hbm, xs_hbm, sem, pad_sem, *, rows, n_rows):
    i = pl.program_id(0)

    @pl.when(i == 0)
    def _():
        def per_expert(e, carry):
            lo, hi = pad_ref[e], pad_ref[N_EXPERTS + 1 + e]

            def start(p, c):
                _row_copy(h_hbm, 0, xs_hbm, p, pad_sem).start()
                return c

            def wait(p, c):
                _row_copy(h_hbm, 0, xs_hbm, p, pad_sem).wait()
                return c

            lax.fori_loop(lo, hi, start, 0)
            lax.fori_loop(lo, hi, wait, 0)
            return carry

        lax.fori_loop(0, N_EXPERTS + 1, per_expert, 0)

    base = i * rows

    def issue(r, carry):
        t = base + r
        for k in range(TOP_K):
            _row_copy(h_hbm, t, xs_hbm, dest_ref[k * n_rows + t], sem).start()
        return carry

    lax.fori_loop(0, rows, issue, 0)

    def wait_one_step():
        n = TOP_K * rows
        pltpu.make_async_copy(h_hbm.at[pl.ds(0, n), :], xs_hbm.at[pl.ds(0, n), :], sem).wait()

    @pl.when(i > 0)
    def _():
        wait_one_step()

    @pl.when(i == pl.num_programs(0) - 1)
    def _():
        wait_one_step()


def _dispatch(dest_flat, pad_meta, hp, n_pad, n_rows):
    rows = 256
    return pl.pallas_call(
        functools.partial(_dispatch_kernel, rows=rows, n_rows=n_rows),
        grid_spec=pltpu.PrefetchScalarGridSpec(
            num_scalar_prefetch=2,
            grid=(n_rows // rows,),
            in_specs=[pl.BlockSpec(memory_space=pl.ANY)],
            out_specs=pl.BlockSpec(memory_space=pl.ANY),
            scratch_shapes=[pltpu.SemaphoreType.DMA(()), pltpu.SemaphoreType.DMA(())],
        ),
        out_shape=jax.ShapeDtypeStruct((n_pad, HALF_D), U32),
        compiler_params=_cparams(("arbitrary",)),
        name="dispatch",
    )(dest_flat, pad_meta, hp)


def _is_new_expert(meta_ref, i):
    prev = meta_ref[jnp.maximum(i - 1, 0)]
    return jnp.logical_or(i == 0, meta_ref[i] != prev)


def _unpack_rows(xp):
    hi = lax.bitcast_convert_type(xp & jnp.uint32(0xFFFF0000), F32).astype(BF16)
    lo = lax.bitcast_convert_type(xp << 16, F32).astype(BF16)
    return jnp.concatenate([hi, lo], axis=1)


def _expert_gu_kernel(meta_ref, x_ref, wg_ref, wu_ref, bg_ref, bu_ref, o_ref, wg_s, wu_s, *, n_blocks):
    i = pl.program_id(1)

    @pl.when(_is_new_expert(meta_ref, i))
    def _():
        wg_s[...] = wg_ref[...].astype(BF16)
        wu_s[...] = wu_ref[...].astype(BF16)

    active = i < meta_ref[n_blocks]

    @pl.when(active)
    def _():
        x = _unpack_rows(x_ref[...])
        g = jnp.dot(x, wg_s[...], preferred_element_type=F32) + bg_ref[...]
        u = jnp.dot(x, wu_s[...], preferred_element_type=F32) + bu_ref[...]
        g = jnp.minimum(g, SWIGLU_LIMIT)
        u = jnp.clip(u, -SWIGLU_LIMIT, SWIGLU_LIMIT)
        o_ref[...] = ((u + 1.0) * (g * _sigmoid(g * SWIGLU_ALPHA))).astype(o_ref.dtype)

    @pl.when(jnp.logical_not(active))
    def _():
        o_ref[...] = jnp.zeros_like(o_ref)


def _active_block(i, m, n_blocks):
    return jnp.minimum(i, m[n_blocks] - 1)


def _expert_gu(meta, xs, w_gu, b_gu, layer, n_blocks):
    tm, tn = EXPERT_TM, 1024
    nj = EXPERT_FF // tn
    b3 = b_gu.reshape(DEPTH, N_EXPERTS, 1, 2 * EXPERT_FF)
    return pl.pallas_call(
        functools.partial(_expert_gu_kernel, n_blocks=n_blocks),
        grid_spec=pltpu.PrefetchScalarGridSpec(
            num_scalar_prefetch=1,
            grid=(nj, n_blocks),
            in_specs=[
                pl.BlockSpec((tm, HALF_D), lambda j, i, m: (_active_block(i, m, n_blocks), 0)),
                pl.BlockSpec((None, None, D_MODEL, tn), lambda j, i, m: (layer, m[i], 0, j)),
                pl.BlockSpec((None, None, D_MODEL, tn), lambda j, i, m: (layer, m[i], 0, nj + j)),
                pl.BlockSpec((None, None, 1, tn), lambda j, i, m: (layer, m[i], 0, j)),
                pl.BlockSpec((None, None, 1, tn), lambda j, i, m: (layer, m[i], 0, nj + j)),
            ],
            out_specs=pl.BlockSpec((tm, tn), lambda j, i, m: (i, j)),
            scratch_shapes=[pltpu.VMEM((D_MODEL, tn), BF16), pltpu.VMEM((D_MODEL, tn), BF16)],
        ),
        out_shape=jax.ShapeDtypeStruct((n_blocks * tm, EXPERT_FF), BF16),
        compiler_params=_cparams(("arbitrary", "arbitrary")),
        name="expert_gu",
    )(meta, xs, w_gu, w_gu, b3, b3)


def _expert_down_kernel(meta_ref, a_ref, w_ref, b_ref, o_ref, w_s, *, n_blocks):
    i = pl.program_id(0)

    @pl.when(_is_new_expert(meta_ref, i))
    def _():
        w_s[...] = w_ref[...].astype(BF16)

    active = i < meta_ref[n_blocks]

    @pl.when(active)
    def _():
        o_ref[...] = jnp.dot(a_ref[...], w_s[...], preferred_element_type=F32) + b_ref[...]

    @pl.when(jnp.logical_not(active))
    def _():
        o_ref[...] = jnp.zeros_like(o_ref)


def _expert_down(meta, act, w_down, b_down, layer, n_blocks):
    tm = EXPERT_TM
    b3 = b_down.reshape(DEPTH, N_EXPERTS, 1, D_MODEL)
    return pl.pallas_call(
        functools.partial(_expert_down_kernel, n_blocks=n_blocks),
        grid_spec=pltpu.PrefetchScalarGridSpec(
            num_scalar_prefetch=1,
            grid=(n_blocks,),
            in_specs=[
                pl.BlockSpec((tm, EXPERT_FF), lambda i, m: (_active_block(i, m, n_blocks), 0)),
                pl.BlockSpec((None, None, EXPERT_FF, D_MODEL), lambda i, m: (layer, m[i], 0, 0)),
                pl.BlockSpec((None, None, 1, D_MODEL), lambda i, m: (layer, m[i], 0, 0)),
            ],
            out_specs=pl.BlockSpec((tm, D_MODEL), lambda i, m: (i, 0)),
            scratch_shapes=[pltpu.VMEM((EXPERT_FF, D_MODEL), BF16)],
        ),
        out_shape=jax.ShapeDtypeStruct((n_blocks * tm, D_MODEL), F32),
        compiler_params=_cparams(("arbitrary",)),
        name="expert_down",
    )(meta, act, w_down, b3)


def _combine_kernel(dest_ref, y_hbm, gt_ref, x_ref, g_ref, o_ref, buf, sems, *, rows, n_rows):
    i = pl.program_id(0)
    n_steps = pl.num_programs(0)

    def issue(step, slot):
        base = step * rows

        def body(r, carry):
            for k in range(TOP_K):
                d = dest_ref[k * n_rows + base + r]
                pltpu.make_async_copy(y_hbm.at[pl.ds(d, 1), :], buf.at[slot, k, pl.ds(r, 1), :],
                                      sems.at[slot]).start()
            return carry

        lax.fori_loop(0, rows, body, 0)

    @pl.when(i == 0)
    def _():
        issue(0, 0)

    @pl.when(i + 1 < n_steps)
    def _():
        issue(i + 1, (i + 1) % 2)

    slot = i % 2
    for k in range(TOP_K):
        pltpu.make_async_copy(y_hbm.at[pl.ds(0, rows), :], buf.at[slot, k], sems.at[slot]).wait()
    gt = gt_ref[...]
    moe = gt[:, 0:1] * buf[slot, 0]
    for k in range(1, TOP_K):
        moe = moe + gt[:, k:k + 1] * buf[slot, k]
    o_ref[...] = x_ref[...] + g_ref[...] * moe


def _combine(dest_flat, ys, gates_t, tok, mod3, n_rows):
    rows = 128
    return pl.pallas_call(
        functools.partial(_combine_kernel, rows=rows, n_rows=n_rows),
        grid_spec=pltpu.PrefetchScalarGridSpec(
            num_scalar_prefetch=1,
            grid=(n_rows // rows,),
            in_specs=[
                pl.BlockSpec(memory_space=pl.ANY),
                pl.BlockSpec((rows, LANES), lambda i, d: (i, 0)),
                pl.BlockSpec((rows, D_MODEL), lambda i, d: (i, 0)),
                pl.BlockSpec((None, 1, D_MODEL), lambda i, d: (_mod_row(i, rows), 0, 5)),
            ],
            out_specs=pl.BlockSpec((rows, D_MODEL), lambda i, d: (i, 0)),
            scratch_shapes=[pltpu.VMEM((2, TOP_K, rows, D_MODEL), F32), pltpu.SemaphoreType.DMA((2,))],
        ),
        out_shape=jax.ShapeDtypeStruct((n_rows, D_MODEL), F32),
        compiler_params=_cparams(("arbitrary",)),
        name="combine",
    )(dest_flat, ys, gates_t, tok, mod3)


def _routing(top_idx, rank, counts, n_rows):
    tm = EXPERT_TM
    n_blocks = -(-(TOP_K * n_rows) // tm) + N_EXPERTS
    counts = counts[:, 0]
    padded = (counts + tm - 1) // tm * tm
    padded_end = jnp.cumsum(padded)
    padded_start = padded_end - padded
    experts = jnp.arange(N_EXPERTS, dtype=jnp.int32)
    start_of = jnp.sum(jnp.where(top_idx[:, :, None] == experts, padded_start, 0), axis=-1)
    dest = (start_of + rank).astype(jnp.int32).reshape(TOP_K * n_rows)
    block_start = jnp.arange(n_blocks, dtype=jnp.int32) * tm
    block_expert = jnp.minimum(jnp.sum(padded_end[None, :] <= block_start[:, None], axis=1), N_EXPERTS - 1)
    meta = jnp.concatenate([block_expert, padded_end[-1:] // tm]).astype(jnp.int32)
    n_pad = jnp.full((1,), n_blocks * tm, jnp.int32)
    pad_meta = jnp.concatenate([padded_start + counts, padded_end[-1:], padded_end, n_pad]).astype(jnp.int32)
    return dest, meta, pad_meta, n_blocks


def _rope_tables():
    rows = SEQ // GRID_W
    row = jnp.broadcast_to(jnp.arange(rows, dtype=F32)[:, None], (rows, GRID_W)).reshape(-1)
    col = jnp.broadcast_to(jnp.arange(GRID_W, dtype=F32)[None, :], (rows, GRID_W)).reshape(-1)
    inv_freq = ROPE_THETA ** (-jnp.arange(ROPE_FREQS, dtype=F32) / ROPE_FREQS)
    ang_r = row[:, None] * inv_freq
    ang_c = col[:, None] * inv_freq
    cos = jnp.concatenate([jnp.cos(ang_r), jnp.cos(ang_r), jnp.cos(ang_c), jnp.cos(ang_c)], axis=-1)
    sin = jnp.concatenate([-jnp.sin(ang_r), jnp.sin(ang_r), -jnp.sin(ang_c), jnp.sin(ang_c)], axis=-1)
    return cos, sin


def kernel(x, c, ctx, c_ctx, ada_w, ada_b, norm1_g, norm2_g, w_in, q_norm_g, k_norm_g, pool_w, pool_scale,
           sgu_norm_g, sgu_w, sgu_b, w_out, router_w, router_b, w_gu, b_gu, w_down, b_down):
    cos, sin = _rope_tables()
    tok = jnp.concatenate([x.reshape(N_LAT, D_MODEL), ctx.reshape(N_CTX, D_MODEL)], axis=0)
    cc = jnp.zeros((MOD_ROWS, D_MODEL), F32).at[:BATCH].set(c).at[BATCH].set(c_ctx)
    mod = _adaln(cc, ada_w, ada_b)
    tri = jnp.triu(jnp.ones((ROUTER_TM, ROUTER_TM), BF16), k=1)
    for l in range(DEPTH):
        last = l == DEPTH - 1
        n_rows = N_LAT if last else N_TOK
        mod3 = mod[l].reshape(MOD_ROWS, 1, 6 * D_MODEL)
        proj = _in_proj(tok, norm1_g[l], mod3, w_in[l].astype(BF16))
        attn = _latent_attention(proj, cos, sin, q_norm_g[l], k_norm_g[l])
        if not last:
            attn = jnp.concatenate([attn, _context_attention(proj, q_norm_g[l], k_norm_g[l])], axis=0)
        sgu_b_full = jnp.broadcast_to(sgu_b[l][:, :, None], (N_SGU_HEADS, CHUNK, LANES))
        mix = _mixers(proj, pool_w[l].astype(BF16), pool_scale[l], sgu_norm_g[l], sgu_w[l].astype(BF16), sgu_b_full)
        tok = _out_proj(attn, mix, w_out[l].astype(BF16), tok, mod3, n_rows)
        rw_pad = jnp.zeros((D_MODEL, LANES), F32).at[:, :N_EXPERTS].set(router_w[l])
        rb_pad = jnp.zeros((1, LANES), F32).at[0, :N_EXPERTS].set(router_b[l])
        hp, top_idx, rank, gates_t, counts = _router(tok, norm2_g[l], mod3, rw_pad, rb_pad, tri, n_rows)
        dest, meta, pad_meta, n_blocks = _routing(top_idx, rank, counts, n_rows)
        xs = _dispatch(dest, pad_meta, hp, n_blocks * EXPERT_TM, n_rows)
        act = _expert_gu(meta, xs, w_gu, b_gu, l, n_blocks)
        ys = _expert_down(meta, act, w_down, b_down, l, n_blocks)
        tok = _combine(dest, ys, gates_t, tok, mod3, n_rows)
    return tok.reshape(BATCH, SEQ, D_MODEL)
```

```python
import functools

import jax
import jax.numpy as jnp
from jax import lax
from jax.experimental import pallas as pl
from jax.experimental.pallas import tpu as pltpu

F32 = jnp.float32
BF16 = jnp.bfloat16
U32 = jnp.uint32

D_MODEL = 2048
BATCH = 4
SEQ = 4096
DEPTH = 2
GRID_W = 64
CTX_LEN = 256
HEAD_DIM = 128
ATTN_W = 1024
N_Q_HEADS = 8
GQA_GROUP = 4
N_KV_HEADS = 2
KV_W = 256
ROPE_THETA = 10000.0
ROPE_FREQS = 32
ATTN_SCALE = HEAD_DIM ** -0.5
POOL_WINDOWS = (2, 4, 8, 16)
POOL_W = 512
SGU_W = 512
N_SGU_HEADS = 4
CHUNK = 128
IN_W = 3072
N_EXPERTS = 32
TOP_K = 4
EXPERT_FF = 2048
SWIGLU_LIMIT = 7.0
SWIGLU_ALPHA = 1.702
EPS = 1e-6

N_LAT = BATCH * SEQ
N_CTX = BATCH * CTX_LEN
N_TOK = N_LAT + N_CTX
MOD_ROWS = 8
LANES = 128
POOL_HALO = 8
MIX_TM = 256
EXPERT_TM = 256
ROUTER_TM = 512
HALF_D = D_MODEL // 2
PACK_TILES = HALF_D // LANES
VMEM_LIMIT = 56 * 1024 * 1024


def _cparams(sem, vmem=VMEM_LIMIT):
    return pltpu.CompilerParams(dimension_semantics=sem, vmem_limit_bytes=vmem)


def _mod_row(row_tile, tm):
    return jnp.minimum(row_tile * tm // SEQ, BATCH)


def _rms(x, g):
    return x * lax.rsqrt(jnp.mean(x * x, axis=-1, keepdims=True) + EPS) * g


def _sigmoid(x):
    return 1.0 / (1.0 + jnp.exp(-x))


def _gelu(x):
    return 0.5 * x * (1.0 + lax.erf(x * 0.7071067811865476))


def _adaln_kernel(cc_ref, w_ref, b_ref, o_ref):
    cc = cc_ref[...]
    s = (cc * _sigmoid(cc)).astype(BF16)
    o_ref[...] = jnp.dot(s, w_ref[...].astype(BF16), preferred_element_type=F32) + b_ref[...]


def _adaln(cc, ada_w, ada_b):
    tn = 1024
    n = 6 * D_MODEL
    return pl.pallas_call(
        _adaln_kernel,
        grid=(DEPTH, n // tn),
        in_specs=[
            pl.BlockSpec((MOD_ROWS, D_MODEL), lambda l, j: (0, 0)),
            pl.BlockSpec((None, D_MODEL, tn), lambda l, j: (l, 0, j)),
            pl.BlockSpec((None, 1, tn), lambda l, j: (l, 0, j)),
        ],
        out_specs=pl.BlockSpec((None, MOD_ROWS, tn), lambda l, j: (l, 0, j)),
        out_shape=jax.ShapeDtypeStruct((DEPTH, MOD_ROWS, n), F32),
        compiler_params=_cparams(("arbitrary", "arbitrary")),
        name="adaln",
    )(cc, ada_w, ada_b.reshape(DEPTH, 1, n))


def _proj_kernel(x_ref, g_ref, sh_ref, sc_ref, w_ref, o_ref, h_ref):
    @pl.when(pl.program_id(1) == 0)
    def _():
        y = _rms(x_ref[...], g_ref[...])
        h_ref[...] = (y * (1.0 + sc_ref[...]) + sh_ref[...]).astype(BF16)

    o_ref[...] = jnp.dot(h_ref[...], w_ref[...], preferred_element_type=F32)


def _in_proj(tok, norm_g, mod3, w_bf):
    tm, tn = 512, 1024
    return pl.pallas_call(
        _proj_kernel,
        grid=(N_TOK // tm, IN_W // tn),
        in_specs=[
            pl.BlockSpec((tm, D_MODEL), lambda i, j: (i, 0)),
            pl.BlockSpec((1, D_MODEL), lambda i, j: (0, 0)),
            pl.BlockSpec((None, 1, D_MODEL), lambda i, j: (_mod_row(i, tm), 0, 0)),
            pl.BlockSpec((None, 1, D_MODEL), lambda i, j: (_mod_row(i, tm), 0, 1)),
            pl.BlockSpec((D_MODEL, tn), lambda i, j: (0, j)),
        ],
        out_specs=pl.BlockSpec((tm, tn), lambda i, j: (i, j)),
        out_shape=jax.ShapeDtypeStruct((N_TOK, IN_W), F32),
        scratch_shapes=[pltpu.VMEM((tm, D_MODEL), BF16)],
        compiler_params=_cparams(("arbitrary", "arbitrary")),
        name="in_proj",
    )(tok, norm_g.reshape(1, D_MODEL), mod3, mod3, w_bf)


def _rope(x, cos, sin):
    lane = lax.broadcasted_iota(jnp.int32, x.shape, 1)
    first = (lane % 64) < 32
    partner = jnp.where(first, pltpu.roll(x, 96, 1), pltpu.roll(x, 32, 1))
    return x * cos + partner * sin


def _softmax_pv(q, k, v):
    s = lax.dot_general(q, k, (((1,), (1,)), ((), ())), preferred_element_type=F32) * ATTN_SCALE
    m = jnp.max(s, axis=-1, keepdims=True)
    p = jnp.exp(s - m)
    l = jnp.sum(p, axis=-1, keepdims=True)
    return jnp.dot(p.astype(BF16), v, preferred_element_type=F32) / l


def _lat_attn_kernel(q_ref, kl_ref, vl_ref, kc_ref, vc_ref, cosq_ref, sinq_ref, cosk_ref, sink_ref,
                     qg_ref, kg_ref, o_ref, k_s, v_s):
    @pl.when(pl.program_id(2) == 0)
    def _():
        k_s[0:CTX_LEN, :] = _rms(kc_ref[...], kg_ref[...]).astype(BF16)
        kl = _rope(_rms(kl_ref[...], kg_ref[...]), cosk_ref[...], sink_ref[...])
        k_s[CTX_LEN:, :] = kl.astype(BF16)
        v_s[0:CTX_LEN, :] = vc_ref[...].astype(BF16)
        v_s[CTX_LEN:, :] = vl_ref[...].astype(BF16)

    for g in range(GQA_GROUP):
        lanes = slice(g * HEAD_DIM, (g + 1) * HEAD_DIM)
        q = _rope(_rms(q_ref[:, lanes], qg_ref[...]), cosq_ref[...], sinq_ref[...])
        o_ref[:, lanes] = _softmax_pv(q.astype(BF16), k_s[...], v_s[...]).astype(o_ref.dtype)


def _latent_attention(proj, cos, sin, q_g, k_g):
    tq = 128
    nq = SEQ // tq
    qw = GQA_GROUP * HEAD_DIM
    kcol = ATTN_W // HEAD_DIM
    vcol = (ATTN_W + KV_W) // HEAD_DIM
    ctx_blk = N_LAT // CTX_LEN
    return pl.pallas_call(
        _lat_attn_kernel,
        grid=(BATCH, N_KV_HEADS, nq),
        in_specs=[
            pl.BlockSpec((tq, qw), lambda b, h, i: (b * nq + i, h)),
            pl.BlockSpec((SEQ, HEAD_DIM), lambda b, h, i: (b, kcol + h)),
            pl.BlockSpec((SEQ, HEAD_DIM), lambda b, h, i: (b, vcol + h)),
            pl.BlockSpec((CTX_LEN, HEAD_DIM), lambda b, h, i: (ctx_blk + b, kcol + h)),
            pl.BlockSpec((CTX_LEN, HEAD_DIM), lambda b, h, i: (ctx_blk + b, vcol + h)),
            pl.BlockSpec((tq, HEAD_DIM), lambda b, h, i: (i, 0)),
            pl.BlockSpec((tq, HEAD_DIM), lambda b, h, i: (i, 0)),
            pl.BlockSpec((SEQ, HEAD_DIM), lambda b, h, i: (0, 0)),
            pl.BlockSpec((SEQ, HEAD_DIM), lambda b, h, i: (0, 0)),
            pl.BlockSpec((1, HEAD_DIM), lambda b, h, i: (0, 0)),
            pl.BlockSpec((1, HEAD_DIM), lambda b, h, i: (0, 0)),
        ],
        out_specs=pl.BlockSpec((tq, qw), lambda b, h, i: (b * nq + i, h)),
        out_shape=jax.ShapeDtypeStruct((N_LAT, ATTN_W), BF16),
        scratch_shapes=[pltpu.VMEM((CTX_LEN + SEQ, HEAD_DIM), BF16),
                        pltpu.VMEM((CTX_LEN + SEQ, HEAD_DIM), BF16)],
        compiler_params=_cparams(("arbitrary", "arbitrary", "arbitrary")),
        name="latent_attention",
    )(proj, proj, proj, proj, proj, cos, sin, cos, sin, q_g.reshape(1, HEAD_DIM), k_g.reshape(1, HEAD_DIM))


def _ctx_attn_kernel(q_ref, k_ref, v_ref, qg_ref, kg_ref, o_ref):
    k = _rms(k_ref[...], kg_ref[...]).astype(BF16)
    v = v_ref[...].astype(BF16)
    for g in range(GQA_GROUP):
        lanes = slice(g * HEAD_DIM, (g + 1) * HEAD_DIM)
        q = _rms(q_ref[:, lanes], qg_ref[...]).astype(BF16)
        o_ref[:, lanes] = _softmax_pv(q, k, v).astype(o_ref.dtype)


def _context_attention(proj, q_g, k_g):
    qw = GQA_GROUP * HEAD_DIM
    kcol = ATTN_W // HEAD_DIM
    vcol = (ATTN_W + KV_W) // HEAD_DIM
    ctx_blk = N_LAT // CTX_LEN
    return pl.pallas_call(
        _ctx_attn_kernel,
        grid=(BATCH, N_KV_HEADS),
        in_specs=[
            pl.BlockSpec((CTX_LEN, qw), lambda b, h: (ctx_blk + b, h)),
            pl.BlockSpec((CTX_LEN, HEAD_DIM), lambda b, h: (ctx_blk + b, kcol + h)),
            pl.BlockSpec((CTX_LEN, HEAD_DIM), lambda b, h: (ctx_blk + b, vcol + h)),
            pl.BlockSpec((1, HEAD_DIM), lambda b, h: (0, 0)),
            pl.BlockSpec((1, HEAD_DIM), lambda b, h: (0, 0)),
        ],
        out_specs=pl.BlockSpec((CTX_LEN, qw), lambda b, h: (b, h)),
        out_shape=jax.ShapeDtypeStruct((N_CTX, ATTN_W), BF16),
        compiler_params=_cparams(("arbitrary", "arbitrary")),
        name="context_attention",
    )(proj, proj, proj, q_g.reshape(1, HEAD_DIM), k_g.reshape(1, HEAD_DIM))


def _mixer_kernel(pin_ref, prev_ref, next_ref, su_ref, sv_ref, pw_ref, ps_ref, sg_ref, sw_ref, sb_ref,
                  o_ref, pad_ref):
    tm = MIX_TM
    i = pl.program_id(0)
    is_lat = i < N_LAT // tm
    pos0 = jnp.where(is_lat, (i % (SEQ // tm)) * tm, 0)
    seq_len = jnp.where(is_lat, SEQ, CTX_LEN)
    pad_ref[0:POOL_HALO, :] = jnp.where(pos0 == 0, 0.0, prev_ref[...])
    pad_ref[POOL_HALO:POOL_HALO + tm, :] = pin_ref[...]
    pad_ref[POOL_HALO + tm:, :] = jnp.where(pos0 + tm == seq_len, 0.0, next_ref[...])
    t = pos0 + lax.broadcasted_iota(jnp.int32, (tm, LANES), 0)
    for gi, w in enumerate(POOL_WINDOWS):
        lanes = slice(gi * LANES, (gi + 1) * LANES)
        acc = pad_ref[POOL_HALO - w // 2:POOL_HALO - w // 2 + tm, lanes]
        for d in range(-w // 2 + 1, w // 2):
            acc = acc + pad_ref[POOL_HALO + d:POOL_HALO + d + tm, lanes]
        cnt = (jnp.minimum(t + w // 2, seq_len) - jnp.maximum(t - w // 2, 0)).astype(F32)
        mixed = acc / cnt - pin_ref[:, lanes]
        y = jnp.dot(mixed.astype(BF16), pw_ref[gi], preferred_element_type=F32) * ps_ref[:, lanes]
        o_ref[:, lanes] = y.astype(o_ref.dtype)

    for h in range(N_SGU_HEADS):
        lanes = slice(h * LANES, (h + 1) * LANES)
        gu = _gelu(su_ref[:, lanes])
        vh = _rms(_gelu(sv_ref[:, lanes]), sg_ref[h:h + 1, :]).astype(BF16)
        for n in range(tm // CHUNK):
            rows = slice(n * CHUNK, (n + 1) * CHUNK)
            mixed = jnp.dot(sw_ref[h], vh[rows], preferred_element_type=F32) + sb_ref[h]
            o_ref[rows, POOL_W + h * LANES:POOL_W + (h + 1) * LANES] = (gu[rows] * mixed).astype(o_ref.dtype)


def _mixers(proj, pool_w_bf, pool_scale, sgu_norm_g, sgu_w_bf, sgu_b_full):
    tm = MIX_TM
    per_tile = tm // POOL_HALO
    last_halo = N_TOK // POOL_HALO - 1
    pcol = (ATTN_W + 2 * KV_W) // POOL_W
    return pl.pallas_call(
        _mixer_kernel,
        grid=(N_TOK // tm,),
        in_specs=[
            pl.BlockSpec((tm, POOL_W), lambda i: (i, pcol)),
            pl.BlockSpec((POOL_HALO, POOL_W), lambda i: (jnp.maximum(i * per_tile - 1, 0), pcol)),
            pl.BlockSpec((POOL_HALO, POOL_W), lambda i: (jnp.minimum((i + 1) * per_tile, last_halo), pcol)),
            pl.BlockSpec((tm, SGU_W), lambda i: (i, pcol + 1)),
            pl.BlockSpec((tm, SGU_W), lambda i: (i, pcol + 2)),
            pl.BlockSpec((len(POOL_WINDOWS), LANES, LANES), lambda i: (0, 0, 0)),
            pl.BlockSpec((1, POOL_W), lambda i: (0, 0)),
            pl.BlockSpec((N_SGU_HEADS, LANES), lambda i: (0, 0)),
            pl.BlockSpec((N_SGU_HEADS, CHUNK, CHUNK), lambda i: (0, 0, 0)),
            pl.BlockSpec((N_SGU_HEADS, CHUNK, LANES), lambda i: (0, 0, 0)),
        ],
        out_specs=pl.BlockSpec((tm, POOL_W + SGU_W), lambda i: (i, 0)),
        out_shape=jax.ShapeDtypeStruct((N_TOK, POOL_W + SGU_W), BF16),
        scratch_shapes=[pltpu.VMEM((tm + 2 * POOL_HALO, POOL_W), F32)],
        compiler_params=_cparams(("arbitrary",)),
        name="mixers",
    )(proj, proj, proj, proj, proj, pool_w_bf, pool_scale.reshape(1, POOL_W), sgu_norm_g, sgu_w_bf, sgu_b_full)


def _outproj_kernel(a_ref, m_ref, w1_ref, w2_ref, x_ref, g_ref, o_ref):
    y = jnp.dot(a_ref[...], w1_ref[...], preferred_element_type=F32)
    y = y + jnp.dot(m_ref[...], w2_ref[...], preferred_element_type=F32)
    o_ref[...] = x_ref[...] + g_ref[...] * y


def _out_proj(attn, mix, w_bf, tok, mod3, n_rows):
    tm, tn = 512, 1024
    nj = D_MODEL // tn
    return pl.pallas_call(
        _outproj_kernel,
        grid=(nj, n_rows // tm),
        in_specs=[
            pl.BlockSpec((tm, ATTN_W), lambda j, i: (i, 0)),
            pl.BlockSpec((tm, POOL_W + SGU_W), lambda j, i: (i, 0)),
            pl.BlockSpec((ATTN_W, tn), lambda j, i: (0, j)),
            pl.BlockSpec((POOL_W + SGU_W, tn), lambda j, i: (1, j)),
            pl.BlockSpec((tm, tn), lambda j, i: (i, j)),
            pl.BlockSpec((None, 1, tn), lambda j, i: (_mod_row(i, tm), 0, 2 * nj + j)),
        ],
        out_specs=pl.BlockSpec((tm, tn), lambda j, i: (i, j)),
        out_shape=jax.ShapeDtypeStruct((n_rows, D_MODEL), F32),
        compiler_params=_cparams(("arbitrary", "arbitrary")),
        name="out_proj",
    )(attn, mix, w_bf, w_bf, tok, mod3)


def _router_kernel(x_ref, g_ref, sh_ref, sc_ref, rw_ref, rb_ref, tri_ref,
                   hp_ref, idx_ref, rank_ref, gate_ref, cnt_ref, run_s):
    @pl.when(pl.program_id(0) == 0)
    def _():
        run_s[...] = jnp.zeros_like(run_s)

    h = _rms(x_ref[...], g_ref[...]) * (1.0 + sc_ref[...]) + sh_ref[...]
    hb = h.astype(BF16).astype(F32)
    hi = lax.bitcast_convert_type(hb[:, :HALF_D], U32)
    lo = lax.bitcast_convert_type(hb[:, HALF_D:], U32)
    packed = hi | (lo >> 16)
    for j in range(PACK_TILES):
        hp_ref[:, j, :] = packed[:, j * LANES:(j + 1) * LANES]

    logits = jnp.dot(h, rw_ref[...], precision=lax.Precision.HIGHEST, preferred_element_type=F32)
    lt = (logits + rb_ref[...]).T[0:N_EXPERTS, :]
    expert = lax.broadcasted_iota(jnp.int32, lt.shape, 0).astype(F32)
    vals, idxs = [], []
    for _ in range(TOP_K):
        m = jnp.max(lt, axis=0, keepdims=True)
        idx = jnp.min(jnp.where(lt == m, expert, float(N_EXPERTS)), axis=0, keepdims=True)
        vals.append(m)
        idxs.append(idx)
        lt = jnp.where(expert == idx, -jnp.inf, lt)

    e = [jnp.exp(v - vals[0]) for v in vals]
    den = e[0] + e[1] + e[2] + e[3]
    slot = lax.broadcasted_iota(jnp.int32, (LANES, lt.shape[1]), 0)
    gates = jnp.zeros((LANES, lt.shape[1]), F32)
    for k in range(TOP_K):
        gates = jnp.where(slot == k, e[k] / den, gates)
    gate_ref[...] = gates.T

    base = run_s[...]
    for k in range(TOP_K):
        onehot = jnp.where(expert == idxs[k], 1.0, 0.0)
        before = jnp.dot(onehot.astype(BF16), tri_ref[...], preferred_element_type=F32)
        rank = jnp.sum(onehot * (before + base[:, 0:1]), axis=0, keepdims=True)
        idx_ref[k:k + 1, :] = idxs[k].astype(jnp.int32)
        rank_ref[k:k + 1, :] = rank.astype(jnp.int32)
        base = base + jnp.sum(onehot, axis=1, keepdims=True)
    run_s[...] = base
    cnt_ref[...] = base.astype(jnp.int32)


def _router(tok, norm_g, mod3, rw_pad, rb_pad, tri, n_rows):
    tm = ROUTER_TM
    return pl.pallas_call(
        _router_kernel,
        grid=(n_rows // tm,),
        in_specs=[
            pl.BlockSpec((tm, D_MODEL), lambda i: (i, 0)),
            pl.BlockSpec((1, D_MODEL), lambda i: (0, 0)),
            pl.BlockSpec((None, 1, D_MODEL), lambda i: (_mod_row(i, tm), 0, 3)),
            pl.BlockSpec((None, 1, D_MODEL), lambda i: (_mod_row(i, tm), 0, 4)),
            pl.BlockSpec((D_MODEL, LANES), lambda i: (0, 0)),
            pl.BlockSpec((1, LANES), lambda i: (0, 0)),
            pl.BlockSpec((tm, tm), lambda i: (0, 0)),
        ],
        out_specs=[
            pl.BlockSpec((tm, PACK_TILES, LANES), lambda i: (i, 0, 0)),
            pl.BlockSpec((TOP_K, tm), lambda i: (0, i)),
            pl.BlockSpec((TOP_K, tm), lambda i: (0, i)),
            pl.BlockSpec((tm, LANES), lambda i: (i, 0)),
            pl.BlockSpec((N_EXPERTS, LANES), lambda i: (0, 0)),
        ],
        out_shape=[
            jax.ShapeDtypeStruct((n_rows, PACK_TILES, LANES), U32),
            jax.ShapeDtypeStruct((TOP_K, n_rows), jnp.int32),
            jax.ShapeDtypeStruct((TOP_K, n_rows), jnp.int32),
            jax.ShapeDtypeStruct((n_rows, LANES), F32),
            jax.ShapeDtypeStruct((N_EXPERTS, LANES), jnp.int32),
        ],
        scratch_shapes=[pltpu.VMEM((N_EXPERTS, LANES), F32)],
        compiler_params=_cparams(("arbitrary",)),
        name="router",
    )(tok, norm_g.reshape(1, D_MODEL), mod3, mod3, rw_pad, rb_pad, tri)


def _dispatch_kernel(dest_ref, pad_ref, hp_ref, xs_hbm, sem, pad_sem, *, rows, n_rows):
    i = pl.program_id(0)

    @pl.when(i == 0)
    def _():
        def per_expert(e, carry):
            lo, hi = pad_ref[e], pad_ref[N_EXPERTS + 1 + e]

            def start(p, c):
                pltpu.make_async_copy(hp_ref.at[0], xs_hbm.at[p], pad_sem).start()
                return c

            def wait(p, c):
                pltpu.make_async_copy(hp_ref.at[0], xs_hbm.at[p], pad_sem).wait()
                return c

            lax.fori_loop(lo, hi, start, 0)
            lax.fori_loop(lo, hi, wait, 0)
            return carry

        lax.fori_loop(0, N_EXPERTS + 1, per_expert, 0)

    base = i * rows

    def issue(r, carry):
        for k in range(TOP_K):
            d = dest_ref[k * n_rows + base + r]
            pltpu.make_async_copy(hp_ref.at[r], xs_hbm.at[d], sem).start()
        return carry

    lax.fori_loop(0, rows, issue, 0)
    for k in range(TOP_K):
        pltpu.make_async_copy(hp_ref, xs_hbm.at[pl.ds(0, rows)], sem).wait()


def _dispatch(dest_flat, pad_meta, hp, n_pad, n_rows):
    rows = 1024
    return pl.pallas_call(
        functools.partial(_dispatch_kernel, rows=rows, n_rows=n_rows),
        grid_spec=pltpu.PrefetchScalarGridSpec(
            num_scalar_prefetch=2,
            grid=(n_rows // rows,),
            in_specs=[pl.BlockSpec((rows, PACK_TILES, LANES), lambda i, d, p: (i, 0, 0))],
            out_specs=pl.BlockSpec(memory_space=pl.ANY),
            scratch_shapes=[pltpu.SemaphoreType.DMA(()), pltpu.SemaphoreType.DMA(())],
        ),
        out_shape=jax.ShapeDtypeStruct((n_pad, PACK_TILES, LANES), U32),
        compiler_params=_cparams(("arbitrary",)),
        name="dispatch",
    )(dest_flat, pad_meta, hp)


def _is_new_expert(meta_ref, i):
    prev = meta_ref[jnp.maximum(i - 1, 0)]
    return jnp.logical_or(i == 0, meta_ref[i] != prev)


def _unpack_rows(x_ref):
    xp = jnp.concatenate([x_ref[:, j, :] for j in range(PACK_TILES)], axis=1)
    hi = lax.bitcast_convert_type(xp & jnp.uint32(0xFFFF0000), F32).astype(BF16)
    lo = lax.bitcast_convert_type(xp << 16, F32).astype(BF16)
    return jnp.concatenate([hi, lo], axis=1)


def _expert_gu_kernel(meta_ref, x_ref, wg_ref, wu_ref, bg_ref, bu_ref, o_ref, wg_s, wu_s, *, n_blocks):
    i = pl.program_id(1)

    @pl.when(_is_new_expert(meta_ref, i))
    def _():
        wg_s[...] = wg_ref[...].astype(BF16)
        wu_s[...] = wu_ref[...].astype(BF16)

    active = i < meta_ref[n_blocks]

    @pl.when(active)
    def _():
        x = _unpack_rows(x_ref)
        g = jnp.dot(x, wg_s[...], preferred_element_type=F32) + bg_ref[...]
        u = jnp.dot(x, wu_s[...], preferred_element_type=F32) + bu_ref[...]
        g = jnp.minimum(g, SWIGLU_LIMIT)
        u = jnp.clip(u, -SWIGLU_LIMIT, SWIGLU_LIMIT)
        o_ref[...] = ((u + 1.0) * (g * _sigmoid(g * SWIGLU_ALPHA))).astype(o_ref.dtype)

    @pl.when(jnp.logical_not(active))
    def _():
        o_ref[...] = jnp.zeros_like(o_ref)


def _active_block(i, m, n_blocks):
    return jnp.minimum(i, m[n_blocks] - 1)


def _expert_gu(meta, xs, w_gu, b_gu, layer, n_blocks):
    tm, tn = EXPERT_TM, 1024
    nj = EXPERT_FF // tn
    b3 = b_gu.reshape(DEPTH, N_EXPERTS, 1, 2 * EXPERT_FF)
    return pl.pallas_call(
        functools.partial(_expert_gu_kernel, n_blocks=n_blocks),
        grid_spec=pltpu.PrefetchScalarGridSpec(
            num_scalar_prefetch=1,
            grid=(nj, n_blocks),
            in_specs=[
                pl.BlockSpec((tm, PACK_TILES, LANES), lambda j, i, m: (_active_block(i, m, n_blocks), 0, 0)),
                pl.BlockSpec((None, None, D_MODEL, tn), lambda j, i, m: (layer, m[i], 0, j)),
                pl.BlockSpec((None, None, D_MODEL, tn), lambda j, i, m: (layer, m[i], 0, nj + j)),
                pl.BlockSpec((None, None, 1, tn), lambda j, i, m: (layer, m[i], 0, j)),
                pl.BlockSpec((None, None, 1, tn), lambda j, i, m: (layer, m[i], 0, nj + j)),
            ],
            out_specs=pl.BlockSpec((tm, tn), lambda j, i, m: (i, j)),
            scratch_shapes=[pltpu.VMEM((D_MODEL, tn), BF16), pltpu.VMEM((D_MODEL, tn), BF16)],
        ),
        out_shape=jax.ShapeDtypeStruct((n_blocks * tm, EXPERT_FF), BF16),
        compiler_params=_cparams(("arbitrary", "arbitrary")),
        name="expert_gu",
    )(meta, xs, w_gu, w_gu, b3, b3)


def _expert_down_kernel(meta_ref, a_ref, w_ref, b_ref, o_ref, w_s, *, n_blocks):
    i = pl.program_id(0)

    @pl.when(_is_new_expert(meta_ref, i))
    def _():
        w_s[...] = w_ref[...].astype(BF16)

    active = i < meta_ref[n_blocks]

    @pl.when(active)
    def _():
        o_ref[...] = jnp.dot(a_ref[...], w_s[...], preferred_element_type=F32) + b_ref[...]

    @pl.when(jnp.logical_not(active))
    def _():
        o_ref[...] = jnp.zeros_like(o_ref)


def _expert_down(meta, act, w_down, b_down, layer, n_blocks):
    tm = EXPERT_TM
    b3 = b_down.reshape(DEPTH, N_EXPERTS, 1, D_MODEL)
    return pl.pallas_call(
        functools.partial(_expert_down_kernel, n_blocks=n_blocks),
        grid_spec=pltpu.PrefetchScalarGridSpec(
            num_scalar_prefetch=1,
            grid=(n_blocks,),
            in_specs=[
                pl.BlockSpec((tm, EXPERT_FF), lambda i, m: (_active_block(i, m, n_blocks), 0)),
                pl.BlockSpec((None, None, EXPERT_FF, D_MODEL), lambda i, m: (layer, m[i], 0, 0)),
                pl.BlockSpec((None, None, 1, D_MODEL), lambda i, m: (layer, m[i], 0, 0)),
            ],
            out_specs=pl.BlockSpec((tm, D_MODEL), lambda i, m: (i, 0)),
            scratch_shapes=[pltpu.VMEM((EXPERT_FF, D_MODEL), BF16)],
        ),
        out_shape=jax.ShapeDtypeStruct((n_blocks * tm, D_MODEL), F32),
        compiler_params=_cparams(("arbitrary",)),
        name="expert_down",
    )(meta, act, w_down, b3)


def _combine_kernel(dest_ref, y_hbm, gt_ref, x_ref, g_ref, o_ref, buf, sems, *, rows, n_rows):
    i = pl.program_id(0)
    n_steps = pl.num_programs(0)

    def issue(step, slot):
        base = step * rows

        def body(r, carry):
            for k in range(TOP_K):
                d = dest_ref[k * n_rows + base + r]
                pltpu.make_async_copy(y_hbm.at[pl.ds(d, 1), :], buf.at[slot, k, pl.ds(r, 1), :],
                                      sems.at[slot]).start()
            return carry

        lax.fori_loop(0, rows, body, 0)

    @pl.when(i == 0)
    def _():
        issue(0, 0)

    @pl.when(i + 1 < n_steps)
    def _():
        issue(i + 1, (i + 1) % 2)

    slot = i % 2
    for k in range(TOP_K):
        pltpu.make_async_copy(y_hbm.at[pl.ds(0, rows), :], buf.at[slot, k], sems.at[slot]).wait()
    gt = gt_ref[...]
    moe = gt[:, 0:1] * buf[slot, 0]
    for k in range(1, TOP_K):
        moe = moe + gt[:, k:k + 1] * buf[slot, k]
    o_ref[...] = x_ref[...] + g_ref[...] * moe


def _combine(dest_flat, ys, gates_t, tok, mod3, n_rows):
    rows = 128
    return pl.pallas_call(
        functools.partial(_combine_kernel, rows=rows, n_rows=n_rows),
        grid_spec=pltpu.PrefetchScalarGridSpec(
            num_scalar_prefetch=1,
            grid=(n_rows // rows,),
            in_specs=[
                pl.BlockSpec(memory_space=pl.ANY),
                pl.BlockSpec((rows, LANES), lambda i, d: (i, 0)),
                pl.BlockSpec((rows, D_MODEL), lambda i, d: (i, 0)),
                pl.BlockSpec((None, 1, D_MODEL), lambda i, d: (_mod_row(i, rows), 0, 5)),
            ],
            out_specs=pl.BlockSpec((rows, D_MODEL), lambda i, d: (i, 0)),
            scratch_shapes=[pltpu.VMEM((2, TOP_K, rows, D_MODEL), F32), pltpu.SemaphoreType.DMA((2,))],
        ),
        out_shape=jax.ShapeDtypeStruct((n_rows, D_MODEL), F32),
        compiler_params=_cparams(("arbitrary",)),
        name="combine",
    )(dest_flat, ys, gates_t, tok, mod3)


def _routing(top_idx, rank, counts, n_rows):
    tm = EXPERT_TM
    n_blocks = -(-(TOP_K * n_rows) // tm) + N_EXPERTS
    counts = counts[:, 0]
    padded = (counts + tm - 1) // tm * tm
    padded_end = jnp.cumsum(padded)
    padded_start = padded_end - padded
    experts = jnp.arange(N_EXPERTS, dtype=jnp.int32)
    start_of = jnp.sum(jnp.where(top_idx[:, :, None] == experts, padded_start, 0), axis=-1)
    dest = (start_of + rank).astype(jnp.int32).reshape(TOP_K * n_rows)
    block_start = jnp.arange(n_blocks, dtype=jnp.int32) * tm
    block_expert = jnp.minimum(jnp.sum(padded_end[None, :] <= block_start[:, None], axis=1), N_EXPERTS - 1)
    meta = jnp.concatenate([block_expert, padded_end[-1:] // tm]).astype(jnp.int32)
    n_pad = jnp.full((1,), n_blocks * tm, jnp.int32)
    pad_meta = jnp.concatenate([padded_start + counts, padded_end[-1:], padded_end, n_pad]).astype(jnp.int32)
    return dest, meta, pad_meta, n_blocks


def _rope_tables():
    rows = SEQ // GRID_W
    row = jnp.broadcast_to(jnp.arange(rows, dtype=F32)[:, None], (rows, GRID_W)).reshape(-1)
    col = jnp.broadcast_to(jnp.arange(GRID_W, dtype=F32)[None, :], (rows, GRID_W)).reshape(-1)
    inv_freq = ROPE_THETA ** (-jnp.arange(ROPE_FREQS, dtype=F32) / ROPE_FREQS)
    ang_r = row[:, None] * inv_freq
    ang_c = col[:, None] * inv_freq
    cos = jnp.concatenate([jnp.cos(ang_r), jnp.cos(ang_r), jnp.cos(ang_c), jnp.cos(ang_c)], axis=-1)
    sin = jnp.concatenate([-jnp.sin(ang_r), jnp.sin(ang_r), -jnp.sin(ang_c), jnp.sin(ang_c)], axis=-1)
    return cos, sin


def kernel(x, c, ctx, c_ctx, ada_w, ada_b, norm1_g, norm2_g, w_in, q_norm_g, k_norm_g, pool_w, pool_scale,
           sgu_norm_g, sgu_w, sgu_b, w_out, router_w, router_b, w_gu, b_gu, w_down, b_down):
    cos, sin = _rope_tables()
    tok = jnp.concatenate([x.reshape(N_LAT, D_MODEL), ctx.reshape(N_CTX, D_MODEL)], axis=0)
    cc = jnp.zeros((MOD_ROWS, D_MODEL), F32).at[:BATCH].set(c).at[BATCH].set(c_ctx)
    mod = _adaln(cc, ada_w, ada_b)
    tri = jnp.triu(jnp.ones((ROUTER_TM, ROUTER_TM), BF16), k=1)
    for l in range(DEPTH):
        last = l == DEPTH - 1
        n_rows = N_LAT if last else N_TOK
        mod3 = mod[l].reshape(MOD_ROWS, 1, 6 * D_MODEL)
        proj = _in_proj(tok, norm1_g[l], mod3, w_in[l].astype(BF16))
        attn = _latent_attention(proj, cos, sin, q_norm_g[l], k_norm_g[l])
        if not last:
            attn = jnp.concatenate([attn, _context_attention(proj, q_norm_g[l], k_norm_g[l])], axis=0)
        sgu_b_full = jnp.broadcast_to(sgu_b[l][:, :, None], (N_SGU_HEADS, CHUNK, LANES))
        mix = _mixers(proj, pool_w[l].astype(BF16), pool_scale[l], sgu_norm_g[l], sgu_w[l].astype(BF16), sgu_b_full)
        tok = _out_proj(attn, mix, w_out[l].astype(BF16), tok, mod3, n_rows)
        rw_pad = jnp.zeros((D_MODEL, LANES), F32).at[:, :N_EXPERTS].set(router_w[l])
        rb_pad = jnp.zeros((1, LANES), F32).at[0, :N_EXPERTS].set(router_b[l])
        hp, top_idx, rank, gates_t, counts = _router(tok, norm2_g[l], mod3, rw_pad, rb_pad, tri, n_rows)
        dest, meta, pad_meta, n_blocks = _routing(top_idx, rank, counts, n_rows)
        xs = _dispatch(dest, pad_meta, hp, n_blocks * EXPERT_TM, n_rows)
        act = _expert_gu(meta, xs, w_gu, b_gu, l, n_blocks)
        ys = _expert_down(meta, act, w_down, b_down, l, n_blocks)
        tok = _combine(dest, ys, gates_t, tok, mod3, n_rows)
    return tok.reshape(BATCH, SEQ, D_MODEL)
```

```python
import functools

import jax
import jax.numpy as jnp
from jax import lax
from jax.experimental import pallas as pl
from jax.experimental.pallas import tpu as pltpu

F32 = jnp.float32
BF16 = jnp.bfloat16
U32 = jnp.uint32

D_MODEL = 2048
BATCH = 4
SEQ = 4096
DEPTH = 2
GRID_W = 64
CTX_LEN = 256
HEAD_DIM = 128
ATTN_W = 1024
N_Q_HEADS = 8
GQA_GROUP = 4
N_KV_HEADS = 2
KV_W = 256
ROPE_THETA = 10000.0
ROPE_FREQS = 32
ATTN_SCALE = HEAD_DIM ** -0.5
LOG2_E = 1.4426950408889634
ATTN_TK = 512
POOL_WINDOWS = (2, 4, 8, 16)
POOL_W = 512
SGU_W = 512
N_SGU_HEADS = 4
CHUNK = 128
IN_W = 3072
N_EXPERTS = 32
TOP_K = 4
EXPERT_FF = 2048
SWIGLU_LIMIT = 7.0
SWIGLU_ALPHA = 1.702
EPS = 1e-6

N_LAT = BATCH * SEQ
N_CTX = BATCH * CTX_LEN
N_TOK = N_LAT + N_CTX
MOD_ROWS = 8
LANES = 128
POOL_HALO = 8
MIX_TM = 256
EXPERT_TM = 256
ROUTER_TM = 512
HALF_D = D_MODEL // 2
PACK_TILES = HALF_D // LANES
VMEM_LIMIT = 56 * 1024 * 1024


def _cparams(sem, vmem=VMEM_LIMIT):
    return pltpu.CompilerParams(dimension_semantics=sem, vmem_limit_bytes=vmem)


def _mod_row(row_tile, tm):
    return jnp.minimum(row_tile * tm // SEQ, BATCH)


def _rms(x, g):
    return x * lax.rsqrt(jnp.mean(x * x, axis=-1, keepdims=True) + EPS) * g


def _sigmoid(x):
    return 1.0 / (1.0 + jnp.exp(-x))


def _gelu(x):
    return 0.5 * x * (1.0 + lax.erf(x * 0.7071067811865476))


def _adaln_kernel(cc_ref, w_ref, b_ref, o_ref):
    cc = cc_ref[...]
    s = (cc * _sigmoid(cc)).astype(BF16)
    o_ref[...] = jnp.dot(s, w_ref[...].astype(BF16), preferred_element_type=F32) + b_ref[...]


def _adaln(cc, ada_w, ada_b):
    tn = 1024
    n = 6 * D_MODEL
    return pl.pallas_call(
        _adaln_kernel,
        grid=(DEPTH, n // tn),
        in_specs=[
            pl.BlockSpec((MOD_ROWS, D_MODEL), lambda l, j: (0, 0)),
            pl.BlockSpec((None, D_MODEL, tn), lambda l, j: (l, 0, j)),
            pl.BlockSpec((None, 1, tn), lambda l, j: (l, 0, j)),
        ],
        out_specs=pl.BlockSpec((None, MOD_ROWS, tn), lambda l, j: (l, 0, j)),
        out_shape=jax.ShapeDtypeStruct((DEPTH, MOD_ROWS, n), F32),
        compiler_params=_cparams(("arbitrary", "arbitrary")),
        name="adaln",
    )(cc, ada_w, ada_b.reshape(DEPTH, 1, n))


def _proj_kernel(x_ref, g_ref, sh_ref, sc_ref, w_ref, o_ref, h_ref):
    @pl.when(pl.program_id(1) == 0)
    def _():
        y = _rms(x_ref[...], g_ref[...])
        h_ref[...] = (y * (1.0 + sc_ref[...]) + sh_ref[...]).astype(BF16)

    o_ref[...] = jnp.dot(h_ref[...], w_ref[...], preferred_element_type=F32)


def _in_proj(tok, norm_g, mod3, w_bf):
    tm, tn = 512, 1024
    return pl.pallas_call(
        _proj_kernel,
        grid=(N_TOK // tm, IN_W // tn),
        in_specs=[
            pl.BlockSpec((tm, D_MODEL), lambda i, j: (i, 0)),
            pl.BlockSpec((1, D_MODEL), lambda i, j: (0, 0)),
            pl.BlockSpec((None, 1, D_MODEL), lambda i, j: (_mod_row(i, tm), 0, 0)),
            pl.BlockSpec((None, 1, D_MODEL), lambda i, j: (_mod_row(i, tm), 0, 1)),
            pl.BlockSpec((D_MODEL, tn), lambda i, j: (0, j)),
        ],
        out_specs=pl.BlockSpec((tm, tn), lambda i, j: (i, j)),
        out_shape=jax.ShapeDtypeStruct((N_TOK, IN_W), F32),
        scratch_shapes=[pltpu.VMEM((tm, D_MODEL), BF16)],
        compiler_params=_cparams(("arbitrary", "arbitrary")),
        name="in_proj",
    )(tok, norm_g.reshape(1, D_MODEL), mod3, mod3, w_bf)


def _rope(x, cos, sin):
    lane = lax.broadcasted_iota(jnp.int32, x.shape, 1)
    first = (lane % 64) < 32
    partner = jnp.where(first, pltpu.roll(x, 96, 1), pltpu.roll(x, 32, 1))
    return x * cos + partner * sin


def _softmax_pv(q, k, v):
    s = lax.dot_general(q, k, (((1,), (1,)), ((), ())), preferred_element_type=F32) * ATTN_SCALE
    m = jnp.max(s, axis=-1, keepdims=True)
    p = jnp.exp(s - m)
    l = jnp.sum(p, axis=-1, keepdims=True)
    return jnp.dot(p.astype(BF16), v, preferred_element_type=F32) / l


def _lat_attn_kernel(q_ref, kl_ref, vl_ref, kc_ref, vc_ref, cosq_ref, sinq_ref, cosk_ref, sink_ref,
                     qg_ref, kg_ref, o_ref, k_s, v_s, s_s):
    @pl.when(pl.program_id(2) == 0)
    def _():
        k_s[0:CTX_LEN, :] = _rms(kc_ref[...], kg_ref[...]).astype(BF16)
        kl = _rope(_rms(kl_ref[...], kg_ref[...]), cosk_ref[...], sink_ref[...])
        k_s[CTX_LEN:, :] = kl.astype(BF16)
        v_s[0:CTX_LEN, :] = vc_ref[...].astype(BF16)
        v_s[CTX_LEN:, :] = vl_ref[...].astype(BF16)

    def scores(g):
        lanes = slice(g * HEAD_DIM, (g + 1) * HEAD_DIM)
        q = _rope(_rms(q_ref[:, lanes], qg_ref[...]), cosq_ref[...], sinq_ref[...])
        q = (q * (ATTN_SCALE * LOG2_E)).astype(BF16)
        s_s[g % 2] = lax.dot_general(q, k_s[...], (((1,), (1,)), ((), ())), preferred_element_type=F32)

    scores(0)
    for g in range(GQA_GROUP):
        if g + 1 < GQA_GROUP:
            scores(g + 1)
        s = s_s[g % 2]
        p = jnp.exp2(s - jnp.max(s, axis=-1, keepdims=True))
        l = jnp.sum(p, axis=-1, keepdims=True)
        o = jnp.dot(p.astype(BF16), v_s[...], preferred_element_type=F32) / l
        o_ref[:, g * HEAD_DIM:(g + 1) * HEAD_DIM] = o.astype(o_ref.dtype)


def _latent_attention(proj, cos, sin, q_g, k_g):
    tq = 256
    nq = SEQ // tq
    qw = GQA_GROUP * HEAD_DIM
    kcol = ATTN_W // HEAD_DIM
    vcol = (ATTN_W + KV_W) // HEAD_DIM
    ctx_blk = N_LAT // CTX_LEN
    return pl.pallas_call(
        _lat_attn_kernel,
        grid=(BATCH, N_KV_HEADS, nq),
        in_specs=[
            pl.BlockSpec((tq, qw), lambda b, h, i: (b * nq + i, h)),
            pl.BlockSpec((SEQ, HEAD_DIM), lambda b, h, i: (b, kcol + h)),
            pl.BlockSpec((SEQ, HEAD_DIM), lambda b, h, i: (b, vcol + h)),
            pl.BlockSpec((CTX_LEN, HEAD_DIM), lambda b, h, i: (ctx_blk + b, kcol + h)),
            pl.BlockSpec((CTX_LEN, HEAD_DIM), lambda b, h, i: (ctx_blk + b, vcol + h)),
            pl.BlockSpec((tq, HEAD_DIM), lambda b, h, i: (i, 0)),
            pl.BlockSpec((tq, HEAD_DIM), lambda b, h, i: (i, 0)),
            pl.BlockSpec((SEQ, HEAD_DIM), lambda b, h, i: (0, 0)),
            pl.BlockSpec((SEQ, HEAD_DIM), lambda b, h, i: (0, 0)),
            pl.BlockSpec((1, HEAD_DIM), lambda b, h, i: (0, 0)),
            pl.BlockSpec((1, HEAD_DIM), lambda b, h, i: (0, 0)),
        ],
        out_specs=pl.BlockSpec((tq, qw), lambda b, h, i: (b * nq + i, h)),
        out_shape=jax.ShapeDtypeStruct((N_LAT, ATTN_W), BF16),
        scratch_shapes=[pltpu.VMEM((CTX_LEN + SEQ, HEAD_DIM), BF16),
                        pltpu.VMEM((CTX_LEN + SEQ, HEAD_DIM), BF16),
                        pltpu.VMEM((2, tq, CTX_LEN + SEQ), F32)],
        compiler_params=_cparams(("arbitrary", "arbitrary", "arbitrary")),
        name="latent_attention",
    )(proj, proj, proj, proj, proj, cos, sin, cos, sin, q_g.reshape(1, HEAD_DIM), k_g.reshape(1, HEAD_DIM))


def _ctx_attn_kernel(q_ref, k_ref, v_ref, qg_ref, kg_ref, o_ref):
    k = _rms(k_ref[...], kg_ref[...]).astype(BF16)
    v = v_ref[...].astype(BF16)
    for g in range(GQA_GROUP):
        lanes = slice(g * HEAD_DIM, (g + 1) * HEAD_DIM)
        q = _rms(q_ref[:, lanes], qg_ref[...]).astype(BF16)
        o_ref[:, lanes] = _softmax_pv(q, k, v).astype(o_ref.dtype)


def _context_attention(proj, q_g, k_g):
    qw = GQA_GROUP * HEAD_DIM
    kcol = ATTN_W // HEAD_DIM
    vcol = (ATTN_W + KV_W) // HEAD_DIM
    ctx_blk = N_LAT // CTX_LEN
    return pl.pallas_call(
        _ctx_attn_kernel,
        grid=(BATCH, N_KV_HEADS),
        in_specs=[
            pl.BlockSpec((CTX_LEN, qw), lambda b, h: (ctx_blk + b, h)),
            pl.BlockSpec((CTX_LEN, HEAD_DIM), lambda b, h: (ctx_blk + b, kcol + h)),
            pl.BlockSpec((CTX_LEN, HEAD_DIM), lambda b, h: (ctx_blk + b, vcol + h)),
            pl.BlockSpec((1, HEAD_DIM), lambda b, h: (0, 0)),
            pl.BlockSpec((1, HEAD_DIM), lambda b, h: (0, 0)),
        ],
        out_specs=pl.BlockSpec((CTX_LEN, qw), lambda b, h: (b, h)),
        out_shape=jax.ShapeDtypeStruct((N_CTX, ATTN_W), BF16),
        compiler_params=_cparams(("arbitrary", "arbitrary")),
        name="context_attention",
    )(proj, proj, proj, q_g.reshape(1, HEAD_DIM), k_g.reshape(1, HEAD_DIM))


def _mixer_kernel(pin_ref, prev_ref, next_ref, su_ref, sv_ref, pw_ref, ps_ref, sg_ref, sw_ref, sb_ref,
                  o_ref, pad_ref):
    tm = MIX_TM
    i = pl.program_id(0)
    is_lat = i < N_LAT // tm
    pos0 = jnp.where(is_lat, (i % (SEQ // tm)) * tm, 0)
    seq_len = jnp.where(is_lat, SEQ, CTX_LEN)
    pad_ref[0:POOL_HALO, :] = jnp.where(pos0 == 0, 0.0, prev_ref[...])
    pad_ref[POOL_HALO:POOL_HALO + tm, :] = pin_ref[...]
    pad_ref[POOL_HALO + tm:, :] = jnp.where(pos0 + tm == seq_len, 0.0, next_ref[...])
    t = pos0 + lax.broadcasted_iota(jnp.int32, (tm, LANES), 0)
    for gi, w in enumerate(POOL_WINDOWS):
        lanes = slice(gi * LANES, (gi + 1) * LANES)
        acc = pad_ref[POOL_HALO - w // 2:POOL_HALO - w // 2 + tm, lanes]
        for d in range(-w // 2 + 1, w // 2):
            acc = acc + pad_ref[POOL_HALO + d:POOL_HALO + d + tm, lanes]
        cnt = (jnp.minimum(t + w // 2, seq_len) - jnp.maximum(t - w // 2, 0)).astype(F32)
        mixed = acc / cnt - pin_ref[:, lanes]
        y = jnp.dot(mixed.astype(BF16), pw_ref[gi], preferred_element_type=F32) * ps_ref[:, lanes]
        o_ref[:, lanes] = y.astype(o_ref.dtype)

    for h in range(N_SGU_HEADS):
        lanes = slice(h * LANES, (h + 1) * LANES)
        gu = _gelu(su_ref[:, lanes])
        vh = _rms(_gelu(sv_ref[:, lanes]), sg_ref[h:h + 1, :]).astype(BF16)
        for n in range(tm // CHUNK):
            rows = slice(n * CHUNK, (n + 1) * CHUNK)
            mixed = jnp.dot(sw_ref[h], vh[rows], preferred_element_type=F32) + sb_ref[h]
            o_ref[rows, POOL_W + h * LANES:POOL_W + (h + 1) * LANES] = (gu[rows] * mixed).astype(o_ref.dtype)


def _mixers(proj, pool_w_bf, pool_scale, sgu_norm_g, sgu_w_bf, sgu_b_full):
    tm = MIX_TM
    per_tile = tm // POOL_HALO
    last_halo = N_TOK // POOL_HALO - 1
    pcol = (ATTN_W + 2 * KV_W) // POOL_W
    return pl.pallas_call(
        _mixer_kernel,
        grid=(N_TOK // tm,),
        in_specs=[
            pl.BlockSpec((tm, POOL_W), lambda i: (i, pcol)),
            pl.BlockSpec((POOL_HALO, POOL_W), lambda i: (jnp.maximum(i * per_tile - 1, 0), pcol)),
            pl.BlockSpec((POOL_HALO, POOL_W), lambda i: (jnp.minimum((i + 1) * per_tile, last_halo), pcol)),
            pl.BlockSpec((tm, SGU_W), lambda i: (i, pcol + 1)),
            pl.BlockSpec((tm, SGU_W), lambda i: (i, pcol + 2)),
            pl.BlockSpec((len(POOL_WINDOWS), LANES, LANES), lambda i: (0, 0, 0)),
            pl.BlockSpec((1, POOL_W), lambda i: (0, 0)),
            pl.BlockSpec((N_SGU_HEADS, LANES), lambda i: (0, 0)),
            pl.BlockSpec((N_SGU_HEADS, CHUNK, CHUNK), lambda i: (0, 0, 0)),
            pl.BlockSpec((N_SGU_HEADS, CHUNK, LANES), lambda i: (0, 0, 0)),
        ],
        out_specs=pl.BlockSpec((tm, POOL_W + SGU_W), lambda i: (i, 0)),
        out_shape=jax.ShapeDtypeStruct((N_TOK, POOL_W + SGU_W), BF16),
        scratch_shapes=[pltpu.VMEM((tm + 2 * POOL_HALO, POOL_W), F32)],
        compiler_params=_cparams(("arbitrary",)),
        name="mixers",
    )(proj, proj, proj, proj, proj, pool_w_bf, pool_scale.reshape(1, POOL_W), sgu_norm_g, sgu_w_bf, sgu_b_full)


def _outproj_kernel(a_ref, m_ref, w1_ref, w2_ref, x_ref, g_ref, o_ref):
    y = jnp.dot(a_ref[...], w1_ref[...], preferred_element_type=F32)
    y = y + jnp.dot(m_ref[...], w2_ref[...], preferred_element_type=F32)
    o_ref[...] = x_ref[...] + g_ref[...] * y


def _out_proj(attn, mix, w_bf, tok, mod3, n_rows):
    tm, tn = 512, 1024
    nj = D_MODEL // tn
    return pl.pallas_call(
        _outproj_kernel,
        grid=(nj, n_rows // tm),
        in_specs=[
            pl.BlockSpec((tm, ATTN_W), lambda j, i: (i, 0)),
            pl.BlockSpec((tm, POOL_W + SGU_W), lambda j, i: (i, 0)),
            pl.BlockSpec((ATTN_W, tn), lambda j, i: (0, j)),
            pl.BlockSpec((POOL_W + SGU_W, tn), lambda j, i: (1, j)),
            pl.BlockSpec((tm, tn), lambda j, i: (i, j)),
            pl.BlockSpec((None, 1, tn), lambda j, i: (_mod_row(i, tm), 0, 2 * nj + j)),
        ],
        out_specs=pl.BlockSpec((tm, tn), lambda j, i: (i, j)),
        out_shape=jax.ShapeDtypeStruct((n_rows, D_MODEL), F32),
        compiler_params=_cparams(("arbitrary", "arbitrary")),
        name="out_proj",
    )(attn, mix, w_bf, w_bf, tok, mod3)


def _router_kernel(x_ref, g_ref, sh_ref, sc_ref, rw_ref, rb_ref, tri_ref,
                   hp_ref, idx_ref, rank_ref, gate_ref, cnt_ref, run_s):
    @pl.when(pl.program_id(0) == 0)
    def _():
        run_s[...] = jnp.zeros_like(run_s)

    h = _rms(x_ref[...], g_ref[...]) * (1.0 + sc_ref[...]) + sh_ref[...]
    hb = h.astype(BF16).astype(F32)
    hi = lax.bitcast_convert_type(hb[:, :HALF_D], U32)
    lo = lax.bitcast_convert_type(hb[:, HALF_D:], U32)
    packed = hi | (lo >> 16)
    for j in range(PACK_TILES):
        hp_ref[pl.ds(j, packed.shape[0], stride=PACK_TILES), :] = packed[:, j * LANES:(j + 1) * LANES]

    logits = jnp.dot(h, rw_ref[...], precision=lax.Precision.HIGHEST, preferred_element_type=F32)
    lt = (logits + rb_ref[...]).T[0:N_EXPERTS, :]
    expert = lax.broadcasted_iota(jnp.int32, lt.shape, 0).astype(F32)
    vals, idxs = [], []
    for _ in range(TOP_K):
        m = jnp.max(lt, axis=0, keepdims=True)
        idx = jnp.min(jnp.where(lt == m, expert, float(N_EXPERTS)), axis=0, keepdims=True)
        vals.append(m)
        idxs.append(idx)
        lt = jnp.where(expert == idx, -jnp.inf, lt)

    e = [jnp.exp(v - vals[0]) for v in vals]
    den = e[0] + e[1] + e[2] + e[3]
    slot = lax.broadcasted_iota(jnp.int32, (LANES, lt.shape[1]), 0)
    gates = jnp.zeros((LANES, lt.shape[1]), F32)
    for k in range(TOP_K):
        gates = jnp.where(slot == k, e[k] / den, gates)
    gate_ref[...] = gates.T

    base = run_s[...]
    for k in range(TOP_K):
        onehot = jnp.where(expert == idxs[k], 1.0, 0.0)
        before = jnp.dot(onehot.astype(BF16), tri_ref[...], preferred_element_type=F32)
        rank = jnp.sum(onehot * (before + base[:, 0:1]), axis=0, keepdims=True)
        idx_ref[k:k + 1, :] = idxs[k].astype(jnp.int32)
        rank_ref[k:k + 1, :] = rank.astype(jnp.int32)
        base = base + jnp.sum(onehot, axis=1, keepdims=True)
    run_s[...] = base
    cnt_ref[...] = base.astype(jnp.int32)


def _router(tok, norm_g, mod3, rw_pad, rb_pad, tri, n_rows):
    tm = ROUTER_TM
    return pl.pallas_call(
        _router_kernel,
        grid=(n_rows // tm,),
        in_specs=[
            pl.BlockSpec((tm, D_MODEL), lambda i: (i, 0)),
            pl.BlockSpec((1, D_MODEL), lambda i: (0, 0)),
            pl.BlockSpec((None, 1, D_MODEL), lambda i: (_mod_row(i, tm), 0, 3)),
            pl.BlockSpec((None, 1, D_MODEL), lambda i: (_mod_row(i, tm), 0, 4)),
            pl.BlockSpec((D_MODEL, LANES), lambda i: (0, 0)),
            pl.BlockSpec((1, LANES), lambda i: (0, 0)),
            pl.BlockSpec((tm, tm), lambda i: (0, 0)),
        ],
        out_specs=[
            pl.BlockSpec((tm * PACK_TILES, LANES), lambda i: (i, 0)),
            pl.BlockSpec((TOP_K, tm), lambda i: (0, i)),
            pl.BlockSpec((TOP_K, tm), lambda i: (0, i)),
            pl.BlockSpec((tm, LANES), lambda i: (i, 0)),
            pl.BlockSpec((N_EXPERTS, LANES), lambda i: (0, 0)),
        ],
        out_shape=[
            jax.ShapeDtypeStruct((n_rows * PACK_TILES, LANES), U32),
            jax.ShapeDtypeStruct((TOP_K, n_rows), jnp.int32),
            jax.ShapeDtypeStruct((TOP_K, n_rows), jnp.int32),
            jax.ShapeDtypeStruct((n_rows, LANES), F32),
            jax.ShapeDtypeStruct((N_EXPERTS, LANES), jnp.int32),
        ],
        scratch_shapes=[pltpu.VMEM((N_EXPERTS, LANES), F32)],
        compiler_params=_cparams(("arbitrary",)),
        name="router",
    )(tok, norm_g.reshape(1, D_MODEL), mod3, mod3, rw_pad, rb_pad, tri)


def _dispatch_kernel(dest_ref, pad_ref, hp_ref, xs_hbm, sem, pad_sem, *, rows, n_rows):
    i = pl.program_id(0)

    @pl.when(i == 0)
    def _():
        def per_expert(e, carry):
            lo, hi = pad_ref[e], pad_ref[N_EXPERTS + 1 + e]

            def start(p, c):
                pltpu.make_async_copy(hp_ref.at[0], xs_hbm.at[p], pad_sem).start()
                return c

            def wait(p, c):
                pltpu.make_async_copy(hp_ref.at[0], xs_hbm.at[p], pad_sem).wait()
                return c

            lax.fori_loop(lo, hi, start, 0)
            lax.fori_loop(lo, hi, wait, 0)
            return carry

        lax.fori_loop(0, N_EXPERTS + 1, per_expert, 0)

    base = i * rows

    def issue(r, carry):
        for k in range(TOP_K):
            d = dest_ref[k * n_rows + base + r]
            pltpu.make_async_copy(hp_ref.at[r], xs_hbm.at[d], sem).start()
        return carry

    lax.fori_loop(0, rows, issue, 0)
    for k in range(TOP_K):
        pltpu.make_async_copy(hp_ref, xs_hbm.at[pl.ds(0, rows)], sem).wait()


def _dispatch(dest_flat, pad_meta, hp, n_pad, n_rows):
    rows = 1024
    return pl.pallas_call(
        functools.partial(_dispatch_kernel, rows=rows, n_rows=n_rows),
        grid_spec=pltpu.PrefetchScalarGridSpec(
            num_scalar_prefetch=2,
            grid=(n_rows // rows,),
            in_specs=[pl.BlockSpec((rows, PACK_TILES, LANES), lambda i, d, p: (i, 0, 0))],
            out_specs=pl.BlockSpec(memory_space=pl.ANY),
            scratch_shapes=[pltpu.SemaphoreType.DMA(()), pltpu.SemaphoreType.DMA(())],
        ),
        out_shape=jax.ShapeDtypeStruct((n_pad, PACK_TILES, LANES), U32),
        compiler_params=_cparams(("arbitrary",)),
        name="dispatch",
    )(dest_flat, pad_meta, hp)


def _is_new_expert(meta_ref, i):
    prev = meta_ref[jnp.maximum(i - 1, 0)]
    return jnp.logical_or(i == 0, meta_ref[i] != prev)


def _cast_rows(src_ref, dst_ref, chunk=256):
    def body(c, carry):
        rows = pl.ds(pl.multiple_of(c * chunk, chunk), chunk)
        dst_ref[rows, :] = src_ref[rows, :].astype(dst_ref.dtype)
        return carry

    lax.fori_loop(0, src_ref.shape[0] // chunk, body, 0)


def _unpack_rows(x_ref):
    n = x_ref.shape[0] // PACK_TILES
    xp = jnp.concatenate([x_ref[pl.ds(j, n, stride=PACK_TILES), :] for j in range(PACK_TILES)], axis=1)
    hi = lax.bitcast_convert_type(xp & jnp.uint32(0xFFFF0000), F32).astype(BF16)
    lo = lax.bitcast_convert_type(xp << 16, F32).astype(BF16)
    return jnp.concatenate([hi, lo], axis=1)


def _expert_gu_kernel(meta_ref, x_ref, w_hbm, bg_ref, bu_ref, o_ref, wg_f, wu_f, wg_s, wu_s, sems,
                      *, n_blocks, layer):
    j, i = pl.program_id(0), pl.program_id(1)
    tn = wg_f.shape[1]
    nj = pl.num_programs(0)

    def fetch(e, jj):
        col = pl.multiple_of(jj * tn, tn)
        return (pltpu.make_async_copy(w_hbm.at[layer, e, :, pl.ds(col, tn)], wg_f, sems.at[0]),
                pltpu.make_async_copy(w_hbm.at[layer, e, :, pl.ds(EXPERT_FF + col, tn)], wu_f, sems.at[1]))

    @pl.when(jnp.logical_and(j == 0, i == 0))
    def _():
        for cp in fetch(meta_ref[0], 0):
            cp.start()

    active = i < meta_ref[n_blocks]

    @pl.when(jnp.logical_and(active, _is_new_expert(meta_ref, i)))
    def _():
        for cp in fetch(meta_ref[i], j):
            cp.wait()
        _cast_rows(wg_f, wg_s)
        _cast_rows(wu_f, wu_s)
        nxt = meta_ref[n_blocks + 1 + i]

        @pl.when(nxt >= 0)
        def _():
            for cp in fetch(nxt, j):
                cp.start()

        @pl.when(jnp.logical_and(nxt < 0, j + 1 < nj))
        def _():
            for cp in fetch(meta_ref[0], j + 1):
                cp.start()

    @pl.when(active)
    def _():
        x = _unpack_rows(x_ref)
        g = jnp.dot(x, wg_s[...], preferred_element_type=F32) + bg_ref[...]
        u = jnp.dot(x, wu_s[...], preferred_element_type=F32) + bu_ref[...]
        g = jnp.minimum(g, SWIGLU_LIMIT)
        u = jnp.clip(u, -SWIGLU_LIMIT, SWIGLU_LIMIT)
        o_ref[...] = ((u + 1.0) * (g * _sigmoid(g * SWIGLU_ALPHA))).astype(o_ref.dtype)

    @pl.when(jnp.logical_not(active))
    def _():
        o_ref[...] = jnp.zeros_like(o_ref)


def _active_block(i, m, n_blocks):
    return jnp.minimum(i, m[n_blocks] - 1)


def _expert_gu(meta, xs, w_gu, b_gu, layer, n_blocks):
    tm, tn = EXPERT_TM, 1024
    nj = EXPERT_FF // tn
    b3 = b_gu.reshape(DEPTH, N_EXPERTS, 1, 2 * EXPERT_FF)
    return pl.pallas_call(
        functools.partial(_expert_gu_kernel, n_blocks=n_blocks, layer=layer),
        grid_spec=pltpu.PrefetchScalarGridSpec(
            num_scalar_prefetch=1,
            grid=(nj, n_blocks),
            in_specs=[
                pl.BlockSpec((tm * PACK_TILES, LANES), lambda j, i, m: (_active_block(i, m, n_blocks), 0)),
                pl.BlockSpec(memory_space=pl.ANY),
                pl.BlockSpec((None, None, 1, tn), lambda j, i, m: (layer, m[i], 0, j)),
                pl.BlockSpec((None, None, 1, tn), lambda j, i, m: (layer, m[i], 0, nj + j)),
            ],
            out_specs=pl.BlockSpec((tm, tn), lambda j, i, m: (i, j)),
            scratch_shapes=[pltpu.VMEM((D_MODEL, tn), F32), pltpu.VMEM((D_MODEL, tn), F32),
                            pltpu.VMEM((D_MODEL, tn), BF16), pltpu.VMEM((D_MODEL, tn), BF16),
                            pltpu.SemaphoreType.DMA((2,))],
        ),
        out_shape=jax.ShapeDtypeStruct((n_blocks * tm, EXPERT_FF), BF16),
        compiler_params=_cparams(("arbitrary", "arbitrary")),
        name="expert_gu",
    )(meta, xs, w_gu, b3, b3)


def _expert_down_kernel(meta_ref, a_ref, w_hbm, b_ref, o_ref, w_f, w_s, sem, *, n_blocks, layer):
    i = pl.program_id(0)

    def fetch(e):
        return pltpu.make_async_copy(w_hbm.at[layer, e], w_f, sem)

    @pl.when(i == 0)
    def _():
        fetch(meta_ref[0]).start()

    active = i < meta_ref[n_blocks]

    @pl.when(jnp.logical_and(active, _is_new_expert(meta_ref, i)))
    def _():
        fetch(meta_ref[i]).wait()
        _cast_rows(w_f, w_s)
        nxt = meta_ref[n_blocks + 1 + i]

        @pl.when(nxt >= 0)
        def _():
            fetch(nxt).start()

    @pl.when(active)
    def _():
        o_ref[...] = jnp.dot(a_ref[...], w_s[...], preferred_element_type=F32) + b_ref[...]

    @pl.when(jnp.logical_not(active))
    def _():
        o_ref[...] = jnp.zeros_like(o_ref)


def _expert_down(meta, act, w_down, b_down, layer, n_blocks):
    tm = EXPERT_TM
    b3 = b_down.reshape(DEPTH, N_EXPERTS, 1, D_MODEL)
    return pl.pallas_call(
        functools.partial(_expert_down_kernel, n_blocks=n_blocks, layer=layer),
        grid_spec=pltpu.PrefetchScalarGridSpec(
            num_scalar_prefetch=1,
            grid=(n_blocks,),
            in_specs=[
                pl.BlockSpec((tm, EXPERT_FF), lambda i, m: (_active_block(i, m, n_blocks), 0)),
                pl.BlockSpec(memory_space=pl.ANY),
                pl.BlockSpec((None, None, 1, D_MODEL), lambda i, m: (layer, m[i], 0, 0)),
            ],
            out_specs=pl.BlockSpec((tm, D_MODEL), lambda i, m: (i, 0)),
            scratch_shapes=[pltpu.VMEM((EXPERT_FF, D_MODEL), F32), pltpu.VMEM((EXPERT_FF, D_MODEL), BF16),
                            pltpu.SemaphoreType.DMA(())],
        ),
        out_shape=jax.ShapeDtypeStruct((n_blocks * tm, D_MODEL), F32),
        compiler_params=_cparams(("arbitrary",)),
        name="expert_down",
    )(meta, act, w_down, b3)


def _combine_kernel(dest_ref, y_hbm, gt_ref, x_ref, g_ref, o_ref, buf, sems, *, rows, n_rows):
    i = pl.program_id(0)
    n_steps = pl.num_programs(0)

    def issue(step, slot):
        base = step * rows

        def body(r, carry):
            for k in range(TOP_K):
                d = dest_ref[k * n_rows + base + r]
                pltpu.make_async_copy(y_hbm.at[pl.ds(d, 1), :], buf.at[slot, k, pl.ds(r, 1), :],
                                      sems.at[slot]).start()
            return carry

        lax.fori_loop(0, rows, body, 0)

    @pl.when(i == 0)
    def _():
        issue(0, 0)

    @pl.when(i + 1 < n_steps)
    def _():
        issue(i + 1, (i + 1) % 2)

    slot = i % 2
    for k in range(TOP_K):
        pltpu.make_async_copy(y_hbm.at[pl.ds(0, rows), :], buf.at[slot, k], sems.at[slot]).wait()
    gt = gt_ref[...]
    moe = gt[:, 0:1] * buf[slot, 0]
    for k in range(1, TOP_K):
        moe = moe + gt[:, k:k + 1] * buf[slot, k]
    o_ref[...] = x_ref[...] + g_ref[...] * moe


def _combine(dest_flat, ys, gates_t, tok, mod3, n_rows):
    rows = 128
    return pl.pallas_call(
        functools.partial(_combine_kernel, rows=rows, n_rows=n_rows),
        grid_spec=pltpu.PrefetchScalarGridSpec(
            num_scalar_prefetch=1,
            grid=(n_rows // rows,),
            in_specs=[
                pl.BlockSpec(memory_space=pl.ANY),
                pl.BlockSpec((rows, LANES), lambda i, d: (i, 0)),
                pl.BlockSpec((rows, D_MODEL), lambda i, d: (i, 0)),
                pl.BlockSpec((None, 1, D_MODEL), lambda i, d: (_mod_row(i, rows), 0, 5)),
            ],
            out_specs=pl.BlockSpec((rows, D_MODEL), lambda i, d: (i, 0)),
            scratch_shapes=[pltpu.VMEM((2, TOP_K, rows, D_MODEL), F32), pltpu.SemaphoreType.DMA((2,))],
        ),
        out_shape=jax.ShapeDtypeStruct((n_rows, D_MODEL), F32),
        compiler_params=_cparams(("arbitrary",)),
        name="combine",
    )(dest_flat, ys, gates_t, tok, mod3)


def _routing(top_idx, rank, counts, n_rows):
    tm = EXPERT_TM
    n_blocks = -(-(TOP_K * n_rows) // tm) + N_EXPERTS
    counts = counts[:, 0]
    padded = (counts + tm - 1) // tm * tm
    padded_end = jnp.cumsum(padded)
    padded_start = padded_end - padded
    experts = jnp.arange(N_EXPERTS, dtype=jnp.int32)
    start_of = jnp.sum(jnp.where(top_idx[:, :, None] == experts, padded_start, 0), axis=-1)
    dest = (start_of + rank).astype(jnp.int32).reshape(TOP_K * n_rows)
    block_start = jnp.arange(n_blocks, dtype=jnp.int32) * tm
    block_expert = jnp.minimum(jnp.sum(padded_end[None, :] <= block_start[:, None], axis=1), N_EXPERTS - 1)
    n_active = padded_end[-1:] // tm
    group_end = jnp.sum(jnp.where(block_expert[:, None] == experts, padded_end // tm, 0), axis=1)
    follower = jnp.sum(jnp.where(group_end[:, None] == jnp.arange(n_blocks)[None, :], block_expert, 0), axis=1)
    next_expert = jnp.where(group_end < n_active, follower, -1)
    meta = jnp.concatenate([block_expert, n_active, next_expert]).astype(jnp.int32)
    n_pad = jnp.full((1,), n_blocks * tm, jnp.int32)
    pad_meta = jnp.concatenate([padded_start + counts, padded_end[-1:], padded_end, n_pad]).astype(jnp.int32)
    return dest, meta, pad_meta, n_blocks


def _rope_tables():
    rows = SEQ // GRID_W
    row = jnp.broadcast_to(jnp.arange(rows, dtype=F32)[:, None], (rows, GRID_W)).reshape(-1)
    col = jnp.broadcast_to(jnp.arange(GRID_W, dtype=F32)[None, :], (rows, GRID_W)).reshape(-1)
    inv_freq = ROPE_THETA ** (-jnp.arange(ROPE_FREQS, dtype=F32) / ROPE_FREQS)
    ang_r = row[:, None] * inv_freq
    ang_c = col[:, None] * inv_freq
    cos = jnp.concatenate([jnp.cos(ang_r), jnp.cos(ang_r), jnp.cos(ang_c), jnp.cos(ang_c)], axis=-1)
    sin = jnp.concatenate([-jnp.sin(ang_r), jnp.sin(ang_r), -jnp.sin(ang_c), jnp.sin(ang_c)], axis=-1)
    return cos, sin


def kernel(x, c, ctx, c_ctx, ada_w, ada_b, norm1_g, norm2_g, w_in, q_norm_g, k_norm_g, pool_w, pool_scale,
           sgu_norm_g, sgu_w, sgu_b, w_out, router_w, router_b, w_gu, b_gu, w_down, b_down):
    cos, sin = _rope_tables()
    tok = jnp.concatenate([x.reshape(N_LAT, D_MODEL), ctx.reshape(N_CTX, D_MODEL)], axis=0)
    cc = jnp.zeros((MOD_ROWS, D_MODEL), F32).at[:BATCH].set(c).at[BATCH].set(c_ctx)
    mod = _adaln(cc, ada_w, ada_b)
    tri = jnp.triu(jnp.ones((ROUTER_TM, ROUTER_TM), BF16), k=1)
    for l in range(DEPTH):
        last = l == DEPTH - 1
        n_rows = N_LAT if last else N_TOK
        mod3 = mod[l].reshape(MOD_ROWS, 1, 6 * D_MODEL)
        proj = _in_proj(tok, norm1_g[l], mod3, w_in[l].astype(BF16))
        attn = _latent_attention(proj, cos, sin, q_norm_g[l], k_norm_g[l])
        if not last:
            attn = jnp.concatenate([attn, _context_attention(proj, q_norm_g[l], k_norm_g[l])], axis=0)
        sgu_b_full = jnp.broadcast_to(sgu_b[l][:, :, None], (N_SGU_HEADS, CHUNK, LANES))
        mix = _mixers(proj, pool_w[l].astype(BF16), pool_scale[l], sgu_norm_g[l], sgu_w[l].astype(BF16), sgu_b_full)
        tok = _out_proj(attn, mix, w_out[l].astype(BF16), tok, mod3, n_rows)
        rw_pad = jnp.zeros((D_MODEL, LANES), F32).at[:, :N_EXPERTS].set(router_w[l])
        rb_pad = jnp.zeros((1, LANES), F32).at[0, :N_EXPERTS].set(router_b[l])
        hp, top_idx, rank, gates_t, counts = _router(tok, norm2_g[l], mod3, rw_pad, rb_pad, tri, n_rows)
        dest, meta, pad_meta, n_blocks = _routing(top_idx, rank, counts, n_rows)
        n_pad = n_blocks * EXPERT_TM
        xs = _dispatch(dest, pad_meta, hp.reshape(n_rows, PACK_TILES, LANES), n_pad, n_rows)
        act = _expert_gu(meta, xs.reshape(n_pad * PACK_TILES, LANES), w_gu, b_gu, l, n_blocks)
        ys = _expert_down(meta, act, w_down, b_down, l, n_blocks)
        tok = _combine(dest, ys, gates_t, tok, mod3, n_rows)
    return tok.reshape(BATCH, SEQ, D_MODEL)
```

```python
import functools

import jax
import jax.numpy as jnp
from jax import lax
from jax.experimental import pallas as pl
from jax.experimental.pallas import tpu as pltpu

F32 = jnp.float32
BF16 = jnp.bfloat16
U32 = jnp.uint32

D_MODEL = 2048
BATCH = 4
SEQ = 4096
DEPTH = 2
GRID_W = 64
CTX_LEN = 256
HEAD_DIM = 128
ATTN_W = 1024
N_Q_HEADS = 8
GQA_GROUP = 4
N_KV_HEADS = 2
KV_W = 256
ROPE_THETA = 10000.0
ROPE_FREQS = 32
ATTN_SCALE = HEAD_DIM ** -0.5
LOG2_E = 1.4426950408889634
ATTN_TK = 512
POOL_WINDOWS = (2, 4, 8, 16)
POOL_W = 512
SGU_W = 512
N_SGU_HEADS = 4
CHUNK = 128
IN_W = 3072
N_EXPERTS = 32
TOP_K = 4
EXPERT_FF = 2048
SWIGLU_LIMIT = 7.0
SWIGLU_ALPHA = 1.702
EPS = 1e-6

N_LAT = BATCH * SEQ
N_CTX = BATCH * CTX_LEN
N_TOK = N_LAT + N_CTX
MOD_ROWS = 8
LANES = 128
POOL_HALO = 8
MIX_TM = 256
EXPERT_TM = 256
ROUTER_TM = 512
HALF_D = D_MODEL // 2
PACK_TILES = HALF_D // LANES
VMEM_LIMIT = 56 * 1024 * 1024


def _cparams(sem, vmem=VMEM_LIMIT):
    return pltpu.CompilerParams(dimension_semantics=sem, vmem_limit_bytes=vmem)


def _mod_row(row_tile, tm):
    return jnp.minimum(row_tile * tm // SEQ, BATCH)


def _rms(x, g):
    return x * lax.rsqrt(jnp.mean(x * x, axis=-1, keepdims=True) + EPS) * g


def _sigmoid(x):
    return 1.0 / (1.0 + jnp.exp(-x))


def _gelu(x):
    return 0.5 * x * (1.0 + lax.erf(x * 0.7071067811865476))


def _adaln_kernel(cc_ref, w_ref, b_ref, o_ref):
    cc = cc_ref[...]
    s = (cc * _sigmoid(cc)).astype(BF16)
    o_ref[...] = jnp.dot(s, w_ref[...].astype(BF16), preferred_element_type=F32) + b_ref[...]


def _adaln(cc, ada_w, ada_b):
    tn = 1024
    n = 6 * D_MODEL
    return pl.pallas_call(
        _adaln_kernel,
        grid=(DEPTH, n // tn),
        in_specs=[
            pl.BlockSpec((MOD_ROWS, D_MODEL), lambda l, j: (0, 0)),
            pl.BlockSpec((None, D_MODEL, tn), lambda l, j: (l, 0, j)),
            pl.BlockSpec((None, 1, tn), lambda l, j: (l, 0, j)),
        ],
        out_specs=pl.BlockSpec((None, MOD_ROWS, tn), lambda l, j: (l, 0, j)),
        out_shape=jax.ShapeDtypeStruct((DEPTH, MOD_ROWS, n), F32),
        compiler_params=_cparams(("arbitrary", "arbitrary")),
        name="adaln",
    )(cc, ada_w, ada_b.reshape(DEPTH, 1, n))


def _proj_kernel(*refs, n_first):
    if n_first is None:
        (xa_ref, g_ref, sh_ref, sc_ref, w_ref, o_ref, h_s), xb_ref = refs, None
    else:
        xa_ref, xb_ref, g_ref, sh_ref, sc_ref, w_ref, o_ref, h_s = refs
    i, j = pl.program_id(0), pl.program_id(1)

    def prepare(tile, slot):
        x = xa_ref[...]
        if xb_ref is not None:
            x = jnp.where(tile < n_first, x, xb_ref[...])
        y = _rms(x, g_ref[...])
        h_s[slot] = (y * (1.0 + sc_ref[...]) + sh_ref[...]).astype(BF16)

    @pl.when(jnp.logical_and(i == 0, j == 0))
    def _():
        prepare(0, 0)

    last = j == pl.num_programs(1) - 1

    @pl.when(last)
    def _():
        o_ref[...] = jnp.dot(h_s[i % 2], w_ref[...], preferred_element_type=F32)
        prepare(i + 1, (i + 1) % 2)

    @pl.when(jnp.logical_not(last))
    def _():
        o_ref[...] = jnp.dot(h_s[i % 2], w_ref[...], preferred_element_type=F32)


def _in_proj(src, norm_g, mod3, w_bf):
    tm, tn = 512, 1024
    n_tiles, nj = N_TOK // tm, IN_W // tn

    def tile(i, j):
        return jnp.minimum(i + (j + 1) // nj, n_tiles - 1)

    if isinstance(src, tuple):
        n_first = N_LAT // tm
        x_specs = [pl.BlockSpec((tm, D_MODEL), lambda i, j: (jnp.minimum(tile(i, j), n_first - 1), 0)),
                   pl.BlockSpec((tm, D_MODEL), lambda i, j: (jnp.maximum(tile(i, j) - n_first, 0), 0))]
    else:
        n_first, src = None, (src,)
        x_specs = [pl.BlockSpec((tm, D_MODEL), lambda i, j: (tile(i, j), 0))]
    return pl.pallas_call(
        functools.partial(_proj_kernel, n_first=n_first),
        grid=(n_tiles, nj),
        in_specs=x_specs + [
            pl.BlockSpec((1, D_MODEL), lambda i, j: (0, 0)),
            pl.BlockSpec((None, 1, D_MODEL), lambda i, j: (_mod_row(tile(i, j), tm), 0, 0)),
            pl.BlockSpec((None, 1, D_MODEL), lambda i, j: (_mod_row(tile(i, j), tm), 0, 1)),
            pl.BlockSpec((D_MODEL, tn), lambda i, j: (0, j)),
        ],
        out_specs=pl.BlockSpec((tm, tn), lambda i, j: (i, j)),
        out_shape=jax.ShapeDtypeStruct((N_TOK, IN_W), F32),
        scratch_shapes=[pltpu.VMEM((2, tm, D_MODEL), BF16)],
        compiler_params=_cparams(("arbitrary", "arbitrary")),
        name="in_proj",
    )(*src, norm_g.reshape(1, D_MODEL), mod3, mod3, w_bf)


def _rope(x, cos, sin):
    lane = lax.broadcasted_iota(jnp.int32, x.shape, 1)
    first = (lane % 64) < 32
    partner = jnp.where(first, pltpu.roll(x, 96, 1), pltpu.roll(x, 32, 1))
    return x * cos + partner * sin


def _softmax_pv(q, k, v):
    s = lax.dot_general(q, k, (((1,), (1,)), ((), ())), preferred_element_type=F32) * ATTN_SCALE
    m = jnp.max(s, axis=-1, keepdims=True)
    p = jnp.exp(s - m)
    l = jnp.sum(p, axis=-1, keepdims=True)
    return jnp.dot(p.astype(BF16), v, preferred_element_type=F32) / l


def _lat_attn_kernel(q_ref, kl_ref, vl_ref, kc_ref, vc_ref, cosq_ref, sinq_ref, cosk_ref, sink_ref,
                     qg_ref, kg_ref, o_ref, k_s, v_s, s_s):
    @pl.when(pl.program_id(2) == 0)
    def _():
        k_s[0:CTX_LEN, :] = _rms(kc_ref[...], kg_ref[...]).astype(BF16)
        kl = _rope(_rms(kl_ref[...], kg_ref[...]), cosk_ref[...], sink_ref[...])
        k_s[CTX_LEN:, :] = kl.astype(BF16)
        v_s[0:CTX_LEN, 0:HEAD_DIM] = vc_ref[...].astype(BF16)
        v_s[CTX_LEN:, 0:HEAD_DIM] = vl_ref[...].astype(BF16)
        v_s[:, HEAD_DIM:] = jnp.ones((CTX_LEN + SEQ, LANES), BF16)

    def scores(g):
        lanes = slice(g * HEAD_DIM, (g + 1) * HEAD_DIM)
        q = _rope(_rms(q_ref[:, lanes], qg_ref[...]), cosq_ref[...], sinq_ref[...])
        q = (q * (ATTN_SCALE * LOG2_E)).astype(BF16)
        s_s[g % 2] = lax.dot_general(q, k_s[...], (((1,), (1,)), ((), ())), preferred_element_type=F32)

    scores(0)
    for g in range(GQA_GROUP):
        if g + 1 < GQA_GROUP:
            scores(g + 1)
        s = s_s[g % 2]
        p = jnp.exp2(s - jnp.max(s, axis=-1, keepdims=True))
        ol = jnp.dot(p.astype(BF16), v_s[...], preferred_element_type=F32)
        o = ol[:, 0:HEAD_DIM] / ol[:, HEAD_DIM:]
        o_ref[:, g * HEAD_DIM:(g + 1) * HEAD_DIM] = o.astype(o_ref.dtype)


def _latent_attention(proj, cos, sin, q_g, k_g):
    tq = 256
    nq = SEQ // tq
    qw = GQA_GROUP * HEAD_DIM
    kcol = ATTN_W // HEAD_DIM
    vcol = (ATTN_W + KV_W) // HEAD_DIM
    ctx_blk = N_LAT // CTX_LEN
    return pl.pallas_call(
        _lat_attn_kernel,
        grid=(BATCH, N_KV_HEADS, nq),
        in_specs=[
            pl.BlockSpec((tq, qw), lambda b, h, i: (b * nq + i, h)),
            pl.BlockSpec((SEQ, HEAD_DIM), lambda b, h, i: (b, kcol + h)),
            pl.BlockSpec((SEQ, HEAD_DIM), lambda b, h, i: (b, vcol + h)),
            pl.BlockSpec((CTX_LEN, HEAD_DIM), lambda b, h, i: (ctx_blk + b, kcol + h)),
            pl.BlockSpec((CTX_LEN, HEAD_DIM), lambda b, h, i: (ctx_blk + b, vcol + h)),
            pl.BlockSpec((tq, HEAD_DIM), lambda b, h, i: (i, 0)),
            pl.BlockSpec((tq, HEAD_DIM), lambda b, h, i: (i, 0)),
            pl.BlockSpec((SEQ, HEAD_DIM), lambda b, h, i: (0, 0)),
            pl.BlockSpec((SEQ, HEAD_DIM), lambda b, h, i: (0, 0)),
            pl.BlockSpec((1, HEAD_DIM), lambda b, h, i: (0, 0)),
            pl.BlockSpec((1, HEAD_DIM), lambda b, h, i: (0, 0)),
        ],
        out_specs=pl.BlockSpec((tq, qw), lambda b, h, i: (b * nq + i, h)),
        out_shape=jax.ShapeDtypeStruct((N_LAT, ATTN_W), BF16),
        scratch_shapes=[pltpu.VMEM((CTX_LEN + SEQ, HEAD_DIM), BF16),
                        pltpu.VMEM((CTX_LEN + SEQ, HEAD_DIM + LANES), BF16),
                        pltpu.VMEM((2, tq, CTX_LEN + SEQ), F32)],
        compiler_params=_cparams(("arbitrary", "arbitrary", "arbitrary")),
        name="latent_attention",
    )(proj, proj, proj, proj, proj, cos, sin, cos, sin, q_g.reshape(1, HEAD_DIM), k_g.reshape(1, HEAD_DIM))


def _ctx_attn_kernel(q_ref, k_ref, v_ref, qg_ref, kg_ref, o_ref):
    k = _rms(k_ref[...], kg_ref[...]).astype(BF16)
    v = v_ref[...].astype(BF16)
    for g in range(GQA_GROUP):
        lanes = slice(g * HEAD_DIM, (g + 1) * HEAD_DIM)
        q = _rms(q_ref[:, lanes], qg_ref[...]).astype(BF16)
        o_ref[:, lanes] = _softmax_pv(q, k, v).astype(o_ref.dtype)


def _context_attention(proj, q_g, k_g):
    qw = GQA_GROUP * HEAD_DIM
    kcol = ATTN_W // HEAD_DIM
    vcol = (ATTN_W + KV_W) // HEAD_DIM
    ctx_blk = N_LAT // CTX_LEN
    return pl.pallas_call(
        _ctx_attn_kernel,
        grid=(BATCH, N_KV_HEADS),
        in_specs=[
            pl.BlockSpec((CTX_LEN, qw), lambda b, h: (ctx_blk + b, h)),
            pl.BlockSpec((CTX_LEN, HEAD_DIM), lambda b, h: (ctx_blk + b, kcol + h)),
            pl.BlockSpec((CTX_LEN, HEAD_DIM), lambda b, h: (ctx_blk + b, vcol + h)),
            pl.BlockSpec((1, HEAD_DIM), lambda b, h: (0, 0)),
            pl.BlockSpec((1, HEAD_DIM), lambda b, h: (0, 0)),
        ],
        out_specs=pl.BlockSpec((CTX_LEN, qw), lambda b, h: (b, h)),
        out_shape=jax.ShapeDtypeStruct((N_CTX, ATTN_W), BF16),
        compiler_params=_cparams(("arbitrary", "arbitrary")),
        name="context_attention",
    )(proj, proj, proj, q_g.reshape(1, HEAD_DIM), k_g.reshape(1, HEAD_DIM))


def _mixer_kernel(pin_ref, prev_ref, next_ref, su_ref, sv_ref, pw_ref, ps_ref, sg_ref, sw_ref, sb_ref,
                  o_ref, pad_ref):
    tm = MIX_TM
    i = pl.program_id(0)
    is_lat = i < N_LAT // tm
    pos0 = jnp.where(is_lat, (i % (SEQ // tm)) * tm, 0)
    seq_len = jnp.where(is_lat, SEQ, CTX_LEN)
    pad_ref[0:POOL_HALO, :] = jnp.where(pos0 == 0, 0.0, prev_ref[...])
    pad_ref[POOL_HALO:POOL_HALO + tm, :] = pin_ref[...]
    pad_ref[POOL_HALO + tm:, :] = jnp.where(pos0 + tm == seq_len, 0.0, next_ref[...])
    t = pos0 + lax.broadcasted_iota(jnp.int32, (tm, LANES), 0)
    for gi, w in enumerate(POOL_WINDOWS):
        lanes = slice(gi * LANES, (gi + 1) * LANES)
        acc = pad_ref[POOL_HALO - w // 2:POOL_HALO - w // 2 + tm, lanes]
        for d in range(-w // 2 + 1, w // 2):
            acc = acc + pad_ref[POOL_HALO + d:POOL_HALO + d + tm, lanes]
        cnt = (jnp.minimum(t + w // 2, seq_len) - jnp.maximum(t - w // 2, 0)).astype(F32)
        mixed = acc / cnt - pin_ref[:, lanes]
        y = jnp.dot(mixed.astype(BF16), pw_ref[gi], preferred_element_type=F32) * ps_ref[:, lanes]
        o_ref[:, lanes] = y.astype(o_ref.dtype)

    for h in range(N_SGU_HEADS):
        lanes = slice(h * LANES, (h + 1) * LANES)
        gu = _gelu(su_ref[:, lanes])
        vh = _rms(_gelu(sv_ref[:, lanes]), sg_ref[h:h + 1, :]).astype(BF16)
        for n in range(tm // CHUNK):
            rows = slice(n * CHUNK, (n + 1) * CHUNK)
            mixed = jnp.dot(sw_ref[h], vh[rows], preferred_element_type=F32) + sb_ref[h]
            o_ref[rows, POOL_W + h * LANES:POOL_W + (h + 1) * LANES] = (gu[rows] * mixed).astype(o_ref.dtype)


def _mixers(proj, pool_w_bf, pool_scale, sgu_norm_g, sgu_w_bf, sgu_b_full):
    tm = MIX_TM
    per_tile = tm // POOL_HALO
    last_halo = N_TOK // POOL_HALO - 1
    pcol = (ATTN_W + 2 * KV_W) // POOL_W
    return pl.pallas_call(
        _mixer_kernel,
        grid=(N_TOK // tm,),
        in_specs=[
            pl.BlockSpec((tm, POOL_W), lambda i: (i, pcol)),
            pl.BlockSpec((POOL_HALO, POOL_W), lambda i: (jnp.maximum(i * per_tile - 1, 0), pcol)),
            pl.BlockSpec((POOL_HALO, POOL_W), lambda i: (jnp.minimum((i + 1) * per_tile, last_halo), pcol)),
            pl.BlockSpec((tm, SGU_W), lambda i: (i, pcol + 1)),
            pl.BlockSpec((tm, SGU_W), lambda i: (i, pcol + 2)),
            pl.BlockSpec((len(POOL_WINDOWS), LANES, LANES), lambda i: (0, 0, 0)),
            pl.BlockSpec((1, POOL_W), lambda i: (0, 0)),
            pl.BlockSpec((N_SGU_HEADS, LANES), lambda i: (0, 0)),
            pl.BlockSpec((N_SGU_HEADS, CHUNK, CHUNK), lambda i: (0, 0, 0)),
            pl.BlockSpec((N_SGU_HEADS, CHUNK, LANES), lambda i: (0, 0, 0)),
        ],
        out_specs=pl.BlockSpec((tm, POOL_W + SGU_W), lambda i: (i, 0)),
        out_shape=jax.ShapeDtypeStruct((N_TOK, POOL_W + SGU_W), BF16),
        scratch_shapes=[pltpu.VMEM((tm + 2 * POOL_HALO, POOL_W), F32)],
        compiler_params=_cparams(("arbitrary",)),
        name="mixers",
    )(proj, proj, proj, proj, proj, pool_w_bf, pool_scale.reshape(1, POOL_W), sgu_norm_g, sgu_w_bf, sgu_b_full)


def _outproj_kernel(*refs, n_first):
    if n_first is None:
        a_ref, m_ref, w1_ref, w2_ref, x_ref, g_ref, o_ref = refs
        a, x = a_ref[...], x_ref[...]
    else:
        a_ref, ac_ref, m_ref, w1_ref, w2_ref, x_ref, xc_ref, g_ref, o_ref = refs
        is_first = pl.program_id(1) < n_first
        a = jnp.where(is_first, a_ref[...], ac_ref[...])
        x = jnp.where(is_first, x_ref[...], xc_ref[...])
    y = jnp.dot(a, w1_ref[...], preferred_element_type=F32)
    y = y + jnp.dot(m_ref[...], w2_ref[...], preferred_element_type=F32)
    o_ref[...] = x + g_ref[...] * y


def _out_proj(attn, mix, w_bf, src, mod3, n_rows):
    tm, tn = 512, 1024
    nj = D_MODEL // tn
    assert isinstance(attn, tuple) == isinstance(src, tuple)
    if isinstance(src, tuple):
        n_first = N_LAT // tm

        def first(i):
            return jnp.minimum(i, n_first - 1)

        def second(i):
            return jnp.maximum(i - n_first, 0)

        attn_specs = [pl.BlockSpec((tm, ATTN_W), lambda j, i: (first(i), 0)),
                      pl.BlockSpec((tm, ATTN_W), lambda j, i: (second(i), 0))]
        x_specs = [pl.BlockSpec((tm, tn), lambda j, i: (first(i), j)),
                   pl.BlockSpec((tm, tn), lambda j, i: (second(i), j))]
    else:
        n_first, attn, src = None, (attn,), (src,)
        attn_specs = [pl.BlockSpec((tm, ATTN_W), lambda j, i: (i, 0))]
        x_specs = [pl.BlockSpec((tm, tn), lambda j, i: (i, j))]
    return pl.pallas_call(
        functools.partial(_outproj_kernel, n_first=n_first),
        grid=(nj, n_rows // tm),
        in_specs=attn_specs + [
            pl.BlockSpec((tm, POOL_W + SGU_W), lambda j, i: (i, 0)),
            pl.BlockSpec((ATTN_W, tn), lambda j, i: (0, j)),
            pl.BlockSpec((POOL_W + SGU_W, tn), lambda j, i: (1, j)),
        ] + x_specs + [
            pl.BlockSpec((None, 1, tn), lambda j, i: (_mod_row(i, tm), 0, 2 * nj + j)),
        ],
        out_specs=pl.BlockSpec((tm, tn), lambda j, i: (i, j)),
        out_shape=jax.ShapeDtypeStruct((n_rows, D_MODEL), F32),
        compiler_params=_cparams(("arbitrary", "arbitrary")),
        name="out_proj",
    )(*attn, mix, w_bf, w_bf, *src, mod3)


def _router_kernel(x_ref, g_ref, sh_ref, sc_ref, rw_ref, rb_ref, tri_ref,
                   hp_ref, idx_ref, rank_ref, gate_ref, cnt_ref, run_s):
    @pl.when(pl.program_id(0) == 0)
    def _():
        run_s[...] = jnp.zeros_like(run_s)

    h = _rms(x_ref[...], g_ref[...]) * (1.0 + sc_ref[...]) + sh_ref[...]
    h_hi = h.astype(BF16)
    hb = h_hi.astype(F32)
    hi = lax.bitcast_convert_type(hb[:, :HALF_D], U32)
    lo = lax.bitcast_convert_type(hb[:, HALF_D:], U32)
    packed = hi | (lo >> 16)
    for j in range(PACK_TILES):
        hp_ref[pl.ds(j, packed.shape[0], stride=PACK_TILES), :] = packed[:, j * LANES:(j + 1) * LANES]

    h_lo = (h - hb).astype(BF16)
    both = jnp.dot(h_hi, rw_ref[...], preferred_element_type=F32)
    tail = jnp.dot(h_lo, rw_ref[:, 0:LANES], preferred_element_type=F32)
    logits = (both[:, 0:LANES] + both[:, LANES:]) + tail
    lt = (logits + rb_ref[...]).T[0:N_EXPERTS, :]
    expert = lax.broadcasted_iota(jnp.int32, lt.shape, 0).astype(F32)
    vals, idxs = [], []
    for _ in range(TOP_K):
        m = jnp.max(lt, axis=0, keepdims=True)
        idx = jnp.min(jnp.where(lt == m, expert, float(N_EXPERTS)), axis=0, keepdims=True)
        vals.append(m)
        idxs.append(idx)
        lt = jnp.where(expert == idx, -jnp.inf, lt)

    e = [jnp.exp(v - vals[0]) for v in vals]
    den = e[0] + e[1] + e[2] + e[3]
    slot = lax.broadcasted_iota(jnp.int32, (LANES, lt.shape[1]), 0)
    gates = jnp.zeros((LANES, lt.shape[1]), F32)
    for k in range(TOP_K):
        gates = jnp.where(slot == k, e[k] / den, gates)
    gate_ref[...] = gates.T

    base = run_s[...]
    for k in range(TOP_K):
        onehot = jnp.where(expert == idxs[k], 1.0, 0.0)
        before = jnp.dot(onehot.astype(BF16), tri_ref[...], preferred_element_type=F32)
        rank = jnp.sum(onehot * (before + base[:, 0:1]), axis=0, keepdims=True)
        idx_ref[k:k + 1, :] = idxs[k].astype(jnp.int32)
        rank_ref[k:k + 1, :] = rank.astype(jnp.int32)
        base = base + jnp.sum(onehot, axis=1, keepdims=True)
    run_s[...] = base
    cnt_ref[...] = base.astype(jnp.int32)


def _router(tok, norm_g, mod3, rw_pad, rb_pad, tri, n_rows):
    tm = ROUTER_TM
    return pl.pallas_call(
        _router_kernel,
        grid=(n_rows // tm,),
        in_specs=[
            pl.BlockSpec((tm, D_MODEL), lambda i: (i, 0)),
            pl.BlockSpec((1, D_MODEL), lambda i: (0, 0)),
            pl.BlockSpec((None, 1, D_MODEL), lambda i: (_mod_row(i, tm), 0, 3)),
            pl.BlockSpec((None, 1, D_MODEL), lambda i: (_mod_row(i, tm), 0, 4)),
            pl.BlockSpec((D_MODEL, 2 * LANES), lambda i: (0, 0)),
            pl.BlockSpec((1, LANES), lambda i: (0, 0)),
            pl.BlockSpec((tm, tm), lambda i: (0, 0)),
        ],
        out_specs=[
            pl.BlockSpec((tm * PACK_TILES, LANES), lambda i: (i, 0)),
            pl.BlockSpec((TOP_K, tm), lambda i: (0, i)),
            pl.BlockSpec((TOP_K, tm), lambda i: (0, i)),
            pl.BlockSpec((tm, LANES), lambda i: (i, 0)),
            pl.BlockSpec((N_EXPERTS, LANES), lambda i: (0, 0)),
        ],
        out_shape=[
            jax.ShapeDtypeStruct((n_rows * PACK_TILES, LANES), U32),
            jax.ShapeDtypeStruct((TOP_K, n_rows), jnp.int32),
            jax.ShapeDtypeStruct((TOP_K, n_rows), jnp.int32),
            jax.ShapeDtypeStruct((n_rows, LANES), F32),
            jax.ShapeDtypeStruct((N_EXPERTS, LANES), jnp.int32),
        ],
        scratch_shapes=[pltpu.VMEM((N_EXPERTS, LANES), F32)],
        compiler_params=_cparams(("arbitrary",)),
        name="router",
    )(tok, norm_g.reshape(1, D_MODEL), mod3, mod3, rw_pad, rb_pad, tri)


def _dispatch_kernel(dest_ref, pad_ref, hp_ref, xs_hbm, sem, pad_sem, *, rows, n_rows):
    i = pl.program_id(0)

    @pl.when(i == 0)
    def _():
        def per_expert(e, carry):
            lo, hi = pad_ref[e], pad_ref[N_EXPERTS + 1 + e]

            def start(p, c):
                pltpu.make_async_copy(hp_ref.at[0], xs_hbm.at[p], pad_sem).start()
                return c

            def wait(p, c):
                pltpu.make_async_copy(hp_ref.at[0], xs_hbm.at[p], pad_sem).wait()
                return c

            lax.fori_loop(lo, hi, start, 0)
            lax.fori_loop(lo, hi, wait, 0)
            return carry

        lax.fori_loop(0, N_EXPERTS + 1, per_expert, 0)

    base = i * rows

    def issue(r, carry):
        for k in range(TOP_K):
            d = dest_ref[k * n_rows + base + r]
            pltpu.make_async_copy(hp_ref.at[r], xs_hbm.at[d], sem).start()
        return carry

    lax.fori_loop(0, rows, issue, 0)
    for k in range(TOP_K):
        pltpu.make_async_copy(hp_ref, xs_hbm.at[pl.ds(0, rows)], sem).wait()


def _dispatch(dest_flat, pad_meta, hp, n_pad, n_rows):
    rows = 1024
    return pl.pallas_call(
        functools.partial(_dispatch_kernel, rows=rows, n_rows=n_rows),
        grid_spec=pltpu.PrefetchScalarGridSpec(
            num_scalar_prefetch=2,
            grid=(n_rows // rows,),
            in_specs=[pl.BlockSpec((rows, PACK_TILES, LANES), lambda i, d, p: (i, 0, 0))],
            out_specs=pl.BlockSpec(memory_space=pl.ANY),
            scratch_shapes=[pltpu.SemaphoreType.DMA(()), pltpu.SemaphoreType.DMA(())],
        ),
        out_shape=jax.ShapeDtypeStruct((n_pad, PACK_TILES, LANES), U32),
        compiler_params=_cparams(("arbitrary",)),
        name="dispatch",
    )(dest_flat, pad_meta, hp)


def _is_new_expert(meta_ref, i):
    prev = meta_ref[jnp.maximum(i - 1, 0)]
    return jnp.logical_or(i == 0, meta_ref[i] != prev)


def _cast_rows(src_ref, dst_ref, chunk=256):
    def body(c, carry):
        rows = pl.ds(pl.multiple_of(c * chunk, chunk), chunk)
        dst_ref[rows, :] = src_ref[rows, :].astype(dst_ref.dtype)
        return carry

    lax.fori_loop(0, src_ref.shape[0] // chunk, body, 0)


def _unpack_rows(x_ref):
    n = x_ref.shape[0] // PACK_TILES
    xp = jnp.concatenate([x_ref[pl.ds(j, n, stride=PACK_TILES), :] for j in range(PACK_TILES)], axis=1)
    hi = lax.bitcast_convert_type(xp & jnp.uint32(0xFFFF0000), F32).astype(BF16)
    lo = lax.bitcast_convert_type(xp << 16, F32).astype(BF16)
    return jnp.concatenate([hi, lo], axis=1)


def _expert_gu_kernel(meta_ref, x_ref, w_hbm, bg_ref, bu_ref, o_ref, wg_f, wu_f, wg_s, wu_s, sems,
                      *, n_blocks, layer):
    j, i = pl.program_id(0), pl.program_id(1)
    tn = wg_f.shape[1]
    nj = pl.num_programs(0)

    def fetch(e, jj):
        col = pl.multiple_of(jj * tn, tn)
        return (pltpu.make_async_copy(w_hbm.at[layer, e, :, pl.ds(col, tn)], wg_f, sems.at[0]),
                pltpu.make_async_copy(w_hbm.at[layer, e, :, pl.ds(EXPERT_FF + col, tn)], wu_f, sems.at[1]))

    @pl.when(jnp.logical_and(j == 0, i == 0))
    def _():
        for cp in fetch(meta_ref[0], 0):
            cp.start()

    active = i < meta_ref[n_blocks]

    @pl.when(jnp.logical_and(active, _is_new_expert(meta_ref, i)))
    def _():
        for cp in fetch(meta_ref[i], j):
            cp.wait()
        _cast_rows(wg_f, wg_s)
        _cast_rows(wu_f, wu_s)
        nxt = meta_ref[n_blocks + 1 + i]

        @pl.when(nxt >= 0)
        def _():
            for cp in fetch(nxt, j):
                cp.start()

        @pl.when(jnp.logical_and(nxt < 0, j + 1 < nj))
        def _():
            for cp in fetch(meta_ref[0], j + 1):
                cp.start()

    @pl.when(active)
    def _():
        x = _unpack_rows(x_ref)
        g = jnp.dot(x, wg_s[...], preferred_element_type=F32) + bg_ref[...]
        u = jnp.dot(x, wu_s[...], preferred_element_type=F32) + bu_ref[...]
        g = jnp.minimum(g, SWIGLU_LIMIT)
        u = jnp.clip(u, -SWIGLU_LIMIT, SWIGLU_LIMIT)
        o_ref[...] = ((u + 1.0) * (g * _sigmoid(g * SWIGLU_ALPHA))).astype(o_ref.dtype)

    @pl.when(jnp.logical_not(active))
    def _():
        o_ref[...] = jnp.zeros_like(o_ref)


def _active_block(i, m, n_blocks):
    return jnp.minimum(i, m[n_blocks] - 1)


def _expert_gu(meta, xs, w_gu, b_gu, layer, n_blocks):
    tm, tn = EXPERT_TM, 1024
    nj = EXPERT_FF // tn
    b3 = b_gu.reshape(DEPTH, N_EXPERTS, 1, 2 * EXPERT_FF)
    return pl.pallas_call(
        functools.partial(_expert_gu_kernel, n_blocks=n_blocks, layer=layer),
        grid_spec=pltpu.PrefetchScalarGridSpec(
            num_scalar_prefetch=1,
            grid=(nj, n_blocks),
            in_specs=[
                pl.BlockSpec((tm * PACK_TILES, LANES), lambda j, i, m: (_active_block(i, m, n_blocks), 0)),
                pl.BlockSpec(memory_space=pl.ANY),
                pl.BlockSpec((None, None, 1, tn), lambda j, i, m: (layer, m[i], 0, j)),
                pl.BlockSpec((None, None, 1, tn), lambda j, i, m: (layer, m[i], 0, nj + j)),
            ],
            out_specs=pl.BlockSpec((tm, tn), lambda j, i, m: (i, j)),
            scratch_shapes=[pltpu.VMEM((D_MODEL, tn), F32), pltpu.VMEM((D_MODEL, tn), F32),
                            pltpu.VMEM((D_MODEL, tn), BF16), pltpu.VMEM((D_MODEL, tn), BF16),
                            pltpu.SemaphoreType.DMA((2,))],
        ),
        out_shape=jax.ShapeDtypeStruct((n_blocks * tm, EXPERT_FF), BF16),
        compiler_params=_cparams(("arbitrary", "arbitrary")),
        name="expert_gu",
    )(meta, xs, w_gu, b3, b3)


def _expert_down_kernel(meta_ref, a_ref, w_hbm, b_ref, o_ref, w_f, w_s, sem, *, n_blocks, layer):
    i = pl.program_id(0)

    def fetch(e):
        return pltpu.make_async_copy(w_hbm.at[layer, e], w_f, sem)

    @pl.when(i == 0)
    def _():
        fetch(meta_ref[0]).start()

    active = i < meta_ref[n_blocks]

    @pl.when(jnp.logical_and(active, _is_new_expert(meta_ref, i)))
    def _():
        fetch(meta_ref[i]).wait()
        _cast_rows(w_f, w_s)
        nxt = meta_ref[n_blocks + 1 + i]

        @pl.when(nxt >= 0)
        def _():
            fetch(nxt).start()

    @pl.when(active)
    def _():
        o_ref[...] = jnp.dot(a_ref[...], w_s[...], preferred_element_type=F32) + b_ref[...]

    @pl.when(jnp.logical_not(active))
    def _():
        o_ref[...] = jnp.zeros_like(o_ref)


def _expert_down(meta, act, w_down, b_down, layer, n_blocks):
    tm = EXPERT_TM
    b3 = b_down.reshape(DEPTH, N_EXPERTS, 1, D_MODEL)
    return pl.pallas_call(
        functools.partial(_expert_down_kernel, n_blocks=n_blocks, layer=layer),
        grid_spec=pltpu.PrefetchScalarGridSpec(
            num_scalar_prefetch=1,
            grid=(n_blocks,),
            in_specs=[
                pl.BlockSpec((tm, EXPERT_FF), lambda i, m: (_active_block(i, m, n_blocks), 0)),
                pl.BlockSpec(memory_space=pl.ANY),
                pl.BlockSpec((None, None, 1, D_MODEL), lambda i, m: (layer, m[i], 0, 0)),
            ],
            out_specs=pl.BlockSpec((tm, D_MODEL), lambda i, m: (i, 0)),
            scratch_shapes=[pltpu.VMEM((EXPERT_FF, D_MODEL), F32), pltpu.VMEM((EXPERT_FF, D_MODEL), BF16),
                            pltpu.SemaphoreType.DMA(())],
        ),
        out_shape=jax.ShapeDtypeStruct((n_blocks * tm, D_MODEL), F32),
        compiler_params=_cparams(("arbitrary",)),
        name="expert_down",
    )(meta, act, w_down, b3)


def _combine_kernel(dest_ref, y_hbm, gt_ref, x_ref, g_ref, o_ref, buf, sems, *, rows, n_rows):
    i = pl.program_id(0)
    n_steps = pl.num_programs(0)

    def issue(step, slot):
        base = step * rows

        def body(r, carry):
            for k in range(TOP_K):
                d = dest_ref[k * n_rows + base + r]
                pltpu.make_async_copy(y_hbm.at[pl.ds(d, 1), :], buf.at[slot, k, pl.ds(r, 1), :],
                                      sems.at[slot]).start()
            return carry

        lax.fori_loop(0, rows, body, 0)

    @pl.when(i == 0)
    def _():
        issue(0, 0)

    @pl.when(i + 1 < n_steps)
    def _():
        issue(i + 1, (i + 1) % 2)

    slot = i % 2
    for k in range(TOP_K):
        pltpu.make_async_copy(y_hbm.at[pl.ds(0, rows), :], buf.at[slot, k], sems.at[slot]).wait()
    gt = gt_ref[...]
    moe = gt[:, 0:1] * buf[slot, 0]
    for k in range(1, TOP_K):
        moe = moe + gt[:, k:k + 1] * buf[slot, k]
    o_ref[...] = x_ref[...] + g_ref[...] * moe


def _combine(dest_flat, ys, gates_t, tok, mod3, n_rows):
    rows = 128
    return pl.pallas_call(
        functools.partial(_combine_kernel, rows=rows, n_rows=n_rows),
        grid_spec=pltpu.PrefetchScalarGridSpec(
            num_scalar_prefetch=1,
            grid=(n_rows // rows,),
            in_specs=[
                pl.BlockSpec(memory_space=pl.ANY),
                pl.BlockSpec((rows, LANES), lambda i, d: (i, 0)),
                pl.BlockSpec((rows, D_MODEL), lambda i, d: (i, 0)),
                pl.BlockSpec((None, 1, D_MODEL), lambda i, d: (_mod_row(i, rows), 0, 5)),
            ],
            out_specs=pl.BlockSpec((rows, D_MODEL), lambda i, d: (i, 0)),
            scratch_shapes=[pltpu.VMEM((2, TOP_K, rows, D_MODEL), F32), pltpu.SemaphoreType.DMA((2,))],
        ),
        out_shape=jax.ShapeDtypeStruct((n_rows, D_MODEL), F32),
        compiler_params=_cparams(("arbitrary",)),
        name="combine",
    )(dest_flat, ys, gates_t, tok, mod3)


def _routing(top_idx, rank, counts, n_rows):
    tm = EXPERT_TM
    n_blocks = -(-(TOP_K * n_rows) // tm) + N_EXPERTS
    counts = counts[:, 0]
    padded = (counts + tm - 1) // tm * tm
    padded_end = jnp.cumsum(padded)
    padded_start = padded_end - padded
    experts = jnp.arange(N_EXPERTS, dtype=jnp.int32)
    start_of = jnp.sum(jnp.where(top_idx[:, :, None] == experts, padded_start, 0), axis=-1)
    dest = (start_of + rank).astype(jnp.int32).reshape(TOP_K * n_rows)
    block_start = jnp.arange(n_blocks, dtype=jnp.int32) * tm
    block_expert = jnp.minimum(jnp.sum(padded_end[None, :] <= block_start[:, None], axis=1), N_EXPERTS - 1)
    n_active = padded_end[-1:] // tm
    group_end = jnp.sum(jnp.where(block_expert[:, None] == experts, padded_end // tm, 0), axis=1)
    follower = jnp.sum(jnp.where(group_end[:, None] == jnp.arange(n_blocks)[None, :], block_expert, 0), axis=1)
    next_expert = jnp.where(group_end < n_active, follower, -1)
    meta = jnp.concatenate([block_expert, n_active, next_expert]).astype(jnp.int32)
    n_pad = jnp.full((1,), n_blocks * tm, jnp.int32)
    pad_meta = jnp.concatenate([padded_start + counts, padded_end[-1:], padded_end, n_pad]).astype(jnp.int32)
    return dest, meta, pad_meta, n_blocks


def _rope_tables():
    rows = SEQ // GRID_W
    row = jnp.broadcast_to(jnp.arange(rows, dtype=F32)[:, None], (rows, GRID_W)).reshape(-1)
    col = jnp.broadcast_to(jnp.arange(GRID_W, dtype=F32)[None, :], (rows, GRID_W)).reshape(-1)
    inv_freq = ROPE_THETA ** (-jnp.arange(ROPE_FREQS, dtype=F32) / ROPE_FREQS)
    ang_r = row[:, None] * inv_freq
    ang_c = col[:, None] * inv_freq
    cos = jnp.concatenate([jnp.cos(ang_r), jnp.cos(ang_r), jnp.cos(ang_c), jnp.cos(ang_c)], axis=-1)
    sin = jnp.concatenate([-jnp.sin(ang_r), jnp.sin(ang_r), -jnp.sin(ang_c), jnp.sin(ang_c)], axis=-1)
    return cos, sin


def kernel(x, c, ctx, c_ctx, ada_w, ada_b, norm1_g, norm2_g, w_in, q_norm_g, k_norm_g, pool_w, pool_scale,
           sgu_norm_g, sgu_w, sgu_b, w_out, router_w, router_b, w_gu, b_gu, w_down, b_down):
    cos, sin = _rope_tables()
    tok = (x.reshape(N_LAT, D_MODEL), ctx.reshape(N_CTX, D_MODEL))
    cc = jnp.zeros((MOD_ROWS, D_MODEL), F32).at[:BATCH].set(c).at[BATCH].set(c_ctx)
    mod = _adaln(cc, ada_w, ada_b)
    tri = jnp.triu(jnp.ones((ROUTER_TM, ROUTER_TM), BF16), k=1)
    for l in range(DEPTH):
        last = l == DEPTH - 1
        n_rows = N_LAT if last else N_TOK
        mod3 = mod[l].reshape(MOD_ROWS, 1, 6 * D_MODEL)
        proj = _in_proj(tok, norm1_g[l], mod3, w_in[l].astype(BF16))
        attn = _latent_attention(proj, cos, sin, q_norm_g[l], k_norm_g[l])
        if not last:
            attn = (attn, _context_attention(proj, q_norm_g[l], k_norm_g[l]))
        sgu_b_full = jnp.broadcast_to(sgu_b[l][:, :, None], (N_SGU_HEADS, CHUNK, LANES))
        mix = _mixers(proj, pool_w[l].astype(BF16), pool_scale[l], sgu_norm_g[l], sgu_w[l].astype(BF16), sgu_b_full)
        tok = _out_proj(attn, mix, w_out[l].astype(BF16), tok, mod3, n_rows)
        rw_hi = router_w[l].astype(BF16)
        rw_lo = (router_w[l] - rw_hi.astype(F32)).astype(BF16)
        rw_pad = (jnp.zeros((D_MODEL, 2 * LANES), BF16).at[:, :N_EXPERTS].set(rw_hi)
                  .at[:, LANES:LANES + N_EXPERTS].set(rw_lo))
        rb_pad = jnp.zeros((1, LANES), F32).at[0, :N_EXPERTS].set(router_b[l])
        hp, top_idx, rank, gates_t, counts = _router(tok, norm2_g[l], mod3, rw_pad, rb_pad, tri, n_rows)
        dest, meta, pad_meta, n_blocks = _routing(top_idx, rank, counts, n_rows)
        n_pad = n_blocks * EXPERT_TM
        xs = _dispatch(dest, pad_meta, hp.reshape(n_rows, PACK_TILES, LANES), n_pad, n_rows)
        act = _expert_gu(meta, xs.reshape(n_pad * PACK_TILES, LANES), w_gu, b_gu, l, n_blocks)
        ys = _expert_down(meta, act, w_down, b_down, l, n_blocks)
        tok = _combine(dest, ys, gates_t, tok, mod3, n_rows)
    return tok.reshape(BATCH, SEQ, D_MODEL)
```

```python
import functools

import jax
import jax.numpy as jnp
from jax import lax
from jax.experimental import pallas as pl
from jax.experimental.pallas import tpu as pltpu

F32 = jnp.float32
BF16 = jnp.bfloat16
U32 = jnp.uint32

D_MODEL = 2048
BATCH = 4
SEQ = 4096
DEPTH = 2
GRID_W = 64
CTX_LEN = 256
HEAD_DIM = 128
ATTN_W = 1024
N_Q_HEADS = 8
GQA_GROUP = 4
N_KV_HEADS = 2
KV_W = 256
ROPE_THETA = 10000.0
ROPE_FREQS = 32
ATTN_SCALE = HEAD_DIM ** -0.5
LOG2_E = 1.4426950408889634
ATTN_TK = 512
POOL_WINDOWS = (2, 4, 8, 16)
POOL_W = 512
SGU_W = 512
N_SGU_HEADS = 4
CHUNK = 128
IN_W = 3072
N_EXPERTS = 32
TOP_K = 4
EXPERT_FF = 2048
SWIGLU_LIMIT = 7.0
SWIGLU_ALPHA = 1.702
EPS = 1e-6

N_LAT = BATCH * SEQ
N_CTX = BATCH * CTX_LEN
N_TOK = N_LAT + N_CTX
MOD_ROWS = 8
LANES = 128
POOL_HALO = 8
MIX_TM = 256
EXPERT_TM = 256
ROUTER_TM = 512
HALF_D = D_MODEL // 2
PACK_TILES = HALF_D // LANES
VMEM_LIMIT = 56 * 1024 * 1024


def _cparams(sem, vmem=VMEM_LIMIT):
    return pltpu.CompilerParams(dimension_semantics=sem, vmem_limit_bytes=vmem)


def _mod_row(row_tile, tm):
    return jnp.minimum(row_tile * tm // SEQ, BATCH)


def _rms(x, g):
    return x * lax.rsqrt(jnp.mean(x * x, axis=-1, keepdims=True) + EPS) * g


def _sigmoid(x):
    return 1.0 / (1.0 + jnp.exp(-x))


def _gelu(x):
    return 0.5 * x * (1.0 + lax.erf(x * 0.7071067811865476))


def _adaln_kernel(cc_ref, w_ref, b_ref, o_ref):
    cc = cc_ref[...]
    s = (cc * _sigmoid(cc)).astype(BF16)
    o_ref[...] = jnp.dot(s, w_ref[...].astype(BF16), preferred_element_type=F32) + b_ref[...]


def _adaln(cc, ada_w, ada_b):
    tn = 1024
    n = 6 * D_MODEL
    return pl.pallas_call(
        _adaln_kernel,
        grid=(DEPTH, n // tn),
        in_specs=[
            pl.BlockSpec((MOD_ROWS, D_MODEL), lambda l, j: (0, 0)),
            pl.BlockSpec((None, D_MODEL, tn), lambda l, j: (l, 0, j)),
            pl.BlockSpec((None, 1, tn), lambda l, j: (l, 0, j)),
        ],
        out_specs=pl.BlockSpec((None, MOD_ROWS, tn), lambda l, j: (l, 0, j)),
        out_shape=jax.ShapeDtypeStruct((DEPTH, MOD_ROWS, n), F32),
        compiler_params=_cparams(("arbitrary", "arbitrary")),
        name="adaln",
    )(cc, ada_w, ada_b.reshape(DEPTH, 1, n))


def _proj_kernel(*refs, n_first):
    if n_first is None:
        (xa_ref, g_ref, sh_ref, sc_ref, w_ref, o_ref, h_s), xb_ref = refs, None
    else:
        xa_ref, xb_ref, g_ref, sh_ref, sc_ref, w_ref, o_ref, h_s = refs
    i, j = pl.program_id(0), pl.program_id(1)

    def prepare(tile, slot):
        x = xa_ref[...]
        if xb_ref is not None:
            x = jnp.where(tile < n_first, x, xb_ref[...])
        y = _rms(x, g_ref[...])
        h_s[slot] = (y * (1.0 + sc_ref[...]) + sh_ref[...]).astype(BF16)

    @pl.when(jnp.logical_and(i == 0, j == 0))
    def _():
        prepare(0, 0)

    last = j == pl.num_programs(1) - 1

    @pl.when(last)
    def _():
        o_ref[...] = jnp.dot(h_s[i % 2], w_ref[...], preferred_element_type=F32)
        prepare(i + 1, (i + 1) % 2)

    @pl.when(jnp.logical_not(last))
    def _():
        o_ref[...] = jnp.dot(h_s[i % 2], w_ref[...], preferred_element_type=F32)


def _in_proj(src, norm_g, mod3, w_bf):
    tm, tn = 512, 1024
    n_tiles, nj = N_TOK // tm, IN_W // tn

    def tile(i, j):
        return jnp.minimum(i + (j + 1) // nj, n_tiles - 1)

    if isinstance(src, tuple):
        n_first = N_LAT // tm
        x_specs = [pl.BlockSpec((tm, D_MODEL), lambda i, j: (jnp.minimum(tile(i, j), n_first - 1), 0)),
                   pl.BlockSpec((tm, D_MODEL), lambda i, j: (jnp.maximum(tile(i, j) - n_first, 0), 0))]
    else:
        n_first, src = None, (src,)
        x_specs = [pl.BlockSpec((tm, D_MODEL), lambda i, j: (tile(i, j), 0))]
    return pl.pallas_call(
        functools.partial(_proj_kernel, n_first=n_first),
        grid=(n_tiles, nj),
        in_specs=x_specs + [
            pl.BlockSpec((1, D_MODEL), lambda i, j: (0, 0)),
            pl.BlockSpec((None, 1, D_MODEL), lambda i, j: (_mod_row(tile(i, j), tm), 0, 0)),
            pl.BlockSpec((None, 1, D_MODEL), lambda i, j: (_mod_row(tile(i, j), tm), 0, 1)),
            pl.BlockSpec((D_MODEL, tn), lambda i, j: (0, j)),
        ],
        out_specs=pl.BlockSpec((tm, tn), lambda i, j: (i, j)),
        out_shape=jax.ShapeDtypeStruct((N_TOK, IN_W), F32),
        scratch_shapes=[pltpu.VMEM((2, tm, D_MODEL), BF16)],
        compiler_params=_cparams(("arbitrary", "arbitrary")),
        name="in_proj",
    )(*src, norm_g.reshape(1, D_MODEL), mod3, mod3, w_bf)


def _rope(x, cos, sin):
    lane = lax.broadcasted_iota(jnp.int32, x.shape, 1)
    first = (lane % 64) < 32
    partner = jnp.where(first, pltpu.roll(x, 96, 1), pltpu.roll(x, 32, 1))
    return x * cos + partner * sin


def _softmax_pv(q, k, v):
    s = lax.dot_general(q, k, (((1,), (1,)), ((), ())), preferred_element_type=F32) * ATTN_SCALE
    m = jnp.max(s, axis=-1, keepdims=True)
    p = jnp.exp(s - m)
    l = jnp.sum(p, axis=-1, keepdims=True)
    return jnp.dot(p.astype(BF16), v, preferred_element_type=F32) / l


def _lat_attn_kernel(q_ref, kl_ref, vl_ref, kc_ref, vc_ref, cosq_ref, sinq_ref, cosk_ref, sink_ref,
                     qg_ref, kg_ref, o_ref, k_s, v_s, s_s):
    @pl.when(pl.program_id(2) == 0)
    def _():
        k_s[0:CTX_LEN, :] = _rms(kc_ref[...], kg_ref[...]).astype(BF16)
        kl = _rope(_rms(kl_ref[...], kg_ref[...]), cosk_ref[...], sink_ref[...])
        k_s[CTX_LEN:, :] = kl.astype(BF16)
        v_s[0:CTX_LEN, 0:HEAD_DIM] = vc_ref[...].astype(BF16)
        v_s[CTX_LEN:, 0:HEAD_DIM] = vl_ref[...].astype(BF16)
        v_s[:, HEAD_DIM:] = jnp.ones((CTX_LEN + SEQ, LANES), BF16)

    def scores(g):
        lanes = slice(g * HEAD_DIM, (g + 1) * HEAD_DIM)
        q = _rope(_rms(q_ref[:, lanes], qg_ref[...]), cosq_ref[...], sinq_ref[...])
        q = (q * (ATTN_SCALE * LOG2_E)).astype(BF16)
        s_s[g % 2] = lax.dot_general(q, k_s[...], (((1,), (1,)), ((), ())), preferred_element_type=F32)

    scores(0)
    for g in range(GQA_GROUP):
        if g + 1 < GQA_GROUP:
            scores(g + 1)
        s = s_s[g % 2]
        p = jnp.exp2(s - jnp.max(s, axis=-1, keepdims=True))
        ol = jnp.dot(p.astype(BF16), v_s[...], preferred_element_type=F32)
        o = ol[:, 0:HEAD_DIM] / ol[:, HEAD_DIM:]
        o_ref[:, g * HEAD_DIM:(g + 1) * HEAD_DIM] = o.astype(o_ref.dtype)


def _latent_attention(proj, cos, sin, q_g, k_g):
    tq = 256
    nq = SEQ // tq
    qw = GQA_GROUP * HEAD_DIM
    kcol = ATTN_W // HEAD_DIM
    vcol = (ATTN_W + KV_W) // HEAD_DIM
    ctx_blk = N_LAT // CTX_LEN
    return pl.pallas_call(
        _lat_attn_kernel,
        grid=(BATCH, N_KV_HEADS, nq),
        in_specs=[
            pl.BlockSpec((tq, qw), lambda b, h, i: (b * nq + i, h)),
            pl.BlockSpec((SEQ, HEAD_DIM), lambda b, h, i: (b, kcol + h)),
            pl.BlockSpec((SEQ, HEAD_DIM), lambda b, h, i: (b, vcol + h)),
            pl.BlockSpec((CTX_LEN, HEAD_DIM), lambda b, h, i: (ctx_blk + b, kcol + h)),
            pl.BlockSpec((CTX_LEN, HEAD_DIM), lambda b, h, i: (ctx_blk + b, vcol + h)),
            pl.BlockSpec((tq, HEAD_DIM), lambda b, h, i: (i, 0)),
            pl.BlockSpec((tq, HEAD_DIM), lambda b, h, i: (i, 0)),
            pl.BlockSpec((SEQ, HEAD_DIM), lambda b, h, i: (0, 0)),
            pl.BlockSpec((SEQ, HEAD_DIM), lambda b, h, i: (0, 0)),
            pl.BlockSpec((1, HEAD_DIM), lambda b, h, i: (0, 0)),
            pl.BlockSpec((1, HEAD_DIM), lambda b, h, i: (0, 0)),
        ],
        out_specs=pl.BlockSpec((tq, qw), lambda b, h, i: (b * nq + i, h)),
        out_shape=jax.ShapeDtypeStruct((N_LAT, ATTN_W), BF16),
        scratch_shapes=[pltpu.VMEM((CTX_LEN + SEQ, HEAD_DIM), BF16),
                        pltpu.VMEM((CTX_LEN + SEQ, HEAD_DIM + LANES), BF16),
                        pltpu.VMEM((2, tq, CTX_LEN + SEQ), F32)],
        compiler_params=_cparams(("arbitrary", "arbitrary", "arbitrary")),
        name="latent_attention",
    )(proj, proj, proj, proj, proj, cos, sin, cos, sin, q_g.reshape(1, HEAD_DIM), k_g.reshape(1, HEAD_DIM))


def _ctx_attn_kernel(q_ref, k_ref, v_ref, qg_ref, kg_ref, o_ref):
    k = _rms(k_ref[...], kg_ref[...]).astype(BF16)
    v = v_ref[...].astype(BF16)
    for g in range(GQA_GROUP):
        lanes = slice(g * HEAD_DIM, (g + 1) * HEAD_DIM)
        q = _rms(q_ref[:, lanes], qg_ref[...]).astype(BF16)
        o_ref[:, lanes] = _softmax_pv(q, k, v).astype(o_ref.dtype)


def _context_attention(proj, q_g, k_g):
    qw = GQA_GROUP * HEAD_DIM
    kcol = ATTN_W // HEAD_DIM
    vcol = (ATTN_W + KV_W) // HEAD_DIM
    ctx_blk = N_LAT // CTX_LEN
    return pl.pallas_call(
        _ctx_attn_kernel,
        grid=(BATCH, N_KV_HEADS),
        in_specs=[
            pl.BlockSpec((CTX_LEN, qw), lambda b, h: (ctx_blk + b, h)),
            pl.BlockSpec((CTX_LEN, HEAD_DIM), lambda b, h: (ctx_blk + b, kcol + h)),
            pl.BlockSpec((CTX_LEN, HEAD_DIM), lambda b, h: (ctx_blk + b, vcol + h)),
            pl.BlockSpec((1, HEAD_DIM), lambda b, h: (0, 0)),
            pl.BlockSpec((1, HEAD_DIM), lambda b, h: (0, 0)),
        ],
        out_specs=pl.BlockSpec((CTX_LEN, qw), lambda b, h: (b, h)),
        out_shape=jax.ShapeDtypeStruct((N_CTX, ATTN_W), BF16),
        compiler_params=_cparams(("arbitrary", "arbitrary")),
        name="context_attention",
    )(proj, proj, proj, q_g.reshape(1, HEAD_DIM), k_g.reshape(1, HEAD_DIM))


def _mixer_kernel(pin_ref, prev_ref, next_ref, su_ref, sv_ref, pw_ref, ps_ref, sg_ref, sw_ref, sb_ref,
                  o_ref, pad_ref):
    tm = MIX_TM
    i = pl.program_id(0)
    is_lat = i < N_LAT // tm
    pos0 = jnp.where(is_lat, (i % (SEQ // tm)) * tm, 0)
    seq_len = jnp.where(is_lat, SEQ, CTX_LEN)
    pad_ref[0:POOL_HALO, :] = jnp.where(pos0 == 0, 0.0, prev_ref[...])
    pad_ref[POOL_HALO:POOL_HALO + tm, :] = pin_ref[...]
    pad_ref[POOL_HALO + tm:, :] = jnp.where(pos0 + tm == seq_len, 0.0, next_ref[...])
    t = pos0 + lax.broadcasted_iota(jnp.int32, (tm, LANES), 0)
    for gi, w in enumerate(POOL_WINDOWS):
        lanes = slice(gi * LANES, (gi + 1) * LANES)
        acc = pad_ref[POOL_HALO - w // 2:POOL_HALO - w // 2 + tm, lanes]
        for d in range(-w // 2 + 1, w // 2):
            acc = acc + pad_ref[POOL_HALO + d:POOL_HALO + d + tm, lanes]
        cnt = (jnp.minimum(t + w // 2, seq_len) - jnp.maximum(t - w // 2, 0)).astype(F32)
        mixed = acc / cnt - pin_ref[:, lanes]
        y = jnp.dot(mixed.astype(BF16), pw_ref[gi], preferred_element_type=F32) * ps_ref[:, lanes]
        o_ref[:, lanes] = y.astype(o_ref.dtype)

    for h in range(N_SGU_HEADS):
        lanes = slice(h * LANES, (h + 1) * LANES)
        gu = _gelu(su_ref[:, lanes])
        vh = _rms(_gelu(sv_ref[:, lanes]), sg_ref[h:h + 1, :]).astype(BF16)
        for n in range(tm // CHUNK):
            rows = slice(n * CHUNK, (n + 1) * CHUNK)
            mixed = jnp.dot(sw_ref[h], vh[rows], preferred_element_type=F32) + sb_ref[h]
            o_ref[rows, POOL_W + h * LANES:POOL_W + (h + 1) * LANES] = (gu[rows] * mixed).astype(o_ref.dtype)


def _mixers(proj, pool_w_bf, pool_scale, sgu_norm_g, sgu_w_bf, sgu_b_full):
    tm = MIX_TM
    per_tile = tm // POOL_HALO
    last_halo = N_TOK // POOL_HALO - 1
    pcol = (ATTN_W + 2 * KV_W) // POOL_W
    return pl.pallas_call(
        _mixer_kernel,
        grid=(N_TOK // tm,),
        in_specs=[
            pl.BlockSpec((tm, POOL_W), lambda i: (i, pcol)),
            pl.BlockSpec((POOL_HALO, POOL_W), lambda i: (jnp.maximum(i * per_tile - 1, 0), pcol)),
            pl.BlockSpec((POOL_HALO, POOL_W), lambda i: (jnp.minimum((i + 1) * per_tile, last_halo), pcol)),
            pl.BlockSpec((tm, SGU_W), lambda i: (i, pcol + 1)),
            pl.BlockSpec((tm, SGU_W), lambda i: (i, pcol + 2)),
            pl.BlockSpec((len(POOL_WINDOWS), LANES, LANES), lambda i: (0, 0, 0)),
            pl.BlockSpec((1, POOL_W), lambda i: (0, 0)),
            pl.BlockSpec((N_SGU_HEADS, LANES), lambda i: (0, 0)),
            pl.BlockSpec((N_SGU_HEADS, CHUNK, CHUNK), lambda i: (0, 0, 0)),
            pl.BlockSpec((N_SGU_HEADS, CHUNK, LANES), lambda i: (0, 0, 0)),
        ],
        out_specs=pl.BlockSpec((tm, POOL_W + SGU_W), lambda i: (i, 0)),
        out_shape=jax.ShapeDtypeStruct((N_TOK, POOL_W + SGU_W), BF16),
        scratch_shapes=[pltpu.VMEM((tm + 2 * POOL_HALO, POOL_W), F32)],
        compiler_params=_cparams(("arbitrary",)),
        name="mixers",
    )(proj, proj, proj, proj, proj, pool_w_bf, pool_scale.reshape(1, POOL_W), sgu_norm_g, sgu_w_bf, sgu_b_full)


def _outproj_kernel(*refs, n_first):
    if n_first is None:
        a_ref, m_ref, w1_ref, w2_ref, x_ref, g_ref, o_ref = refs
        a, x = a_ref[...], x_ref[...]
    else:
        a_ref, ac_ref, m_ref, w1_ref, w2_ref, x_ref, xc_ref, g_ref, o_ref = refs
        is_first = pl.program_id(1) < n_first
        a = jnp.where(is_first, a_ref[...], ac_ref[...])
        x = jnp.where(is_first, x_ref[...], xc_ref[...])
    y = jnp.dot(a, w1_ref[...], preferred_element_type=F32)
    y = y + jnp.dot(m_ref[...], w2_ref[...], preferred_element_type=F32)
    o_ref[...] = x + g_ref[...] * y


def _out_proj(attn, mix, w_bf, src, mod3, n_rows):
    tm, tn = 512, 1024
    nj = D_MODEL // tn
    assert isinstance(attn, tuple) == isinstance(src, tuple)
    if isinstance(src, tuple):
        n_first = N_LAT // tm

        def first(i):
            return jnp.minimum(i, n_first - 1)

        def second(i):
            return jnp.maximum(i - n_first, 0)

        attn_specs = [pl.BlockSpec((tm, ATTN_W), lambda j, i: (first(i), 0)),
                      pl.BlockSpec((tm, ATTN_W), lambda j, i: (second(i), 0))]
        x_specs = [pl.BlockSpec((tm, tn), lambda j, i: (first(i), j)),
                   pl.BlockSpec((tm, tn), lambda j, i: (second(i), j))]
    else:
        n_first, attn, src = None, (attn,), (src,)
        attn_specs = [pl.BlockSpec((tm, ATTN_W), lambda j, i: (i, 0))]
        x_specs = [pl.BlockSpec((tm, tn), lambda j, i: (i, j))]
    return pl.pallas_call(
        functools.partial(_outproj_kernel, n_first=n_first),
        grid=(nj, n_rows // tm),
        in_specs=attn_specs + [
            pl.BlockSpec((tm, POOL_W + SGU_W), lambda j, i: (i, 0)),
            pl.BlockSpec((ATTN_W, tn), lambda j, i: (0, j)),
            pl.BlockSpec((POOL_W + SGU_W, tn), lambda j, i: (1, j)),
        ] + x_specs + [
            pl.BlockSpec((None, 1, tn), lambda j, i: (_mod_row(i, tm), 0, 2 * nj + j)),
        ],
        out_specs=pl.BlockSpec((tm, tn), lambda j, i: (i, j)),
        out_shape=jax.ShapeDtypeStruct((n_rows, D_MODEL), F32),
        compiler_params=_cparams(("arbitrary", "arbitrary")),
        name="out_proj",
    )(*attn, mix, w_bf, w_bf, *src, mod3)


def _router_kernel(x_ref, g_ref, sh_ref, sc_ref, rw_ref, rb_ref, tri_ref,
                   hp_ref, idx_ref, rank_ref, gate_ref, cnt_ref, run_s):
    @pl.when(pl.program_id(0) == 0)
    def _():
        run_s[...] = jnp.zeros_like(run_s)

    h = _rms(x_ref[...], g_ref[...]) * (1.0 + sc_ref[...]) + sh_ref[...]
    h_hi = h.astype(BF16)
    hb = h_hi.astype(F32)
    hi = lax.bitcast_convert_type(hb[:, :HALF_D], U32)
    lo = lax.bitcast_convert_type(hb[:, HALF_D:], U32)
    packed = hi | (lo >> 16)
    for j in range(PACK_TILES):
        hp_ref[pl.ds(j, packed.shape[0], stride=PACK_TILES), :] = packed[:, j * LANES:(j + 1) * LANES]

    h_lo = (h - hb).astype(BF16)
    both = jnp.dot(h_hi, rw_ref[...], preferred_element_type=F32)
    tail = jnp.dot(h_lo, rw_ref[:, 0:LANES], preferred_element_type=F32)
    logits = (both[:, 0:LANES] + both[:, LANES:]) + tail
    lt = (logits + rb_ref[...]).T[0:N_EXPERTS, :]
    expert = lax.broadcasted_iota(jnp.int32, lt.shape, 0).astype(F32)
    vals, idxs = [], []
    for _ in range(TOP_K):
        m = jnp.max(lt, axis=0, keepdims=True)
        idx = jnp.min(jnp.where(lt == m, expert, float(N_EXPERTS)), axis=0, keepdims=True)
        vals.append(m)
        idxs.append(idx)
        lt = jnp.where(expert == idx, -jnp.inf, lt)

    e = [jnp.exp(v - vals[0]) for v in vals]
    den = e[0] + e[1] + e[2] + e[3]
    slot = lax.broadcasted_iota(jnp.int32, (LANES, lt.shape[1]), 0)
    gates = jnp.zeros((LANES, lt.shape[1]), F32)
    for k in range(TOP_K):
        gates = jnp.where(slot == k, e[k] / den, gates)
    gate_ref[...] = gates.T

    base = run_s[...]
    for k in range(TOP_K):
        onehot = jnp.where(expert == idxs[k], 1.0, 0.0)
        before = jnp.dot(onehot.astype(BF16), tri_ref[...], preferred_element_type=F32)
        rank = jnp.sum(onehot * (before + base[:, 0:1]), axis=0, keepdims=True)
        idx_ref[k:k + 1, :] = idxs[k].astype(jnp.int32)
        rank_ref[k:k + 1, :] = rank.astype(jnp.int32)
        base = base + jnp.sum(onehot, axis=1, keepdims=True)
    run_s[...] = base
    cnt_ref[...] = base.astype(jnp.int32)


def _router(tok, norm_g, mod3, rw_pad, rb_pad, tri, n_rows):
    tm = ROUTER_TM
    return pl.pallas_call(
        _router_kernel,
        grid=(n_rows // tm,),
        in_specs=[
            pl.BlockSpec((tm, D_MODEL), lambda i: (i, 0)),
            pl.BlockSpec((1, D_MODEL), lambda i: (0, 0)),
            pl.BlockSpec((None, 1, D_MODEL), lambda i: (_mod_row(i, tm), 0, 3)),
            pl.BlockSpec((None, 1, D_MODEL), lambda i: (_mod_row(i, tm), 0, 4)),
            pl.BlockSpec((D_MODEL, 2 * LANES), lambda i: (0, 0)),
            pl.BlockSpec((1, LANES), lambda i: (0, 0)),
            pl.BlockSpec((tm, tm), lambda i: (0, 0)),
        ],
        out_specs=[
            pl.BlockSpec((tm * PACK_TILES, LANES), lambda i: (i, 0)),
            pl.BlockSpec((TOP_K, tm), lambda i: (0, i)),
            pl.BlockSpec((TOP_K, tm), lambda i: (0, i)),
            pl.BlockSpec((tm, LANES), lambda i: (i, 0)),
            pl.BlockSpec((N_EXPERTS, LANES), lambda i: (0, 0)),
        ],
        out_shape=[
            jax.ShapeDtypeStruct((n_rows * PACK_TILES, LANES), U32),
            jax.ShapeDtypeStruct((TOP_K, n_rows), jnp.int32),
            jax.ShapeDtypeStruct((TOP_K, n_rows), jnp.int32),
            jax.ShapeDtypeStruct((n_rows, LANES), F32),
            jax.ShapeDtypeStruct((N_EXPERTS, LANES), jnp.int32),
        ],
        scratch_shapes=[pltpu.VMEM((N_EXPERTS, LANES), F32)],
        compiler_params=_cparams(("arbitrary",)),
        name="router",
    )(tok, norm_g.reshape(1, D_MODEL), mod3, mod3, rw_pad, rb_pad, tri)


def _dispatch_kernel(dest_ref, pad_ref, hp_ref, xs_hbm, sem, pad_sem, *, rows, n_rows):
    i = pl.program_id(0)

    @pl.when(i == 0)
    def _():
        def per_expert(e, carry):
            lo, hi = pad_ref[e], pad_ref[N_EXPERTS + 1 + e]

            def start(p, c):
                pltpu.make_async_copy(hp_ref.at[0], xs_hbm.at[p], pad_sem).start()
                return c

            def wait(p, c):
                pltpu.make_async_copy(hp_ref.at[0], xs_hbm.at[p], pad_sem).wait()
                return c

            lax.fori_loop(lo, hi, start, 0)
            lax.fori_loop(lo, hi, wait, 0)
            return carry

        lax.fori_loop(0, N_EXPERTS + 1, per_expert, 0)

    base = i * rows

    def issue(r2, carry):
        for u in range(2):
            r = 2 * r2 + u
            for k in range(TOP_K):
                d = dest_ref[k * n_rows + base + r]
                pltpu.make_async_copy(hp_ref.at[r], xs_hbm.at[d], sem).start(priority=k % 2)
        return carry

    lax.fori_loop(0, rows // 2, issue, 0)
    for k in range(TOP_K):
        pltpu.make_async_copy(hp_ref, xs_hbm.at[pl.ds(0, rows)], sem).wait()


def _dispatch(dest_flat, pad_meta, hp, n_pad, n_rows):
    rows = 1024
    return pl.pallas_call(
        functools.partial(_dispatch_kernel, rows=rows, n_rows=n_rows),
        grid_spec=pltpu.PrefetchScalarGridSpec(
            num_scalar_prefetch=2,
            grid=(n_rows // rows,),
            in_specs=[pl.BlockSpec((rows, PACK_TILES, LANES), lambda i, d, p: (i, 0, 0))],
            out_specs=pl.BlockSpec(memory_space=pl.ANY),
            scratch_shapes=[pltpu.SemaphoreType.DMA(()), pltpu.SemaphoreType.DMA(())],
        ),
        out_shape=jax.ShapeDtypeStruct((n_pad, PACK_TILES, LANES), U32),
        compiler_params=_cparams(("arbitrary",)),
        name="dispatch",
    )(dest_flat, pad_meta, hp)


def _is_new_expert(meta_ref, i):
    prev = meta_ref[jnp.maximum(i - 1, 0)]
    return jnp.logical_or(i == 0, meta_ref[i] != prev)


def _cast_rows(src_ref, dst_ref, chunk=256):
    def body(c, carry):
        rows = pl.ds(pl.multiple_of(c * chunk, chunk), chunk)
        dst_ref[rows, :] = src_ref[rows, :].astype(dst_ref.dtype)
        return carry

    lax.fori_loop(0, src_ref.shape[0] // chunk, body, 0)


def _unpack_rows(x_ref):
    n = x_ref.shape[0] // PACK_TILES
    xp = jnp.concatenate([x_ref[pl.ds(j, n, stride=PACK_TILES), :] for j in range(PACK_TILES)], axis=1)
    hi = lax.bitcast_convert_type(xp & jnp.uint32(0xFFFF0000), F32).astype(BF16)
    lo = lax.bitcast_convert_type(xp << 16, F32).astype(BF16)
    return jnp.concatenate([hi, lo], axis=1)


def _expert_gu_kernel(meta_ref, x_ref, w_hbm, bg_ref, bu_ref, o_ref, wg_f, wu_f, wg_s, wu_s, sems,
                      *, n_blocks, layer):
    j, i = pl.program_id(0), pl.program_id(1)
    tn = wg_f.shape[1]
    nj = pl.num_programs(0)

    def fetch(e, jj):
        col = pl.multiple_of(jj * tn, tn)
        return (pltpu.make_async_copy(w_hbm.at[layer, e, :, pl.ds(col, tn)], wg_f, sems.at[0]),
                pltpu.make_async_copy(w_hbm.at[layer, e, :, pl.ds(EXPERT_FF + col, tn)], wu_f, sems.at[1]))

    @pl.when(jnp.logical_and(j == 0, i == 0))
    def _():
        for cp in fetch(meta_ref[0], 0):
            cp.start()

    active = i < meta_ref[n_blocks]

    @pl.when(jnp.logical_and(active, _is_new_expert(meta_ref, i)))
    def _():
        for cp in fetch(meta_ref[i], j):
            cp.wait()
        _cast_rows(wg_f, wg_s)
        _cast_rows(wu_f, wu_s)
        nxt = meta_ref[n_blocks + 1 + i]

        @pl.when(nxt >= 0)
        def _():
            for cp in fetch(nxt, j):
                cp.start()

        @pl.when(jnp.logical_and(nxt < 0, j + 1 < nj))
        def _():
            for cp in fetch(meta_ref[0], j + 1):
                cp.start()

    @pl.when(active)
    def _():
        x = _unpack_rows(x_ref)
        g = jnp.dot(x, wg_s[...], preferred_element_type=F32) + bg_ref[...]
        u = jnp.dot(x, wu_s[...], preferred_element_type=F32) + bu_ref[...]
        g = jnp.minimum(g, SWIGLU_LIMIT)
        u = jnp.clip(u, -SWIGLU_LIMIT, SWIGLU_LIMIT)
        o_ref[...] = ((u + 1.0) * (g * _sigmoid(g * SWIGLU_ALPHA))).astype(o_ref.dtype)

    @pl.when(jnp.logical_not(active))
    def _():
        o_ref[...] = jnp.zeros_like(o_ref)


def _active_block(i, m, n_blocks):
    return jnp.minimum(i, m[n_blocks] - 1)


def _expert_gu(meta, xs, w_gu, b_gu, layer, n_blocks):
    tm, tn = EXPERT_TM, 1024
    nj = EXPERT_FF // tn
    b3 = b_gu.reshape(DEPTH, N_EXPERTS, 1, 2 * EXPERT_FF)
    return pl.pallas_call(
        functools.partial(_expert_gu_kernel, n_blocks=n_blocks, layer=layer),
        grid_spec=pltpu.PrefetchScalarGridSpec(
            num_scalar_prefetch=1,
            grid=(nj, n_blocks),
            in_specs=[
                pl.BlockSpec((tm * PACK_TILES, LANES), lambda j, i, m: (_active_block(i, m, n_blocks), 0)),
                pl.BlockSpec(memory_space=pl.ANY),
                pl.BlockSpec((None, None, 1, tn), lambda j, i, m: (layer, m[i], 0, j)),
                pl.BlockSpec((None, None, 1, tn), lambda j, i, m: (layer, m[i], 0, nj + j)),
            ],
            out_specs=pl.BlockSpec((tm, tn), lambda j, i, m: (i, j)),
            scratch_shapes=[pltpu.VMEM((D_MODEL, tn), F32), pltpu.VMEM((D_MODEL, tn), F32),
                            pltpu.VMEM((D_MODEL, tn), BF16), pltpu.VMEM((D_MODEL, tn), BF16),
                            pltpu.SemaphoreType.DMA((2,))],
        ),
        out_shape=jax.ShapeDtypeStruct((n_blocks * tm, EXPERT_FF), BF16),
        compiler_params=_cparams(("arbitrary", "arbitrary")),
        name="expert_gu",
    )(meta, xs, w_gu, b3, b3)


def _expert_down_kernel(meta_ref, a_ref, w_hbm, b_ref, o_ref, w_f, w_s, sem, *, n_blocks, layer):
    i = pl.program_id(0)

    def fetch(e):
        return pltpu.make_async_copy(w_hbm.at[layer, e], w_f, sem)

    @pl.when(i == 0)
    def _():
        fetch(meta_ref[0]).start()

    active = i < meta_ref[n_blocks]

    @pl.when(jnp.logical_and(active, _is_new_expert(meta_ref, i)))
    def _():
        fetch(meta_ref[i]).wait()
        _cast_rows(w_f, w_s)
        nxt = meta_ref[n_blocks + 1 + i]

        @pl.when(nxt >= 0)
        def _():
            fetch(nxt).start()

    @pl.when(active)
    def _():
        o_ref[...] = jnp.dot(a_ref[...], w_s[...], preferred_element_type=F32) + b_ref[...]

    @pl.when(jnp.logical_not(active))
    def _():
        o_ref[...] = jnp.zeros_like(o_ref)


def _expert_down(meta, act, w_down, b_down, layer, n_blocks):
    tm = EXPERT_TM
    b3 = b_down.reshape(DEPTH, N_EXPERTS, 1, D_MODEL)
    return pl.pallas_call(
        functools.partial(_expert_down_kernel, n_blocks=n_blocks, layer=layer),
        grid_spec=pltpu.PrefetchScalarGridSpec(
            num_scalar_prefetch=1,
            grid=(n_blocks,),
            in_specs=[
                pl.BlockSpec((tm, EXPERT_FF), lambda i, m: (_active_block(i, m, n_blocks), 0)),
                pl.BlockSpec(memory_space=pl.ANY),
                pl.BlockSpec((None, None, 1, D_MODEL), lambda i, m: (layer, m[i], 0, 0)),
            ],
            out_specs=pl.BlockSpec((tm, D_MODEL), lambda i, m: (i, 0)),
            scratch_shapes=[pltpu.VMEM((EXPERT_FF, D_MODEL), F32), pltpu.VMEM((EXPERT_FF, D_MODEL), BF16),
                            pltpu.SemaphoreType.DMA(())],
        ),
        out_shape=jax.ShapeDtypeStruct((n_blocks * tm, D_MODEL), F32),
        compiler_params=_cparams(("arbitrary",)),
        name="expert_down",
    )(meta, act, w_down, b3)


def _combine_kernel(dest_ref, y_hbm, gt_ref, x_ref, g_ref, o_ref, buf, sems, *, rows, n_rows):
    i = pl.program_id(0)
    n_steps = pl.num_programs(0)

    def issue(step, slot):
        base = step * rows

        def body(r8, carry):
            r0 = pl.multiple_of(r8 * 8, 8)
            for u in range(8):
                for k in range(TOP_K):
                    d = dest_ref[k * n_rows + base + r0 + u]
                    pltpu.make_async_copy(y_hbm.at[pl.ds(d, 1), :],
                                          buf.at[slot, k, pl.ds(r0, 8), :].at[pl.ds(u, 1), :],
                                          sems.at[slot]).start(priority=k % 2)
            return carry

        lax.fori_loop(0, rows // 8, body, 0)

    @pl.when(i == 0)
    def _():
        issue(0, 0)

    @pl.when(i + 1 < n_steps)
    def _():
        issue(i + 1, (i + 1) % 2)

    slot = i % 2
    for k in range(TOP_K):
        pltpu.make_async_copy(y_hbm.at[pl.ds(0, rows), :], buf.at[slot, k], sems.at[slot]).wait()
    gt = gt_ref[...]
    moe = gt[:, 0:1] * buf[slot, 0]
    for k in range(1, TOP_K):
        moe = moe + gt[:, k:k + 1] * buf[slot, k]
    o_ref[...] = x_ref[...] + g_ref[...] * moe


def _combine(dest_flat, ys, gates_t, tok, mod3, n_rows):
    rows = 128
    return pl.pallas_call(
        functools.partial(_combine_kernel, rows=rows, n_rows=n_rows),
        grid_spec=pltpu.PrefetchScalarGridSpec(
            num_scalar_prefetch=1,
            grid=(n_rows // rows,),
            in_specs=[
                pl.BlockSpec(memory_space=pl.ANY),
                pl.BlockSpec((rows, LANES), lambda i, d: (i, 0)),
                pl.BlockSpec((rows, D_MODEL), lambda i, d: (i, 0)),
                pl.BlockSpec((None, 1, D_MODEL), lambda i, d: (_mod_row(i, rows), 0, 5)),
            ],
            out_specs=pl.BlockSpec((rows, D_MODEL), lambda i, d: (i, 0)),
            scratch_shapes=[pltpu.VMEM((2, TOP_K, rows, D_MODEL), F32), pltpu.SemaphoreType.DMA((2,))],
        ),
        out_shape=jax.ShapeDtypeStruct((n_rows, D_MODEL), F32),
        compiler_params=_cparams(("arbitrary",)),
        name="combine",
    )(dest_flat, ys, gates_t, tok, mod3)


def _routing(top_idx, rank, counts, n_rows):
    tm = EXPERT_TM
    n_blocks = -(-(TOP_K * n_rows) // tm) + N_EXPERTS
    counts = counts[:, 0]
    padded = (counts + tm - 1) // tm * tm
    padded_end = jnp.cumsum(padded)
    padded_start = padded_end - padded
    experts = jnp.arange(N_EXPERTS, dtype=jnp.int32)
    start_of = jnp.sum(jnp.where(top_idx[:, :, None] == experts, padded_start, 0), axis=-1)
    dest = (start_of + rank).astype(jnp.int32).reshape(TOP_K * n_rows)
    block_start = jnp.arange(n_blocks, dtype=jnp.int32) * tm
    block_expert = jnp.minimum(jnp.sum(padded_end[None, :] <= block_start[:, None], axis=1), N_EXPERTS - 1)
    n_active = padded_end[-1:] // tm
    group_end = jnp.sum(jnp.where(block_expert[:, None] == experts, padded_end // tm, 0), axis=1)
    follower = jnp.sum(jnp.where(group_end[:, None] == jnp.arange(n_blocks)[None, :], block_expert, 0), axis=1)
    next_expert = jnp.where(group_end < n_active, follower, -1)
    meta = jnp.concatenate([block_expert, n_active, next_expert]).astype(jnp.int32)
    n_pad = jnp.full((1,), n_blocks * tm, jnp.int32)
    pad_meta = jnp.concatenate([padded_start + counts, padded_end[-1:], padded_end, n_pad]).astype(jnp.int32)
    return dest, meta, pad_meta, n_blocks


def _rope_tables():
    rows = SEQ // GRID_W
    row = jnp.broadcast_to(jnp.arange(rows, dtype=F32)[:, None], (rows, GRID_W)).reshape(-1)
    col = jnp.broadcast_to(jnp.arange(GRID_W, dtype=F32)[None, :], (rows, GRID_W)).reshape(-1)
    inv_freq = ROPE_THETA ** (-jnp.arange(ROPE_FREQS, dtype=F32) / ROPE_FREQS)
    ang_r = row[:, None] * inv_freq
    ang_c = col[:, None] * inv_freq
    cos = jnp.concatenate([jnp.cos(ang_r), jnp.cos(ang_r), jnp.cos(ang_c), jnp.cos(ang_c)], axis=-1)
    sin = jnp.concatenate([-jnp.sin(ang_r), jnp.sin(ang_r), -jnp.sin(ang_c), jnp.sin(ang_c)], axis=-1)
    return cos, sin


def kernel(x, c, ctx, c_ctx, ada_w, ada_b, norm1_g, norm2_g, w_in, q_norm_g, k_norm_g, pool_w, pool_scale,
           sgu_norm_g, sgu_w, sgu_b, w_out, router_w, router_b, w_gu, b_gu, w_down, b_down):
    cos, sin = _rope_tables()
    tok = (x.reshape(N_LAT, D_MODEL), ctx.reshape(N_CTX, D_MODEL))
    cc = jnp.zeros((MOD_ROWS, D_MODEL), F32).at[:BATCH].set(c).at[BATCH].set(c_ctx)
    mod = _adaln(cc, ada_w, ada_b)
    tri = jnp.triu(jnp.ones((ROUTER_TM, ROUTER_TM), BF16), k=1)
    for l in range(DEPTH):
        last = l == DEPTH - 1
        n_rows = N_LAT if last else N_TOK
        mod3 = mod[l].reshape(MOD_ROWS, 1, 6 * D_MODEL)
        proj = _in_proj(tok, norm1_g[l], mod3, w_in[l].astype(BF16))
        attn = _latent_attention(proj, cos, sin, q_norm_g[l], k_norm_g[l])
        if not last:
            attn = (attn, _context_attention(proj, q_norm_g[l], k_norm_g[l]))
        sgu_b_full = jnp.broadcast_to(sgu_b[l][:, :, None], (N_SGU_HEADS, CHUNK, LANES))
        mix = _mixers(proj, pool_w[l].astype(BF16), pool_scale[l], sgu_norm_g[l], sgu_w[l].astype(BF16), sgu_b_full)
        tok = _out_proj(attn, mix, w_out[l].astype(BF16), tok, mod3, n_rows)
        rw_hi = router_w[l].astype(BF16)
        rw_lo = (router_w[l] - rw_hi.astype(F32)).astype(BF16)
        rw_pad = (jnp.zeros((D_MODEL, 2 * LANES), BF16).at[:, :N_EXPERTS].set(rw_hi)
                  .at[:, LANES:LANES + N_EXPERTS].set(rw_lo))
        rb_pad = jnp.zeros((1, LANES), F32).at[0, :N_EXPERTS].set(router_b[l])
        hp, top_idx, rank, gates_t, counts = _router(tok, norm2_g[l], mod3, rw_pad, rb_pad, tri, n_rows)
        dest, meta, pad_meta, n_blocks = _routing(top_idx, rank, counts, n_rows)
        n_pad = n_blocks * EXPERT_TM
        xs = _dispatch(dest, pad_meta, hp.reshape(n_rows, PACK_TILES, LANES), n_pad, n_rows)
        act = _expert_gu(meta, xs.reshape(n_pad * PACK_TILES, LANES), w_gu, b_gu, l, n_blocks)
        ys = _expert_down(meta, act, w_down, b_down, l, n_blocks)
        tok = _combine(dest, ys, gates_t, tok, mod3, n_rows)
    return tok.reshape(BATCH, SEQ, D_MODEL)
```

```python
import functools

import jax
import jax.numpy as jnp
from jax import lax
from jax.experimental import pallas as pl
from jax.experimental.pallas import tpu as pltpu

F32 = jnp.float32
BF16 = jnp.bfloat16
U32 = jnp.uint32

D_MODEL = 2048
BATCH = 4
SEQ = 4096
DEPTH = 2
GRID_W = 64
CTX_LEN = 256
HEAD_DIM = 128
ATTN_W = 1024
N_Q_HEADS = 8
GQA_GROUP = 4
N_KV_HEADS = 2
KV_W = 256
ROPE_THETA = 10000.0
ROPE_FREQS = 32
ATTN_SCALE = HEAD_DIM ** -0.5
LOG2_E = 1.4426950408889634
ONES_ROWS = 16
POOL_WINDOWS = (2, 4, 8, 16)
POOL_W = 512
SGU_W = 512
N_SGU_HEADS = 4
CHUNK = 128
IN_W = 3072
N_EXPERTS = 32
TOP_K = 4
EXPERT_FF = 2048
SWIGLU_LIMIT = 7.0
SWIGLU_ALPHA = 1.702
EPS = 1e-6

N_LAT = BATCH * SEQ
N_CTX = BATCH * CTX_LEN
N_TOK = N_LAT + N_CTX
MOD_ROWS = 8
LANES = 128
POOL_HALO = 8
MIX_TM = 256
EXPERT_TM = 256
ROUTER_TM = 512
HALF_D = D_MODEL // 2
PACK_TILES = HALF_D // LANES
VMEM_LIMIT = 56 * 1024 * 1024


def _cparams(sem, vmem=VMEM_LIMIT):
    return pltpu.CompilerParams(dimension_semantics=sem, vmem_limit_bytes=vmem)


def _mod_row(row_tile, tm):
    return jnp.minimum(row_tile * tm // SEQ, BATCH)


def _rms(x, g):
    return x * lax.rsqrt(jnp.mean(x * x, axis=-1, keepdims=True) + EPS) * g


def _sigmoid(x):
    return 1.0 / (1.0 + jnp.exp(-x))


def _gelu(x):
    return 0.5 * x * (1.0 + lax.erf(x * 0.7071067811865476))


def _adaln_kernel(cc_ref, w_ref, b_ref, o_ref):
    cc = cc_ref[...]
    s = (cc * _sigmoid(cc)).astype(BF16)
    o_ref[...] = jnp.dot(s, w_ref[...].astype(BF16), preferred_element_type=F32) + b_ref[...]


def _adaln(cc, ada_w, ada_b):
    tn = 1024
    n = 6 * D_MODEL
    return pl.pallas_call(
        _adaln_kernel,
        grid=(DEPTH, n // tn),
        in_specs=[
            pl.BlockSpec((MOD_ROWS, D_MODEL), lambda l, j: (0, 0)),
            pl.BlockSpec((None, D_MODEL, tn), lambda l, j: (l, 0, j)),
            pl.BlockSpec((None, 1, tn), lambda l, j: (l, 0, j)),
        ],
        out_specs=pl.BlockSpec((None, MOD_ROWS, tn), lambda l, j: (l, 0, j)),
        out_shape=jax.ShapeDtypeStruct((DEPTH, MOD_ROWS, n), F32),
        compiler_params=_cparams(("arbitrary", "arbitrary")),
        name="adaln",
    )(cc, ada_w, ada_b.reshape(DEPTH, 1, n))


def _proj_kernel(*refs, n_first):
    if n_first is None:
        (xa_ref, g_ref, sh_ref, sc_ref, w_ref, o_ref, h_s), xb_ref = refs, None
    else:
        xa_ref, xb_ref, g_ref, sh_ref, sc_ref, w_ref, o_ref, h_s = refs
    i, j = pl.program_id(0), pl.program_id(1)

    def prepare(tile, slot):
        x = xa_ref[...]
        if xb_ref is not None:
            x = jnp.where(tile < n_first, x, xb_ref[...])
        y = _rms(x, g_ref[...])
        h_s[slot] = (y * (1.0 + sc_ref[...]) + sh_ref[...]).astype(BF16)

    @pl.when(jnp.logical_and(i == 0, j == 0))
    def _():
        prepare(0, 0)

    last = j == pl.num_programs(1) - 1

    @pl.when(last)
    def _():
        o_ref[...] = jnp.dot(h_s[i % 2], w_ref[...], preferred_element_type=F32)
        prepare(i + 1, (i + 1) % 2)

    @pl.when(jnp.logical_not(last))
    def _():
        o_ref[...] = jnp.dot(h_s[i % 2], w_ref[...], preferred_element_type=F32)


def _in_proj(src, norm_g, mod3, w_bf):
    tm, tn = 512, 1024
    n_tiles, nj = N_TOK // tm, IN_W // tn

    def tile(i, j):
        return jnp.minimum(i + (j + 1) // nj, n_tiles - 1)

    if isinstance(src, tuple):
        n_first = N_LAT // tm
        x_specs = [pl.BlockSpec((tm, D_MODEL), lambda i, j: (jnp.minimum(tile(i, j), n_first - 1), 0)),
                   pl.BlockSpec((tm, D_MODEL), lambda i, j: (jnp.maximum(tile(i, j) - n_first, 0), 0))]
    else:
        n_first, src = None, (src,)
        x_specs = [pl.BlockSpec((tm, D_MODEL), lambda i, j: (tile(i, j), 0))]
    return pl.pallas_call(
        functools.partial(_proj_kernel, n_first=n_first),
        grid=(n_tiles, nj),
        in_specs=x_specs + [
            pl.BlockSpec((1, D_MODEL), lambda i, j: (0, 0)),
            pl.BlockSpec((None, 1, D_MODEL), lambda i, j: (_mod_row(tile(i, j), tm), 0, 0)),
            pl.BlockSpec((None, 1, D_MODEL), lambda i, j: (_mod_row(tile(i, j), tm), 0, 1)),
            pl.BlockSpec((D_MODEL, tn), lambda i, j: (0, j)),
        ],
        out_specs=pl.BlockSpec((tm, tn), lambda i, j: (i, j)),
        out_shape=jax.ShapeDtypeStruct((N_TOK, IN_W), F32),
        scratch_shapes=[pltpu.VMEM((2, tm, D_MODEL), BF16)],
        compiler_params=_cparams(("arbitrary", "arbitrary")),
        name="in_proj",
    )(*src, norm_g.reshape(1, D_MODEL), mod3, mod3, w_bf)


def _rope(x, cos, sin):
    lane = lax.broadcasted_iota(jnp.int32, x.shape, 1)
    first = (lane % 64) < 32
    partner = jnp.where(first, pltpu.roll(x, 96, 1), pltpu.roll(x, 32, 1))
    return x * cos + partner * sin


def _softmax_pv(q, k, v):
    s = lax.dot_general(q, k, (((1,), (1,)), ((), ())), preferred_element_type=F32) * ATTN_SCALE
    m = jnp.max(s, axis=-1, keepdims=True)
    p = jnp.exp(s - m)
    l = jnp.sum(p, axis=-1, keepdims=True)
    return jnp.dot(p.astype(BF16), v, preferred_element_type=F32) / l


def _lat_attn_kernel(q_ref, kl_ref, vl_ref, kc_ref, vc_ref, cosq_ref, sinq_ref, cosk_ref, sink_ref,
                     qg_ref, kg_ref, o_ref, k_s, vt_s, s_s):
    @pl.when(pl.program_id(2) == 0)
    def _():
        k_s[0:CTX_LEN, :] = _rms(kc_ref[...], kg_ref[...]).astype(BF16)
        kl = _rope(_rms(kl_ref[...], kg_ref[...]), cosk_ref[...], sink_ref[...])
        k_s[CTX_LEN:, :] = kl.astype(BF16)
        vt_s[0:HEAD_DIM, 0:CTX_LEN] = vc_ref[...].T.astype(BF16)
        vt_s[0:HEAD_DIM, CTX_LEN:] = vl_ref[...].T.astype(BF16)
        vt_s[HEAD_DIM:, :] = jnp.ones((ONES_ROWS, CTX_LEN + SEQ), BF16)

    def scores(g):
        lanes = slice(g * HEAD_DIM, (g + 1) * HEAD_DIM)
        q = _rope(_rms(q_ref[:, lanes], qg_ref[...]), cosq_ref[...], sinq_ref[...])
        qt = (q * (ATTN_SCALE * LOG2_E)).T.astype(BF16)
        s_s[g % 2] = jnp.dot(k_s[...], qt, preferred_element_type=F32)

    scores(0)
    for g in range(GQA_GROUP):
        if g + 1 < GQA_GROUP:
            scores(g + 1)
        m = jnp.max(s_s[g % 2], axis=0, keepdims=True)
        p = jnp.exp2(s_s[g % 2] - m).astype(BF16)
        ol = jnp.dot(vt_s[...], p, preferred_element_type=F32)
        o_t = ol[0:HEAD_DIM, :] / ol[HEAD_DIM:HEAD_DIM + 1, :]
        o_ref[:, g * HEAD_DIM:(g + 1) * HEAD_DIM] = o_t.T.astype(o_ref.dtype)


def _latent_attention(proj, cos, sin, q_g, k_g):
    tq = 512
    nq = SEQ // tq
    qw = GQA_GROUP * HEAD_DIM
    kcol = ATTN_W // HEAD_DIM
    vcol = (ATTN_W + KV_W) // HEAD_DIM
    ctx_blk = N_LAT // CTX_LEN
    return pl.pallas_call(
        _lat_attn_kernel,
        grid=(BATCH, N_KV_HEADS, nq),
        in_specs=[
            pl.BlockSpec((tq, qw), lambda b, h, i: (b * nq + i, h)),
            pl.BlockSpec((SEQ, HEAD_DIM), lambda b, h, i: (b, kcol + h)),
            pl.BlockSpec((SEQ, HEAD_DIM), lambda b, h, i: (b, vcol + h)),
            pl.BlockSpec((CTX_LEN, HEAD_DIM), lambda b, h, i: (ctx_blk + b, kcol + h)),
            pl.BlockSpec((CTX_LEN, HEAD_DIM), lambda b, h, i: (ctx_blk + b, vcol + h)),
            pl.BlockSpec((tq, HEAD_DIM), lambda b, h, i: (i, 0)),
            pl.BlockSpec((tq, HEAD_DIM), lambda b, h, i: (i, 0)),
            pl.BlockSpec((SEQ, HEAD_DIM), lambda b, h, i: (0, 0)),
            pl.BlockSpec((SEQ, HEAD_DIM), lambda b, h, i: (0, 0)),
            pl.BlockSpec((1, HEAD_DIM), lambda b, h, i: (0, 0)),
            pl.BlockSpec((1, HEAD_DIM), lambda b, h, i: (0, 0)),
        ],
        out_specs=pl.BlockSpec((tq, qw), lambda b, h, i: (b * nq + i, h)),
        out_shape=jax.ShapeDtypeStruct((N_LAT, ATTN_W), BF16),
        scratch_shapes=[pltpu.VMEM((CTX_LEN + SEQ, HEAD_DIM), BF16),
                        pltpu.VMEM((HEAD_DIM + ONES_ROWS, CTX_LEN + SEQ), BF16),
                        pltpu.VMEM((2, CTX_LEN + SEQ, tq), F32)],
        compiler_params=_cparams(("arbitrary", "arbitrary", "arbitrary")),
        name="latent_attention",
    )(proj, proj, proj, proj, proj, cos, sin, cos, sin, q_g.reshape(1, HEAD_DIM), k_g.reshape(1, HEAD_DIM))


def _ctx_attn_kernel(q_ref, k_ref, v_ref, qg_ref, kg_ref, o_ref):
    k = _rms(k_ref[...], kg_ref[...]).astype(BF16)
    v = v_ref[...].astype(BF16)
    for g in range(GQA_GROUP):
        lanes = slice(g * HEAD_DIM, (g + 1) * HEAD_DIM)
        q = _rms(q_ref[:, lanes], qg_ref[...]).astype(BF16)
        o_ref[:, lanes] = _softmax_pv(q, k, v).astype(o_ref.dtype)


def _context_attention(proj, q_g, k_g):
    qw = GQA_GROUP * HEAD_DIM
    kcol = ATTN_W // HEAD_DIM
    vcol = (ATTN_W + KV_W) // HEAD_DIM
    ctx_blk = N_LAT // CTX_LEN
    return pl.pallas_call(
        _ctx_attn_kernel,
        grid=(BATCH, N_KV_HEADS),
        in_specs=[
            pl.BlockSpec((CTX_LEN, qw), lambda b, h: (ctx_blk + b, h)),
            pl.BlockSpec((CTX_LEN, HEAD_DIM), lambda b, h: (ctx_blk + b, kcol + h)),
            pl.BlockSpec((CTX_LEN, HEAD_DIM), lambda b, h: (ctx_blk + b, vcol + h)),
            pl.BlockSpec((1, HEAD_DIM), lambda b, h: (0, 0)),
            pl.BlockSpec((1, HEAD_DIM), lambda b, h: (0, 0)),
        ],
        out_specs=pl.BlockSpec((CTX_LEN, qw), lambda b, h: (b, h)),
        out_shape=jax.ShapeDtypeStruct((N_CTX, ATTN_W), BF16),
        compiler_params=_cparams(("arbitrary", "arbitrary")),
        name="context_attention",
    )(proj, proj, proj, q_g.reshape(1, HEAD_DIM), k_g.reshape(1, HEAD_DIM))


def _mixer_kernel(pin_ref, prev_ref, next_ref, su_ref, sv_ref, pw_ref, ps_ref, sg_ref, sw_ref, sb_ref,
                  o_ref, pad_ref):
    tm = MIX_TM
    i = pl.program_id(0)
    is_lat = i < N_LAT // tm
    pos0 = jnp.where(is_lat, (i % (SEQ // tm)) * tm, 0)
    seq_len = jnp.where(is_lat, SEQ, CTX_LEN)
    pad_ref[0:POOL_HALO, :] = jnp.where(pos0 == 0, 0.0, prev_ref[...])
    pad_ref[POOL_HALO:POOL_HALO + tm, :] = pin_ref[...]
    pad_ref[POOL_HALO + tm:, :] = jnp.where(pos0 + tm == seq_len, 0.0, next_ref[...])
    t = pos0 + lax.broadcasted_iota(jnp.int32, (tm, LANES), 0)
    for gi, w in enumerate(POOL_WINDOWS):
        lanes = slice(gi * LANES, (gi + 1) * LANES)
        acc = pad_ref[POOL_HALO - w // 2:POOL_HALO - w // 2 + tm, lanes]
        for d in range(-w // 2 + 1, w // 2):
            acc = acc + pad_ref[POOL_HALO + d:POOL_HALO + d + tm, lanes]
        cnt = (jnp.minimum(t + w // 2, seq_len) - jnp.maximum(t - w // 2, 0)).astype(F32)
        mixed = acc / cnt - pin_ref[:, lanes]
        y = jnp.dot(mixed.astype(BF16), pw_ref[gi], preferred_element_type=F32) * ps_ref[:, lanes]
        o_ref[:, lanes] = y.astype(o_ref.dtype)

    for h in range(N_SGU_HEADS):
        lanes = slice(h * LANES, (h + 1) * LANES)
        gu = _gelu(su_ref[:, lanes])
        vh = _rms(_gelu(sv_ref[:, lanes]), sg_ref[h:h + 1, :]).astype(BF16)
        for n in range(tm // CHUNK):
            rows = slice(n * CHUNK, (n + 1) * CHUNK)
            mixed = jnp.dot(sw_ref[h], vh[rows], preferred_element_type=F32) + sb_ref[h]
            o_ref[rows, POOL_W + h * LANES:POOL_W + (h + 1) * LANES] = (gu[rows] * mixed).astype(o_ref.dtype)


def _mixers(proj, pool_w_bf, pool_scale, sgu_norm_g, sgu_w_bf, sgu_b_full):
    tm = MIX_TM
    per_tile = tm // POOL_HALO
    last_halo = N_TOK // POOL_HALO - 1
    pcol = (ATTN_W + 2 * KV_W) // POOL_W
    return pl.pallas_call(
        _mixer_kernel,
        grid=(N_TOK // tm,),
        in_specs=[
            pl.BlockSpec((tm, POOL_W), lambda i: (i, pcol)),
            pl.BlockSpec((POOL_HALO, POOL_W), lambda i: (jnp.maximum(i * per_tile - 1, 0), pcol)),
            pl.BlockSpec((POOL_HALO, POOL_W), lambda i: (jnp.minimum((i + 1) * per_tile, last_halo), pcol)),
            pl.BlockSpec((tm, SGU_W), lambda i: (i, pcol + 1)),
            pl.BlockSpec((tm, SGU_W), lambda i: (i, pcol + 2)),
            pl.BlockSpec((len(POOL_WINDOWS), LANES, LANES), lambda i: (0, 0, 0)),
            pl.BlockSpec((1, POOL_W), lambda i: (0, 0)),
            pl.BlockSpec((N_SGU_HEADS, LANES), lambda i: (0, 0)),
            pl.BlockSpec((N_SGU_HEADS, CHUNK, CHUNK), lambda i: (0, 0, 0)),
            pl.BlockSpec((N_SGU_HEADS, CHUNK, LANES), lambda i: (0, 0, 0)),
        ],
        out_specs=pl.BlockSpec((tm, POOL_W + SGU_W), lambda i: (i, 0)),
        out_shape=jax.ShapeDtypeStruct((N_TOK, POOL_W + SGU_W), BF16),
        scratch_shapes=[pltpu.VMEM((tm + 2 * POOL_HALO, POOL_W), F32)],
        compiler_params=_cparams(("arbitrary",)),
        name="mixers",
    )(proj, proj, proj, proj, proj, pool_w_bf, pool_scale.reshape(1, POOL_W), sgu_norm_g, sgu_w_bf, sgu_b_full)


def _outproj_kernel(*refs, n_first):
    if n_first is None:
        a_ref, m_ref, w1_ref, w2_ref, x_ref, g_ref, o_ref = refs
        a, x = a_ref[...], x_ref[...]
    else:
        a_ref, ac_ref, m_ref, w1_ref, w2_ref, x_ref, xc_ref, g_ref, o_ref = refs
        is_first = pl.program_id(1) < n_first
        a = jnp.where(is_first, a_ref[...], ac_ref[...])
        x = jnp.where(is_first, x_ref[...], xc_ref[...])
    y = jnp.dot(a, w1_ref[...], preferred_element_type=F32)
    y = y + jnp.dot(m_ref[...], w2_ref[...], preferred_element_type=F32)
    o_ref[...] = x + g_ref[...] * y


def _out_proj(attn, mix, w_bf, src, mod3, n_rows):
    tm, tn = 512, 1024
    nj = D_MODEL // tn
    assert isinstance(attn, tuple) == isinstance(src, tuple)
    if isinstance(src, tuple):
        n_first = N_LAT // tm

        def first(i):
            return jnp.minimum(i, n_first - 1)

        def second(i):
            return jnp.maximum(i - n_first, 0)

        attn_specs = [pl.BlockSpec((tm, ATTN_W), lambda j, i: (first(i), 0)),
                      pl.BlockSpec((tm, ATTN_W), lambda j, i: (second(i), 0))]
        x_specs = [pl.BlockSpec((tm, tn), lambda j, i: (first(i), j)),
                   pl.BlockSpec((tm, tn), lambda j, i: (second(i), j))]
    else:
        n_first, attn, src = None, (attn,), (src,)
        attn_specs = [pl.BlockSpec((tm, ATTN_W), lambda j, i: (i, 0))]
        x_specs = [pl.BlockSpec((tm, tn), lambda j, i: (i, j))]
    return pl.pallas_call(
        functools.partial(_outproj_kernel, n_first=n_first),
        grid=(nj, n_rows // tm),
        in_specs=attn_specs + [
            pl.BlockSpec((tm, POOL_W + SGU_W), lambda j, i: (i, 0)),
            pl.BlockSpec((ATTN_W, tn), lambda j, i: (0, j)),
            pl.BlockSpec((POOL_W + SGU_W, tn), lambda j, i: (1, j)),
        ] + x_specs + [
            pl.BlockSpec((None, 1, tn), lambda j, i: (_mod_row(i, tm), 0, 2 * nj + j)),
        ],
        out_specs=pl.BlockSpec((tm, tn), lambda j, i: (i, j)),
        out_shape=jax.ShapeDtypeStruct((n_rows, D_MODEL), F32),
        compiler_params=_cparams(("arbitrary", "arbitrary")),
        name="out_proj",
    )(*attn, mix, w_bf, w_bf, *src, mod3)


def _router_kernel(x_ref, g_ref, sh_ref, sc_ref, rw_ref, rb_ref, tri_ref,
                   hp_ref, idx_ref, rank_ref, gate_ref, cnt_ref, run_s):
    @pl.when(pl.program_id(0) == 0)
    def _():
        run_s[...] = jnp.zeros_like(run_s)

    h = _rms(x_ref[...], g_ref[...]) * (1.0 + sc_ref[...]) + sh_ref[...]
    h_hi = h.astype(BF16)
    hb = h_hi.astype(F32)
    hi = lax.bitcast_convert_type(hb[:, :HALF_D], U32)
    lo = lax.bitcast_convert_type(hb[:, HALF_D:], U32)
    packed = hi | (lo >> 16)
    for j in range(PACK_TILES):
        hp_ref[pl.ds(j, packed.shape[0], stride=PACK_TILES), :] = packed[:, j * LANES:(j + 1) * LANES]

    h_lo = (h - hb).astype(BF16)
    both = jnp.dot(h_hi, rw_ref[...], preferred_element_type=F32)
    tail = jnp.dot(h_lo, rw_ref[:, 0:LANES], preferred_element_type=F32)
    logits = (both[:, 0:LANES] + both[:, LANES:]) + tail
    lt = (logits + rb_ref[...]).T[0:N_EXPERTS, :]
    expert = lax.broadcasted_iota(jnp.int32, lt.shape, 0).astype(F32)
    vals, idxs = [], []
    for _ in range(TOP_K):
        m = jnp.max(lt, axis=0, keepdims=True)
        idx = jnp.min(jnp.where(lt == m, expert, float(N_EXPERTS)), axis=0, keepdims=True)
        vals.append(m)
        idxs.append(idx)
        lt = jnp.where(expert == idx, -jnp.inf, lt)

    e = [jnp.exp(v - vals[0]) for v in vals]
    den = e[0] + e[1] + e[2] + e[3]
    slot = lax.broadcasted_iota(jnp.int32, (LANES, lt.shape[1]), 0)
    gates = jnp.zeros((LANES, lt.shape[1]), F32)
    for k in range(TOP_K):
        gates = jnp.where(slot == k, e[k] / den, gates)
    gate_ref[...] = gates.T

    base = run_s[...]
    for k in range(TOP_K):
        onehot = jnp.where(expert == idxs[k], 1.0, 0.0)
        before = jnp.dot(onehot.astype(BF16), tri_ref[...], preferred_element_type=F32)
        rank = jnp.sum(onehot * (before + base[:, 0:1]), axis=0, keepdims=True)
        idx_ref[k:k + 1, :] = idxs[k].astype(jnp.int32)
        rank_ref[k:k + 1, :] = rank.astype(jnp.int32)
        base = base + jnp.sum(onehot, axis=1, keepdims=True)
    run_s[...] = base
    cnt_ref[...] = base.astype(jnp.int32)


def _router(tok, norm_g, mod3, rw_pad, rb_pad, tri, n_rows):
    tm = ROUTER_TM
    return pl.pallas_call(
        _router_kernel,
        grid=(n_rows // tm,),
        in_specs=[
            pl.BlockSpec((tm, D_MODEL), lambda i: (i, 0)),
            pl.BlockSpec((1, D_MODEL), lambda i: (0, 0)),
            pl.BlockSpec((None, 1, D_MODEL), lambda i: (_mod_row(i, tm), 0, 3)),
            pl.BlockSpec((None, 1, D_MODEL), lambda i: (_mod_row(i, tm), 0, 4)),
            pl.BlockSpec((D_MODEL, 2 * LANES), lambda i: (0, 0)),
            pl.BlockSpec((1, LANES), lambda i: (0, 0)),
            pl.BlockSpec((tm, tm), lambda i: (0, 0)),
        ],
        out_specs=[
            pl.BlockSpec((tm * PACK_TILES, LANES), lambda i: (i, 0)),
            pl.BlockSpec((TOP_K, tm), lambda i: (0, i)),
            pl.BlockSpec((TOP_K, tm), lambda i: (0, i)),
            pl.BlockSpec((tm, LANES), lambda i: (i, 0)),
            pl.BlockSpec((N_EXPERTS, LANES), lambda i: (0, 0)),
        ],
        out_shape=[
            jax.ShapeDtypeStruct((n_rows * PACK_TILES, LANES), U32),
            jax.ShapeDtypeStruct((TOP_K, n_rows), jnp.int32),
            jax.ShapeDtypeStruct((TOP_K, n_rows), jnp.int32),
            jax.ShapeDtypeStruct((n_rows, LANES), F32),
            jax.ShapeDtypeStruct((N_EXPERTS, LANES), jnp.int32),
        ],
        scratch_shapes=[pltpu.VMEM((N_EXPERTS, LANES), F32)],
        compiler_params=_cparams(("arbitrary",)),
        name="router",
    )(tok, norm_g.reshape(1, D_MODEL), mod3, mod3, rw_pad, rb_pad, tri)


def _dispatch_kernel(dest_ref, pad_ref, hp_ref, xs_hbm, sem, pad_sem, *, rows, n_rows):
    i = pl.program_id(0)

    @pl.when(i == 0)
    def _():
        def per_expert(e, carry):
            lo, hi = pad_ref[e], pad_ref[N_EXPERTS + 1 + e]

            def start(p, c):
                pltpu.make_async_copy(hp_ref.at[0], xs_hbm.at[p], pad_sem).start()
                return c

            def wait(p, c):
                pltpu.make_async_copy(hp_ref.at[0], xs_hbm.at[p], pad_sem).wait()
                return c

            lax.fori_loop(lo, hi, start, 0)
            lax.fori_loop(lo, hi, wait, 0)
            return carry

        lax.fori_loop(0, N_EXPERTS + 1, per_expert, 0)

    base = i * rows

    def issue(r2, carry):
        for u in range(2):
            r = 2 * r2 + u
            for k in range(TOP_K):
                d = dest_ref[k * n_rows + base + r]
                pltpu.make_async_copy(hp_ref.at[r], xs_hbm.at[d], sem).start(priority=k % 2)
        return carry

    lax.fori_loop(0, rows // 2, issue, 0)
    for k in range(TOP_K):
        pltpu.make_async_copy(hp_ref, xs_hbm.at[pl.ds(0, rows)], sem).wait()


def _dispatch(dest_flat, pad_meta, hp, n_pad, n_rows):
    rows = 1024
    return pl.pallas_call(
        functools.partial(_dispatch_kernel, rows=rows, n_rows=n_rows),
        grid_spec=pltpu.PrefetchScalarGridSpec(
            num_scalar_prefetch=2,
            grid=(n_rows // rows,),
            in_specs=[pl.BlockSpec((rows, PACK_TILES, LANES), lambda i, d, p: (i, 0, 0))],
            out_specs=pl.BlockSpec(memory_space=pl.ANY),
            scratch_shapes=[pltpu.SemaphoreType.DMA(()), pltpu.SemaphoreType.DMA(())],
        ),
        out_shape=jax.ShapeDtypeStruct((n_pad, PACK_TILES, LANES), U32),
        compiler_params=_cparams(("arbitrary",)),
        name="dispatch",
    )(dest_flat, pad_meta, hp)


def _is_new_expert(meta_ref, i):
    prev = meta_ref[jnp.maximum(i - 1, 0)]
    return jnp.logical_or(i == 0, meta_ref[i] != prev)


def _cast_rows(src_ref, dst_ref, chunk=256):
    def body(c, carry):
        rows = pl.ds(pl.multiple_of(c * chunk, chunk), chunk)
        dst_ref[rows, :] = src_ref[rows, :].astype(dst_ref.dtype)
        return carry

    lax.fori_loop(0, src_ref.shape[0] // chunk, body, 0)


def _unpack_rows(x_ref):
    n = x_ref.shape[0] // PACK_TILES
    xp = jnp.concatenate([x_ref[pl.ds(j, n, stride=PACK_TILES), :] for j in range(PACK_TILES)], axis=1)
    hi = lax.bitcast_convert_type(xp & jnp.uint32(0xFFFF0000), F32).astype(BF16)
    lo = lax.bitcast_convert_type(xp << 16, F32).astype(BF16)
    return jnp.concatenate([hi, lo], axis=1)


def _expert_gu_kernel(meta_ref, x_ref, w_hbm, bg_ref, bu_ref, o_ref, wg_f, wu_f, wg_s, wu_s, sems,
                      *, n_blocks, layer):
    j, i = pl.program_id(0), pl.program_id(1)
    tn = wg_f.shape[1]
    nj = pl.num_programs(0)

    def fetch(e, jj):
        col = pl.multiple_of(jj * tn, tn)
        return (pltpu.make_async_copy(w_hbm.at[layer, e, :, pl.ds(col, tn)], wg_f, sems.at[0]),
                pltpu.make_async_copy(w_hbm.at[layer, e, :, pl.ds(EXPERT_FF + col, tn)], wu_f, sems.at[1]))

    @pl.when(jnp.logical_and(j == 0, i == 0))
    def _():
        for cp in fetch(meta_ref[0], 0):
            cp.start()

    active = i < meta_ref[n_blocks]

    @pl.when(jnp.logical_and(active, _is_new_expert(meta_ref, i)))
    def _():
        for cp in fetch(meta_ref[i], j):
            cp.wait()
        _cast_rows(wg_f, wg_s)
        _cast_rows(wu_f, wu_s)
        nxt = meta_ref[n_blocks + 1 + i]

        @pl.when(nxt >= 0)
        def _():
            for cp in fetch(nxt, j):
                cp.start()

        @pl.when(jnp.logical_and(nxt < 0, j + 1 < nj))
        def _():
            for cp in fetch(meta_ref[0], j + 1):
                cp.start()

    @pl.when(active)
    def _():
        x = _unpack_rows(x_ref)
        g = jnp.dot(x, wg_s[...], preferred_element_type=F32) + bg_ref[...]
        u = jnp.dot(x, wu_s[...], preferred_element_type=F32) + bu_ref[...]
        g = jnp.minimum(g, SWIGLU_LIMIT)
        u = jnp.clip(u, -SWIGLU_LIMIT, SWIGLU_LIMIT)
        o_ref[...] = ((u + 1.0) * (g * _sigmoid(g * SWIGLU_ALPHA))).astype(o_ref.dtype)

    @pl.when(jnp.logical_not(active))
    def _():
        o_ref[...] = jnp.zeros_like(o_ref)


def _active_block(i, m, n_blocks):
    return jnp.minimum(i, m[n_blocks] - 1)


def _expert_gu(meta, xs, w_gu, b_gu, layer, n_blocks):
    tm, tn = EXPERT_TM, 1024
    nj = EXPERT_FF // tn
    b3 = b_gu.reshape(DEPTH, N_EXPERTS, 1, 2 * EXPERT_FF)
    return pl.pallas_call(
        functools.partial(_expert_gu_kernel, n_blocks=n_blocks, layer=layer),
        grid_spec=pltpu.PrefetchScalarGridSpec(
            num_scalar_prefetch=1,
            grid=(nj, n_blocks),
            in_specs=[
                pl.BlockSpec((tm * PACK_TILES, LANES), lambda j, i, m: (_active_block(i, m, n_blocks), 0)),
                pl.BlockSpec(memory_space=pl.ANY),
                pl.BlockSpec((None, None, 1, tn), lambda j, i, m: (layer, m[i], 0, j)),
                pl.BlockSpec((None, None, 1, tn), lambda j, i, m: (layer, m[i], 0, nj + j)),
            ],
            out_specs=pl.BlockSpec((tm, tn), lambda j, i, m: (i, j)),
            scratch_shapes=[pltpu.VMEM((D_MODEL, tn), F32), pltpu.VMEM((D_MODEL, tn), F32),
                            pltpu.VMEM((D_MODEL, tn), BF16), pltpu.VMEM((D_MODEL, tn), BF16),
                            pltpu.SemaphoreType.DMA((2,))],
        ),
        out_shape=jax.ShapeDtypeStruct((n_blocks * tm, EXPERT_FF), BF16),
        compiler_params=_cparams(("arbitrary", "arbitrary")),
        name="expert_gu",
    )(meta, xs, w_gu, b3, b3)


def _expert_down_kernel(meta_ref, a_ref, w_hbm, b_ref, o_ref, w_f, w_s, sem, *, n_blocks, layer):
    i = pl.program_id(0)

    def fetch(e):
        return pltpu.make_async_copy(w_hbm.at[layer, e], w_f, sem)

    @pl.when(i == 0)
    def _():
        fetch(meta_ref[0]).start()

    active = i < meta_ref[n_blocks]

    @pl.when(jnp.logical_and(active, _is_new_expert(meta_ref, i)))
    def _():
        fetch(meta_ref[i]).wait()
        _cast_rows(w_f, w_s)
        nxt = meta_ref[n_blocks + 1 + i]

        @pl.when(nxt >= 0)
        def _():
            fetch(nxt).start()

    @pl.when(active)
    def _():
        o_ref[...] = jnp.dot(a_ref[...], w_s[...], preferred_element_type=F32) + b_ref[...]

    @pl.when(jnp.logical_not(active))
    def _():
        o_ref[...] = jnp.zeros_like(o_ref)


def _expert_down(meta, act, w_down, b_down, layer, n_blocks):
    tm = EXPERT_TM
    b3 = b_down.reshape(DEPTH, N_EXPERTS, 1, D_MODEL)
    return pl.pallas_call(
        functools.partial(_expert_down_kernel, n_blocks=n_blocks, layer=layer),
        grid_spec=pltpu.PrefetchScalarGridSpec(
            num_scalar_prefetch=1,
            grid=(n_blocks,),
            in_specs=[
                pl.BlockSpec((tm, EXPERT_FF), lambda i, m: (_active_block(i, m, n_blocks), 0)),
                pl.BlockSpec(memory_space=pl.ANY),
                pl.BlockSpec((None, None, 1, D_MODEL), lambda i, m: (layer, m[i], 0, 0)),
            ],
            out_specs=pl.BlockSpec((tm, D_MODEL), lambda i, m: (i, 0)),
            scratch_shapes=[pltpu.VMEM((EXPERT_FF, D_MODEL), F32), pltpu.VMEM((EXPERT_FF, D_MODEL), BF16),
                            pltpu.SemaphoreType.DMA(())],
        ),
        out_shape=jax.ShapeDtypeStruct((n_blocks * tm, D_MODEL), F32),
        compiler_params=_cparams(("arbitrary",)),
        name="expert_down",
    )(meta, act, w_down, b3)


def _combine_kernel(dest_ref, y_hbm, gt_ref, x_ref, g_ref, o_ref, buf, sems, *, rows, n_rows):
    i = pl.program_id(0)
    n_steps = pl.num_programs(0)

    def issue(step, slot):
        base = step * rows

        def body(r8, carry):
            r0 = pl.multiple_of(r8 * 8, 8)
            for u in range(8):
                for k in range(TOP_K):
                    d = dest_ref[k * n_rows + base + r0 + u]
                    pltpu.make_async_copy(y_hbm.at[pl.ds(d, 1), :],
                                          buf.at[slot, k, pl.ds(r0, 8), :].at[pl.ds(u, 1), :],
                                          sems.at[slot]).start(priority=k % 2)
            return carry

        lax.fori_loop(0, rows // 8, body, 0)

    @pl.when(i == 0)
    def _():
        issue(0, 0)

    @pl.when(i + 1 < n_steps)
    def _():
        issue(i + 1, (i + 1) % 2)

    slot = i % 2
    for k in range(TOP_K):
        pltpu.make_async_copy(y_hbm.at[pl.ds(0, rows), :], buf.at[slot, k], sems.at[slot]).wait()
    gt = gt_ref[...]
    moe = gt[:, 0:1] * buf[slot, 0]
    for k in range(1, TOP_K):
        moe = moe + gt[:, k:k + 1] * buf[slot, k]
    o_ref[...] = x_ref[...] + g_ref[...] * moe


def _combine(dest_flat, ys, gates_t, tok, mod3, n_rows):
    rows = 128
    return pl.pallas_call(
        functools.partial(_combine_kernel, rows=rows, n_rows=n_rows),
        grid_spec=pltpu.PrefetchScalarGridSpec(
            num_scalar_prefetch=1,
            grid=(n_rows // rows,),
            in_specs=[
                pl.BlockSpec(memory_space=pl.ANY),
                pl.BlockSpec((rows, LANES), lambda i, d: (i, 0)),
                pl.BlockSpec((rows, D_MODEL), lambda i, d: (i, 0)),
                pl.BlockSpec((None, 1, D_MODEL), lambda i, d: (_mod_row(i, rows), 0, 5)),
            ],
            out_specs=pl.BlockSpec((rows, D_MODEL), lambda i, d: (i, 0)),
            scratch_shapes=[pltpu.VMEM((2, TOP_K, rows, D_MODEL), F32), pltpu.SemaphoreType.DMA((2,))],
        ),
        out_shape=jax.ShapeDtypeStruct((n_rows, D_MODEL), F32),
        compiler_params=_cparams(("arbitrary",)),
        name="combine",
    )(dest_flat, ys, gates_t, tok, mod3)


def _routing(top_idx, rank, counts, n_rows):
    tm = EXPERT_TM
    n_blocks = -(-(TOP_K * n_rows) // tm) + N_EXPERTS
    counts = counts[:, 0]
    padded = (counts + tm - 1) // tm * tm
    padded_end = jnp.cumsum(padded)
    padded_start = padded_end - padded
    experts = jnp.arange(N_EXPERTS, dtype=jnp.int32)
    start_of = jnp.sum(jnp.where(top_idx[:, :, None] == experts, padded_start, 0), axis=-1)
    dest = (start_of + rank).astype(jnp.int32).reshape(TOP_K * n_rows)
    block_start = jnp.arange(n_blocks, dtype=jnp.int32) * tm
    block_expert = jnp.minimum(jnp.sum(padded_end[None, :] <= block_start[:, None], axis=1), N_EXPERTS - 1)
    n_active = padded_end[-1:] // tm
    group_end = jnp.sum(jnp.where(block_expert[:, None] == experts, padded_end // tm, 0), axis=1)
    follower = jnp.sum(jnp.where(group_end[:, None] == jnp.arange(n_blocks)[None, :], block_expert, 0), axis=1)
    next_expert = jnp.where(group_end < n_active, follower, -1)
    meta = jnp.concatenate([block_expert, n_active, next_expert]).astype(jnp.int32)
    n_pad = jnp.full((1,), n_blocks * tm, jnp.int32)
    pad_meta = jnp.concatenate([padded_start + counts, padded_end[-1:], padded_end, n_pad]).astype(jnp.int32)
    return dest, meta, pad_meta, n_blocks


def _rope_tables():
    rows = SEQ // GRID_W
    row = jnp.broadcast_to(jnp.arange(rows, dtype=F32)[:, None], (rows, GRID_W)).reshape(-1)
    col = jnp.broadcast_to(jnp.arange(GRID_W, dtype=F32)[None, :], (rows, GRID_W)).reshape(-1)
    inv_freq = ROPE_THETA ** (-jnp.arange(ROPE_FREQS, dtype=F32) / ROPE_FREQS)
    ang_r = row[:, None] * inv_freq
    ang_c = col[:, None] * inv_freq
    cos = jnp.concatenate([jnp.cos(ang_r), jnp.cos(ang_r), jnp.cos(ang_c), jnp.cos(ang_c)], axis=-1)
    sin = jnp.concatenate([-jnp.sin(ang_r), jnp.sin(ang_r), -jnp.sin(ang_c), jnp.sin(ang_c)], axis=-1)
    return cos, sin


def kernel(x, c, ctx, c_ctx, ada_w, ada_b, norm1_g, norm2_g, w_in, q_norm_g, k_norm_g, pool_w, pool_scale,
           sgu_norm_g, sgu_w, sgu_b, w_out, router_w, router_b, w_gu, b_gu, w_down, b_down):
    cos, sin = _rope_tables()
    tok = (x.reshape(N_LAT, D_MODEL), ctx.reshape(N_CTX, D_MODEL))
    cc = jnp.zeros((MOD_ROWS, D_MODEL), F32).at[:BATCH].set(c).at[BATCH].set(c_ctx)
    mod = _adaln(cc, ada_w, ada_b)
    tri = jnp.triu(jnp.ones((ROUTER_TM, ROUTER_TM), BF16), k=1)
    for l in range(DEPTH):
        last = l == DEPTH - 1
        n_rows = N_LAT if last else N_TOK
        mod3 = mod[l].reshape(MOD_ROWS, 1, 6 * D_MODEL)
        proj = _in_proj(tok, norm1_g[l], mod3, w_in[l].astype(BF16))
        attn = _latent_attention(proj, cos, sin, q_norm_g[l], k_norm_g[l])
        if not last:
            attn = (attn, _context_attention(proj, q_norm_g[l], k_norm_g[l]))
        sgu_b_full = jnp.broadcast_to(sgu_b[l][:, :, None], (N_SGU_HEADS, CHUNK, LANES))
        mix = _mixers(proj, pool_w[l].astype(BF16), pool_scale[l], sgu_norm_g[l], sgu_w[l].astype(BF16), sgu_b_full)
        tok = _out_proj(attn, mix, w_out[l].astype(BF16), tok, mod3, n_rows)
        rw_hi = router_w[l].astype(BF16)
        rw_lo = (router_w[l] - rw_hi.astype(F32)).astype(BF16)
        rw_pad = (jnp.zeros((D_MODEL, 2 * LANES), BF16).at[:, :N_EXPERTS].set(rw_hi)
                  .at[:, LANES:LANES + N_EXPERTS].set(rw_lo))
        rb_pad = jnp.zeros((1, LANES), F32).at[0, :N_EXPERTS].set(router_b[l])
        hp, top_idx, rank, gates_t, counts = _router(tok, norm2_g[l], mod3, rw_pad, rb_pad, tri, n_rows)
        dest, meta, pad_meta, n_blocks = _routing(top_idx, rank, counts, n_rows)
        n_pad = n_blocks * EXPERT_TM
        xs = _dispatch(dest, pad_meta, hp.reshape(n_rows, PACK_TILES, LANES), n_pad, n_rows)
        act = _expert_gu(meta, xs.reshape(n_pad * PACK_TILES, LANES), w_gu, b_gu, l, n_blocks)
        ys = _expert_down(meta, act, w_down, b_down, l, n_blocks)
        tok = _combine(dest, ys, gates_t, tok, mod3, n_rows)
    return tok.reshape(BATCH, SEQ, D_MODEL)
```

```python
import functools

import jax
import jax.numpy as jnp
from jax import lax
from jax.experimental import pallas as pl
from jax.experimental.pallas import tpu as pltpu

F32 = jnp.float32
BF16 = jnp.bfloat16
U32 = jnp.uint32

D_MODEL = 2048
BATCH = 4
SEQ = 4096
DEPTH = 2
GRID_W = 64
CTX_LEN = 256
HEAD_DIM = 128
ATTN_W = 1024
N_Q_HEADS = 8
GQA_GROUP = 4
N_KV_HEADS = 2
KV_W = 256
ROPE_THETA = 10000.0
ROPE_FREQS = 32
ATTN_SCALE = HEAD_DIM ** -0.5
LOG2_E = 1.4426950408889634
ATTN_TK = 512
POOL_WINDOWS = (2, 4, 8, 16)
POOL_W = 512
SGU_W = 512
N_SGU_HEADS = 4
CHUNK = 128
IN_W = 3072
N_EXPERTS = 32
TOP_K = 4
EXPERT_FF = 2048
SWIGLU_LIMIT = 7.0
SWIGLU_ALPHA = 1.702
EPS = 1e-6

N_LAT = BATCH * SEQ
N_CTX = BATCH * CTX_LEN
N_TOK = N_LAT + N_CTX
MOD_ROWS = 8
LANES = 128
POOL_HALO = 8
MIX_TM = 256
EXPERT_TM = 256
EXPERT_ROW_STEPS = (64, 128, 192, 256)
ROUTER_TM = 512
HALF_D = D_MODEL // 2
PACK_TILES = HALF_D // LANES
VMEM_LIMIT = 56 * 1024 * 1024


def _cparams(sem, vmem=VMEM_LIMIT):
    return pltpu.CompilerParams(dimension_semantics=sem, vmem_limit_bytes=vmem)


def _mod_row(row_tile, tm):
    return jnp.minimum(row_tile * tm // SEQ, BATCH)


def _rms(x, g):
    return x * lax.rsqrt(jnp.mean(x * x, axis=-1, keepdims=True) + EPS) * g


def _sigmoid(x):
    return 1.0 / (1.0 + jnp.exp(-x))


def _gelu(x):
    return 0.5 * x * (1.0 + lax.erf(x * 0.7071067811865476))


def _adaln_kernel(cc_ref, w_ref, b_ref, o_ref):
    cc = cc_ref[...]
    s = (cc * _sigmoid(cc)).astype(BF16)
    o_ref[...] = jnp.dot(s, w_ref[...].astype(BF16), preferred_element_type=F32) + b_ref[...]


def _adaln(cc, ada_w, ada_b):
    tn = 1024
    n = 6 * D_MODEL
    return pl.pallas_call(
        _adaln_kernel,
        grid=(DEPTH, n // tn),
        in_specs=[
            pl.BlockSpec((MOD_ROWS, D_MODEL), lambda l, j: (0, 0)),
            pl.BlockSpec((None, D_MODEL, tn), lambda l, j: (l, 0, j)),
            pl.BlockSpec((None, 1, tn), lambda l, j: (l, 0, j)),
        ],
        out_specs=pl.BlockSpec((None, MOD_ROWS, tn), lambda l, j: (l, 0, j)),
        out_shape=jax.ShapeDtypeStruct((DEPTH, MOD_ROWS, n), F32),
        compiler_params=_cparams(("arbitrary", "arbitrary")),
        name="adaln",
    )(cc, ada_w, ada_b.reshape(DEPTH, 1, n))


def _proj_kernel(*refs, n_first):
    if n_first is None:
        (xa_ref, g_ref, sh_ref, sc_ref, w_ref, o_ref, h_s), xb_ref = refs, None
    else:
        xa_ref, xb_ref, g_ref, sh_ref, sc_ref, w_ref, o_ref, h_s = refs
    i, j = pl.program_id(0), pl.program_id(1)

    def prepare(tile, slot):
        x = xa_ref[...]
        if xb_ref is not None:
            x = jnp.where(tile < n_first, x, xb_ref[...])
        y = _rms(x, g_ref[...])
        h_s[slot] = (y * (1.0 + sc_ref[...]) + sh_ref[...]).astype(BF16)

    @pl.when(jnp.logical_and(i == 0, j == 0))
    def _():
        prepare(0, 0)

    last = j == pl.num_programs(1) - 1

    @pl.when(last)
    def _():
        o_ref[...] = jnp.dot(h_s[i % 2], w_ref[...], preferred_element_type=F32)
        prepare(i + 1, (i + 1) % 2)

    @pl.when(jnp.logical_not(last))
    def _():
        o_ref[...] = jnp.dot(h_s[i % 2], w_ref[...], preferred_element_type=F32)


def _in_proj(src, norm_g, mod3, w_bf):
    tm, tn = 512, 1024
    n_tiles, nj = N_TOK // tm, IN_W // tn

    def tile(i, j):
        return jnp.minimum(i + (j + 1) // nj, n_tiles - 1)

    if isinstance(src, tuple):
        n_first = N_LAT // tm
        x_specs = [pl.BlockSpec((tm, D_MODEL), lambda i, j: (jnp.minimum(tile(i, j), n_first - 1), 0)),
                   pl.BlockSpec((tm, D_MODEL), lambda i, j: (jnp.maximum(tile(i, j) - n_first, 0), 0))]
    else:
        n_first, src = None, (src,)
        x_specs = [pl.BlockSpec((tm, D_MODEL), lambda i, j: (tile(i, j), 0))]
    return pl.pallas_call(
        functools.partial(_proj_kernel, n_first=n_first),
        grid=(n_tiles, nj),
        in_specs=x_specs + [
            pl.BlockSpec((1, D_MODEL), lambda i, j: (0, 0)),
            pl.BlockSpec((None, 1, D_MODEL), lambda i, j: (_mod_row(tile(i, j), tm), 0, 0)),
            pl.BlockSpec((None, 1, D_MODEL), lambda i, j: (_mod_row(tile(i, j), tm), 0, 1)),
            pl.BlockSpec((D_MODEL, tn), lambda i, j: (0, j)),
        ],
        out_specs=pl.BlockSpec((tm, tn), lambda i, j: (i, j)),
        out_shape=jax.ShapeDtypeStruct((N_TOK, IN_W), F32),
        scratch_shapes=[pltpu.VMEM((2, tm, D_MODEL), BF16)],
        compiler_params=_cparams(("arbitrary", "arbitrary")),
        name="in_proj",
    )(*src, norm_g.reshape(1, D_MODEL), mod3, mod3, w_bf)


def _rope(x, cos, sin):
    lane = lax.broadcasted_iota(jnp.int32, x.shape, 1)
    first = (lane % 64) < 32
    partner = jnp.where(first, pltpu.roll(x, 96, 1), pltpu.roll(x, 32, 1))
    return x * cos + partner * sin


def _softmax_pv(q, k, v):
    s = lax.dot_general(q, k, (((1,), (1,)), ((), ())), preferred_element_type=F32) * ATTN_SCALE
    m = jnp.max(s, axis=-1, keepdims=True)
    p = jnp.exp(s - m)
    l = jnp.sum(p, axis=-1, keepdims=True)
    return jnp.dot(p.astype(BF16), v, preferred_element_type=F32) / l


def _lat_attn_kernel(q_ref, kl_ref, vl_ref, kc_ref, vc_ref, cosq_ref, sinq_ref, cosk_ref, sink_ref,
                     qg_ref, kg_ref, o_ref, k_s, v_s, s_s):
    @pl.when(pl.program_id(2) == 0)
    def _():
        k_s[0:CTX_LEN, :] = _rms(kc_ref[...], kg_ref[...]).astype(BF16)
        kl = _rope(_rms(kl_ref[...], kg_ref[...]), cosk_ref[...], sink_ref[...])
        k_s[CTX_LEN:, :] = kl.astype(BF16)
        v_s[0:CTX_LEN, 0:HEAD_DIM] = vc_ref[...].astype(BF16)
        v_s[CTX_LEN:, 0:HEAD_DIM] = vl_ref[...].astype(BF16)
        v_s[:, HEAD_DIM:] = jnp.ones((CTX_LEN + SEQ, LANES), BF16)

    def scores(g):
        lanes = slice(g * HEAD_DIM, (g + 1) * HEAD_DIM)
        q = _rope(_rms(q_ref[:, lanes], qg_ref[...]), cosq_ref[...], sinq_ref[...])
        q = (q * (ATTN_SCALE * LOG2_E)).astype(BF16)
        s_s[g % 2] = lax.dot_general(q, k_s[...], (((1,), (1,)), ((), ())), preferred_element_type=F32)

    scores(0)
    for g in range(GQA_GROUP):
        if g + 1 < GQA_GROUP:
            scores(g + 1)
        s = s_s[g % 2]
        p = jnp.exp2(s - jnp.max(s, axis=-1, keepdims=True))
        ol = jnp.dot(p.astype(BF16), v_s[...], preferred_element_type=F32)
        o = ol[:, 0:HEAD_DIM] / ol[:, HEAD_DIM:]
        o_ref[:, g * HEAD_DIM:(g + 1) * HEAD_DIM] = o.astype(o_ref.dtype)


def _latent_attention(proj, cos, sin, q_g, k_g):
    tq = 256
    nq = SEQ // tq
    qw = GQA_GROUP * HEAD_DIM
    kcol = ATTN_W // HEAD_DIM
    vcol = (ATTN_W + KV_W) // HEAD_DIM
    ctx_blk = N_LAT // CTX_LEN
    return pl.pallas_call(
        _lat_attn_kernel,
        grid=(BATCH, N_KV_HEADS, nq),
        in_specs=[
            pl.BlockSpec((tq, qw), lambda b, h, i: (b * nq + i, h)),
            pl.BlockSpec((SEQ, HEAD_DIM), lambda b, h, i: (b, kcol + h)),
            pl.BlockSpec((SEQ, HEAD_DIM), lambda b, h, i: (b, vcol + h)),
            pl.BlockSpec((CTX_LEN, HEAD_DIM), lambda b, h, i: (ctx_blk + b, kcol + h)),
            pl.BlockSpec((CTX_LEN, HEAD_DIM), lambda b, h, i: (ctx_blk + b, vcol + h)),
            pl.BlockSpec((tq, HEAD_DIM), lambda b, h, i: (i, 0)),
            pl.BlockSpec((tq, HEAD_DIM), lambda b, h, i: (i, 0)),
            pl.BlockSpec((SEQ, HEAD_DIM), lambda b, h, i: (0, 0)),
            pl.BlockSpec((SEQ, HEAD_DIM), lambda b, h, i: (0, 0)),
            pl.BlockSpec((1, HEAD_DIM), lambda b, h, i: (0, 0)),
            pl.BlockSpec((1, HEAD_DIM), lambda b, h, i: (0, 0)),
        ],
        out_specs=pl.BlockSpec((tq, qw), lambda b, h, i: (b * nq + i, h)),
        out_shape=jax.ShapeDtypeStruct((N_LAT, ATTN_W), BF16),
        scratch_shapes=[pltpu.VMEM((CTX_LEN + SEQ, HEAD_DIM), BF16),
                        pltpu.VMEM((CTX_LEN + SEQ, HEAD_DIM + LANES), BF16),
                        pltpu.VMEM((2, tq, CTX_LEN + SEQ), F32)],
        compiler_params=_cparams(("arbitrary", "arbitrary", "arbitrary")),
        name="latent_attention",
    )(proj, proj, proj, proj, proj, cos, sin, cos, sin, q_g.reshape(1, HEAD_DIM), k_g.reshape(1, HEAD_DIM))


def _ctx_attn_kernel(q_ref, k_ref, v_ref, qg_ref, kg_ref, o_ref):
    k = _rms(k_ref[...], kg_ref[...]).astype(BF16)
    v = v_ref[...].astype(BF16)
    for g in range(GQA_GROUP):
        lanes = slice(g * HEAD_DIM, (g + 1) * HEAD_DIM)
        q = _rms(q_ref[:, lanes], qg_ref[...]).astype(BF16)
        o_ref[:, lanes] = _softmax_pv(q, k, v).astype(o_ref.dtype)


def _context_attention(proj, q_g, k_g):
    qw = GQA_GROUP * HEAD_DIM
    kcol = ATTN_W // HEAD_DIM
    vcol = (ATTN_W + KV_W) // HEAD_DIM
    ctx_blk = N_LAT // CTX_LEN
    return pl.pallas_call(
        _ctx_attn_kernel,
        grid=(BATCH, N_KV_HEADS),
        in_specs=[
            pl.BlockSpec((CTX_LEN, qw), lambda b, h: (ctx_blk + b, h)),
            pl.BlockSpec((CTX_LEN, HEAD_DIM), lambda b, h: (ctx_blk + b, kcol + h)),
            pl.BlockSpec((CTX_LEN, HEAD_DIM), lambda b, h: (ctx_blk + b, vcol + h)),
            pl.BlockSpec((1, HEAD_DIM), lambda b, h: (0, 0)),
            pl.BlockSpec((1, HEAD_DIM), lambda b, h: (0, 0)),
        ],
        out_specs=pl.BlockSpec((CTX_LEN, qw), lambda b, h: (b, h)),
        out_shape=jax.ShapeDtypeStruct((N_CTX, ATTN_W), BF16),
        compiler_params=_cparams(("arbitrary", "arbitrary")),
        name="context_attention",
    )(proj, proj, proj, q_g.reshape(1, HEAD_DIM), k_g.reshape(1, HEAD_DIM))


def _mixer_kernel(pin_ref, prev_ref, next_ref, su_ref, sv_ref, pw_ref, ps_ref, sg_ref, sw_ref, sb_ref,
                  o_ref, pad_ref):
    tm = MIX_TM
    i = pl.program_id(0)
    is_lat = i < N_LAT // tm
    pos0 = jnp.where(is_lat, (i % (SEQ // tm)) * tm, 0)
    seq_len = jnp.where(is_lat, SEQ, CTX_LEN)
    pad_ref[0:POOL_HALO, :] = jnp.where(pos0 == 0, 0.0, prev_ref[...])
    pad_ref[POOL_HALO:POOL_HALO + tm, :] = pin_ref[...]
    pad_ref[POOL_HALO + tm:, :] = jnp.where(pos0 + tm == seq_len, 0.0, next_ref[...])
    t = pos0 + lax.broadcasted_iota(jnp.int32, (tm, LANES), 0)
    for gi, w in enumerate(POOL_WINDOWS):
        lanes = slice(gi * LANES, (gi + 1) * LANES)
        acc = pad_ref[POOL_HALO - w // 2:POOL_HALO - w // 2 + tm, lanes]
        for d in range(-w // 2 + 1, w // 2):
            acc = acc + pad_ref[POOL_HALO + d:POOL_HALO + d + tm, lanes]
        cnt = (jnp.minimum(t + w // 2, seq_len) - jnp.maximum(t - w // 2, 0)).astype(F32)
        mixed = acc / cnt - pin_ref[:, lanes]
        y = jnp.dot(mixed.astype(BF16), pw_ref[gi], preferred_element_type=F32) * ps_ref[:, lanes]
        o_ref[:, lanes] = y.astype(o_ref.dtype)

    for h in range(N_SGU_HEADS):
        lanes = slice(h * LANES, (h + 1) * LANES)
        gu = _gelu(su_ref[:, lanes])
        vh = _rms(_gelu(sv_ref[:, lanes]), sg_ref[h:h + 1, :]).astype(BF16)
        for n in range(tm // CHUNK):
            rows = slice(n * CHUNK, (n + 1) * CHUNK)
            mixed = jnp.dot(sw_ref[h], vh[rows], preferred_element_type=F32) + sb_ref[h]
            o_ref[rows, POOL_W + h * LANES:POOL_W + (h + 1) * LANES] = (gu[rows] * mixed).astype(o_ref.dtype)


def _mixers(proj, pool_w_bf, pool_scale, sgu_norm_g, sgu_w_bf, sgu_b_full):
    tm = MIX_TM
    per_tile = tm // POOL_HALO
    last_halo = N_TOK // POOL_HALO - 1
    pcol = (ATTN_W + 2 * KV_W) // POOL_W
    return pl.pallas_call(
        _mixer_kernel,
        grid=(N_TOK // tm,),
        in_specs=[
            pl.BlockSpec((tm, POOL_W), lambda i: (i, pcol)),
            pl.BlockSpec((POOL_HALO, POOL_W), lambda i: (jnp.maximum(i * per_tile - 1, 0), pcol)),
            pl.BlockSpec((POOL_HALO, POOL_W), lambda i: (jnp.minimum((i + 1) * per_tile, last_halo), pcol)),
            pl.BlockSpec((tm, SGU_W), lambda i: (i, pcol + 1)),
            pl.BlockSpec((tm, SGU_W), lambda i: (i, pcol + 2)),
            pl.BlockSpec((len(POOL_WINDOWS), LANES, LANES), lambda i: (0, 0, 0)),
            pl.BlockSpec((1, POOL_W), lambda i: (0, 0)),
            pl.BlockSpec((N_SGU_HEADS, LANES), lambda i: (0, 0)),
            pl.BlockSpec((N_SGU_HEADS, CHUNK, CHUNK), lambda i: (0, 0, 0)),
            pl.BlockSpec((N_SGU_HEADS, CHUNK, LANES), lambda i: (0, 0, 0)),
        ],
        out_specs=pl.BlockSpec((tm, POOL_W + SGU_W), lambda i: (i, 0)),
        out_shape=jax.ShapeDtypeStruct((N_TOK, POOL_W + SGU_W), BF16),
        scratch_shapes=[pltpu.VMEM((tm + 2 * POOL_HALO, POOL_W), F32)],
        compiler_params=_cparams(("arbitrary",)),
        name="mixers",
    )(proj, proj, proj, proj, proj, pool_w_bf, pool_scale.reshape(1, POOL_W), sgu_norm_g, sgu_w_bf, sgu_b_full)


def _outproj_kernel(*refs, n_first):
    if n_first is None:
        a_ref, m_ref, w1_ref, w2_ref, x_ref, g_ref, o_ref = refs
        a, x = a_ref[...], x_ref[...]
    else:
        a_ref, ac_ref, m_ref, w1_ref, w2_ref, x_ref, xc_ref, g_ref, o_ref = refs
        is_first = pl.program_id(1) < n_first
        a = jnp.where(is_first, a_ref[...], ac_ref[...])
        x = jnp.where(is_first, x_ref[...], xc_ref[...])
    y = jnp.dot(a, w1_ref[...], preferred_element_type=F32)
    y = y + jnp.dot(m_ref[...], w2_ref[...], preferred_element_type=F32)
    o_ref[...] = x + g_ref[...] * y


def _out_proj(attn, mix, w_bf, src, mod3, n_rows):
    tm, tn = 512, 1024
    nj = D_MODEL // tn
    assert isinstance(attn, tuple) == isinstance(src, tuple)
    if isinstance(src, tuple):
        n_first = N_LAT // tm

        def first(i):
            return jnp.minimum(i, n_first - 1)

        def second(i):
            return jnp.maximum(i - n_first, 0)

        attn_specs = [pl.BlockSpec((tm, ATTN_W), lambda j, i: (first(i), 0)),
                      pl.BlockSpec((tm, ATTN_W), lambda j, i: (second(i), 0))]
        x_specs = [pl.BlockSpec((tm, tn), lambda j, i: (first(i), j)),
                   pl.BlockSpec((tm, tn), lambda j, i: (second(i), j))]
    else:
        n_first, attn, src = None, (attn,), (src,)
        attn_specs = [pl.BlockSpec((tm, ATTN_W), lambda j, i: (i, 0))]
        x_specs = [pl.BlockSpec((tm, tn), lambda j, i: (i, j))]
    return pl.pallas_call(
        functools.partial(_outproj_kernel, n_first=n_first),
        grid=(nj, n_rows // tm),
        in_specs=attn_specs + [
            pl.BlockSpec((tm, POOL_W + SGU_W), lambda j, i: (i, 0)),
            pl.BlockSpec((ATTN_W, tn), lambda j, i: (0, j)),
            pl.BlockSpec((POOL_W + SGU_W, tn), lambda j, i: (1, j)),
        ] + x_specs + [
            pl.BlockSpec((None, 1, tn), lambda j, i: (_mod_row(i, tm), 0, 2 * nj + j)),
        ],
        out_specs=pl.BlockSpec((tm, tn), lambda j, i: (i, j)),
        out_shape=jax.ShapeDtypeStruct((n_rows, D_MODEL), F32),
        compiler_params=_cparams(("arbitrary", "arbitrary")),
        name="out_proj",
    )(*attn, mix, w_bf, w_bf, *src, mod3)


def _router_kernel(x_ref, g_ref, sh_ref, sc_ref, rw_ref, rb_ref, tri_ref,
                   hp_ref, idx_ref, rank_ref, gate_ref, cnt_ref, run_s):
    @pl.when(pl.program_id(0) == 0)
    def _():
        run_s[...] = jnp.zeros_like(run_s)

    h = _rms(x_ref[...], g_ref[...]) * (1.0 + sc_ref[...]) + sh_ref[...]
    h_hi = h.astype(BF16)
    hb = h_hi.astype(F32)
    hi = lax.bitcast_convert_type(hb[:, :HALF_D], U32)
    lo = lax.bitcast_convert_type(hb[:, HALF_D:], U32)
    packed = hi | (lo >> 16)
    for j in range(PACK_TILES):
        hp_ref[pl.ds(j, packed.shape[0], stride=PACK_TILES), :] = packed[:, j * LANES:(j + 1) * LANES]

    h_lo = (h - hb).astype(BF16)
    both = jnp.dot(h_hi, rw_ref[...], preferred_element_type=F32)
    tail = jnp.dot(h_lo, rw_ref[:, 0:LANES], preferred_element_type=F32)
    logits = (both[:, 0:LANES] + both[:, LANES:]) + tail
    lt = (logits + rb_ref[...]).T[0:N_EXPERTS, :]
    expert = lax.broadcasted_iota(jnp.int32, lt.shape, 0).astype(F32)
    vals, idxs = [], []
    for _ in range(TOP_K):
        m = jnp.max(lt, axis=0, keepdims=True)
        idx = jnp.min(jnp.where(lt == m, expert, float(N_EXPERTS)), axis=0, keepdims=True)
        vals.append(m)
        idxs.append(idx)
        lt = jnp.where(expert == idx, -jnp.inf, lt)

    e = [jnp.exp(v - vals[0]) for v in vals]
    den = e[0] + e[1] + e[2] + e[3]
    slot = lax.broadcasted_iota(jnp.int32, (LANES, lt.shape[1]), 0)
    gates = jnp.zeros((LANES, lt.shape[1]), F32)
    for k in range(TOP_K):
        gates = jnp.where(slot == k, e[k] / den, gates)
    gate_ref[...] = gates.T

    base = run_s[...]
    for k in range(TOP_K):
        onehot = jnp.where(expert == idxs[k], 1.0, 0.0)
        before = jnp.dot(onehot.astype(BF16), tri_ref[...], preferred_element_type=F32)
        rank = jnp.sum(onehot * (before + base[:, 0:1]), axis=0, keepdims=True)
        idx_ref[k:k + 1, :] = idxs[k].astype(jnp.int32)
        rank_ref[k:k + 1, :] = rank.astype(jnp.int32)
        base = base + jnp.sum(onehot, axis=1, keepdims=True)
    run_s[...] = base
    cnt_ref[...] = base.astype(jnp.int32)


def _router(tok, norm_g, mod3, rw_pad, rb_pad, tri, n_rows):
    tm = ROUTER_TM
    return pl.pallas_call(
        _router_kernel,
        grid=(n_rows // tm,),
        in_specs=[
            pl.BlockSpec((tm, D_MODEL), lambda i: (i, 0)),
            pl.BlockSpec((1, D_MODEL), lambda i: (0, 0)),
            pl.BlockSpec((None, 1, D_MODEL), lambda i: (_mod_row(i, tm), 0, 3)),
            pl.BlockSpec((None, 1, D_MODEL), lambda i: (_mod_row(i, tm), 0, 4)),
            pl.BlockSpec((D_MODEL, 2 * LANES), lambda i: (0, 0)),
            pl.BlockSpec((1, LANES), lambda i: (0, 0)),
            pl.BlockSpec((tm, tm), lambda i: (0, 0)),
        ],
        out_specs=[
            pl.BlockSpec((tm * PACK_TILES, LANES), lambda i: (i, 0)),
            pl.BlockSpec((TOP_K, tm), lambda i: (0, i)),
            pl.BlockSpec((TOP_K, tm), lambda i: (0, i)),
            pl.BlockSpec((tm, LANES), lambda i: (i, 0)),
            pl.BlockSpec((N_EXPERTS, LANES), lambda i: (0, 0)),
        ],
        out_shape=[
            jax.ShapeDtypeStruct((n_rows * PACK_TILES, LANES), U32),
            jax.ShapeDtypeStruct((TOP_K, n_rows), jnp.int32),
            jax.ShapeDtypeStruct((TOP_K, n_rows), jnp.int32),
            jax.ShapeDtypeStruct((n_rows, LANES), F32),
            jax.ShapeDtypeStruct((N_EXPERTS, LANES), jnp.int32),
        ],
        scratch_shapes=[pltpu.VMEM((N_EXPERTS, LANES), F32)],
        compiler_params=_cparams(("arbitrary",)),
        name="router",
    )(tok, norm_g.reshape(1, D_MODEL), mod3, mod3, rw_pad, rb_pad, tri)


def _dispatch_kernel(dest_ref, pad_ref, hp_ref, xs_hbm, sem, pad_sem, *, rows, n_rows):
    i = pl.program_id(0)

    @pl.when(i == 0)
    def _():
        def per_expert(e, carry):
            lo, hi = pad_ref[e], pad_ref[N_EXPERTS + 1 + e]

            def start(p, c):
                pltpu.make_async_copy(hp_ref.at[0], xs_hbm.at[p], pad_sem).start()
                return c

            def wait(p, c):
                pltpu.make_async_copy(hp_ref.at[0], xs_hbm.at[p], pad_sem).wait()
                return c

            lax.fori_loop(lo, hi, start, 0)
            lax.fori_loop(lo, hi, wait, 0)
            return carry

        lax.fori_loop(0, N_EXPERTS + 1, per_expert, 0)

    base = i * rows

    def issue(r2, carry):
        for u in range(2):
            r = 2 * r2 + u
            for k in range(TOP_K):
                d = dest_ref[k * n_rows + base + r]
                pltpu.make_async_copy(hp_ref.at[r], xs_hbm.at[d], sem).start(priority=k % 2)
        return carry

    lax.fori_loop(0, rows // 2, issue, 0)
    for k in range(TOP_K):
        pltpu.make_async_copy(hp_ref, xs_hbm.at[pl.ds(0, rows)], sem).wait()


def _dispatch(dest_flat, pad_meta, hp, n_pad, n_rows):
    rows = 1024
    return pl.pallas_call(
        functools.partial(_dispatch_kernel, rows=rows, n_rows=n_rows),
        grid_spec=pltpu.PrefetchScalarGridSpec(
            num_scalar_prefetch=2,
            grid=(n_rows // rows,),
            in_specs=[pl.BlockSpec((rows, PACK_TILES, LANES), lambda i, d, p: (i, 0, 0))],
            out_specs=pl.BlockSpec(memory_space=pl.ANY),
            scratch_shapes=[pltpu.SemaphoreType.DMA(()), pltpu.SemaphoreType.DMA(())],
        ),
        out_shape=jax.ShapeDtypeStruct((n_pad, PACK_TILES, LANES), U32),
        compiler_params=_cparams(("arbitrary",)),
        name="dispatch",
    )(dest_flat, pad_meta, hp)


def _is_new_expert(meta_ref, i):
    prev = meta_ref[jnp.maximum(i - 1, 0)]
    return jnp.logical_or(i == 0, meta_ref[i] != prev)


def _cast_rows(src_ref, dst_ref, chunk=256):
    def body(c, carry):
        rows = pl.ds(pl.multiple_of(c * chunk, chunk), chunk)
        dst_ref[rows, :] = src_ref[rows, :].astype(dst_ref.dtype)
        return carry

    lax.fori_loop(0, src_ref.shape[0] // chunk, body, 0)


def _for_valid_rows(valid, fn):
    lo = 0
    for n in EXPERT_ROW_STEPS:
        @pl.when(jnp.logical_and(valid > lo, valid <= n))
        def _(n=n):
            fn(n)

        lo = n


def _unpack_rows(x_ref, n):
    xp = jnp.concatenate([x_ref[pl.ds(j, n, stride=PACK_TILES), :] for j in range(PACK_TILES)], axis=1)
    hi = lax.bitcast_convert_type(xp & jnp.uint32(0xFFFF0000), F32).astype(BF16)
    lo = lax.bitcast_convert_type(xp << 16, F32).astype(BF16)
    return jnp.concatenate([hi, lo], axis=1)


def _expert_gu_kernel(meta_ref, x_ref, w_hbm, bg_ref, bu_ref, o_ref, wg_f, wu_f, wg_s, wu_s, sems,
                      *, n_blocks, layer):
    j, i = pl.program_id(0), pl.program_id(1)
    tn = wg_f.shape[1]
    nj = pl.num_programs(0)

    def fetch(e, jj):
        col = pl.multiple_of(jj * tn, tn)
        return (pltpu.make_async_copy(w_hbm.at[layer, e, :, pl.ds(col, tn)], wg_f, sems.at[0]),
                pltpu.make_async_copy(w_hbm.at[layer, e, :, pl.ds(EXPERT_FF + col, tn)], wu_f, sems.at[1]))

    @pl.when(jnp.logical_and(j == 0, i == 0))
    def _():
        for cp in fetch(meta_ref[0], 0):
            cp.start()

    active = i < meta_ref[n_blocks]

    @pl.when(jnp.logical_and(active, _is_new_expert(meta_ref, i)))
    def _():
        for cp in fetch(meta_ref[i], j):
            cp.wait()
        _cast_rows(wg_f, wg_s)
        _cast_rows(wu_f, wu_s)
        nxt = meta_ref[n_blocks + 1 + i]

        @pl.when(nxt >= 0)
        def _():
            for cp in fetch(nxt, j):
                cp.start()

        @pl.when(jnp.logical_and(nxt < 0, j + 1 < nj))
        def _():
            for cp in fetch(meta_ref[0], j + 1):
                cp.start()

    def compute(n):
        x = _unpack_rows(x_ref, n)
        g = jnp.dot(x, wg_s[...], preferred_element_type=F32) + bg_ref[...]
        u = jnp.dot(x, wu_s[...], preferred_element_type=F32) + bu_ref[...]
        g = jnp.minimum(g, SWIGLU_LIMIT)
        u = jnp.clip(u, -SWIGLU_LIMIT, SWIGLU_LIMIT)
        o_ref[0:n, :] = ((u + 1.0) * (g * _sigmoid(g * SWIGLU_ALPHA))).astype(o_ref.dtype)
        if n < o_ref.shape[0]:
            o_ref[n:, :] = jnp.zeros((o_ref.shape[0] - n, o_ref.shape[1]), o_ref.dtype)

    _for_valid_rows(meta_ref[2 * n_blocks + 1 + i], compute)

    @pl.when(jnp.logical_not(active))
    def _():
        o_ref[...] = jnp.zeros_like(o_ref)


def _active_block(i, m, n_blocks):
    return jnp.minimum(i, m[n_blocks] - 1)


def _expert_gu(meta, xs, w_gu, b_gu, layer, n_blocks):
    tm, tn = EXPERT_TM, 1024
    nj = EXPERT_FF // tn
    b3 = b_gu.reshape(DEPTH, N_EXPERTS, 1, 2 * EXPERT_FF)
    return pl.pallas_call(
        functools.partial(_expert_gu_kernel, n_blocks=n_blocks, layer=layer),
        grid_spec=pltpu.PrefetchScalarGridSpec(
            num_scalar_prefetch=1,
            grid=(nj, n_blocks),
            in_specs=[
                pl.BlockSpec((tm * PACK_TILES, LANES), lambda j, i, m: (_active_block(i, m, n_blocks), 0)),
                pl.BlockSpec(memory_space=pl.ANY),
                pl.BlockSpec((None, None, 1, tn), lambda j, i, m: (layer, m[i], 0, j)),
                pl.BlockSpec((None, None, 1, tn), lambda j, i, m: (layer, m[i], 0, nj + j)),
            ],
            out_specs=pl.BlockSpec((tm, tn), lambda j, i, m: (i, j)),
            scratch_shapes=[pltpu.VMEM((D_MODEL, tn), F32), pltpu.VMEM((D_MODEL, tn), F32),
                            pltpu.VMEM((D_MODEL, tn), BF16), pltpu.VMEM((D_MODEL, tn), BF16),
                            pltpu.SemaphoreType.DMA((2,))],
        ),
        out_shape=jax.ShapeDtypeStruct((n_blocks * tm, EXPERT_FF), BF16),
        compiler_params=_cparams(("arbitrary", "arbitrary")),
        name="expert_gu",
    )(meta, xs, w_gu, b3, b3)


def _expert_down_kernel(meta_ref, a_ref, w_hbm, b_ref, o_ref, w_f, w_s, sem, *, n_blocks, layer):
    i = pl.program_id(0)

    def fetch(e):
        return pltpu.make_async_copy(w_hbm.at[layer, e], w_f, sem)

    @pl.when(i == 0)
    def _():
        fetch(meta_ref[0]).start()

    active = i < meta_ref[n_blocks]

    @pl.when(jnp.logical_and(active, _is_new_expert(meta_ref, i)))
    def _():
        fetch(meta_ref[i]).wait()
        _cast_rows(w_f, w_s)
        nxt = meta_ref[n_blocks + 1 + i]

        @pl.when(nxt >= 0)
        def _():
            fetch(nxt).start()

    def compute(n):
        o_ref[0:n, :] = jnp.dot(a_ref[0:n, :], w_s[...], preferred_element_type=F32) + b_ref[...]
        if n < o_ref.shape[0]:
            o_ref[n:, :] = jnp.zeros((o_ref.shape[0] - n, o_ref.shape[1]), o_ref.dtype)

    _for_valid_rows(meta_ref[2 * n_blocks + 1 + i], compute)

    @pl.when(jnp.logical_not(active))
    def _():
        o_ref[...] = jnp.zeros_like(o_ref)


def _expert_down(meta, act, w_down, b_down, layer, n_blocks):
    tm = EXPERT_TM
    b3 = b_down.reshape(DEPTH, N_EXPERTS, 1, D_MODEL)
    return pl.pallas_call(
        functools.partial(_expert_down_kernel, n_blocks=n_blocks, layer=layer),
        grid_spec=pltpu.PrefetchScalarGridSpec(
            num_scalar_prefetch=1,
            grid=(n_blocks,),
            in_specs=[
                pl.BlockSpec((tm, EXPERT_FF), lambda i, m: (_active_block(i, m, n_blocks), 0)),
                pl.BlockSpec(memory_space=pl.ANY),
                pl.BlockSpec((None, None, 1, D_MODEL), lambda i, m: (layer, m[i], 0, 0)),
            ],
            out_specs=pl.BlockSpec((tm, D_MODEL), lambda i, m: (i, 0)),
            scratch_shapes=[pltpu.VMEM((EXPERT_FF, D_MODEL), F32), pltpu.VMEM((EXPERT_FF, D_MODEL), BF16),
                            pltpu.SemaphoreType.DMA(())],
        ),
        out_shape=jax.ShapeDtypeStruct((n_blocks * tm, D_MODEL), F32),
        compiler_params=_cparams(("arbitrary",)),
        name="expert_down",
    )(meta, act, w_down, b3)


def _combine_kernel(dest_ref, y_hbm, gt_ref, x_ref, g_ref, o_ref, buf, sems, *, rows, n_rows):
    i = pl.program_id(0)
    n_steps = pl.num_programs(0)

    def issue(step, slot):
        base = step * rows

        def body(r8, carry):
            r0 = pl.multiple_of(r8 * 8, 8)
            for u in range(8):
                for k in range(TOP_K):
                    d = dest_ref[k * n_rows + base + r0 + u]
                    pltpu.make_async_copy(y_hbm.at[pl.ds(d, 1), :],
                                          buf.at[slot, k, pl.ds(r0, 8), :].at[pl.ds(u, 1), :],
                                          sems.at[slot]).start(priority=k % 2)
            return carry

        lax.fori_loop(0, rows // 8, body, 0)

    @pl.when(i == 0)
    def _():
        issue(0, 0)

    @pl.when(i + 1 < n_steps)
    def _():
        issue(i + 1, (i + 1) % 2)

    slot = i % 2
    for k in range(TOP_K):
        pltpu.make_async_copy(y_hbm.at[pl.ds(0, rows), :], buf.at[slot, k], sems.at[slot]).wait()
    gt = gt_ref[...]
    moe = gt[:, 0:1] * buf[slot, 0]
    for k in range(1, TOP_K):
        moe = moe + gt[:, k:k + 1] * buf[slot, k]
    o_ref[...] = x_ref[...] + g_ref[...] * moe


def _combine(dest_flat, ys, gates_t, tok, mod3, n_rows):
    rows = 128
    return pl.pallas_call(
        functools.partial(_combine_kernel, rows=rows, n_rows=n_rows),
        grid_spec=pltpu.PrefetchScalarGridSpec(
            num_scalar_prefetch=1,
            grid=(n_rows // rows,),
            in_specs=[
                pl.BlockSpec(memory_space=pl.ANY),
                pl.BlockSpec((rows, LANES), lambda i, d: (i, 0)),
                pl.BlockSpec((rows, D_MODEL), lambda i, d: (i, 0)),
                pl.BlockSpec((None, 1, D_MODEL), lambda i, d: (_mod_row(i, rows), 0, 5)),
            ],
            out_specs=pl.BlockSpec((rows, D_MODEL), lambda i, d: (i, 0)),
            scratch_shapes=[pltpu.VMEM((2, TOP_K, rows, D_MODEL), F32), pltpu.SemaphoreType.DMA((2,))],
        ),
        out_shape=jax.ShapeDtypeStruct((n_rows, D_MODEL), F32),
        compiler_params=_cparams(("arbitrary",)),
        name="combine",
    )(dest_flat, ys, gates_t, tok, mod3)


def _routing(top_idx, rank, counts, n_rows):
    tm = EXPERT_TM
    n_blocks = -(-(TOP_K * n_rows) // tm) + N_EXPERTS
    counts = counts[:, 0]
    padded = (counts + tm - 1) // tm * tm
    padded_end = jnp.cumsum(padded)
    padded_start = padded_end - padded
    experts = jnp.arange(N_EXPERTS, dtype=jnp.int32)
    start_of = jnp.sum(jnp.where(top_idx[:, :, None] == experts, padded_start, 0), axis=-1)
    dest = (start_of + rank).astype(jnp.int32).reshape(TOP_K * n_rows)
    block_start = jnp.arange(n_blocks, dtype=jnp.int32) * tm
    block_expert = jnp.minimum(jnp.sum(padded_end[None, :] <= block_start[:, None], axis=1), N_EXPERTS - 1)
    n_active = padded_end[-1:] // tm
    group_end = jnp.sum(jnp.where(block_expert[:, None] == experts, padded_end // tm, 0), axis=1)
    follower = jnp.sum(jnp.where(group_end[:, None] == jnp.arange(n_blocks)[None, :], block_expert, 0), axis=1)
    next_expert = jnp.where(group_end < n_active, follower, -1)
    is_expert = block_expert[:, None] == experts
    block_count = jnp.sum(jnp.where(is_expert, counts, 0), axis=1)
    block_first = jnp.sum(jnp.where(is_expert, padded_start, 0), axis=1)
    block_valid = jnp.clip(block_count - (block_start - block_first), 0, tm)
    meta = jnp.concatenate([block_expert, n_active, next_expert, block_valid]).astype(jnp.int32)
    n_pad = jnp.full((1,), n_blocks * tm, jnp.int32)
    pad_meta = jnp.concatenate([padded_start + counts, padded_end[-1:], padded_end, n_pad]).astype(jnp.int32)
    return dest, meta, pad_meta, n_blocks


def _rope_tables():
    rows = SEQ // GRID_W
    row = jnp.broadcast_to(jnp.arange(rows, dtype=F32)[:, None], (rows, GRID_W)).reshape(-1)
    col = jnp.broadcast_to(jnp.arange(GRID_W, dtype=F32)[None, :], (rows, GRID_W)).reshape(-1)
    inv_freq = ROPE_THETA ** (-jnp.arange(ROPE_FREQS, dtype=F32) / ROPE_FREQS)
    ang_r = row[:, None] * inv_freq
    ang_c = col[:, None] * inv_freq
    cos = jnp.concatenate([jnp.cos(ang_r), jnp.cos(ang_r), jnp.cos(ang_c), jnp.cos(ang_c)], axis=-1)
    sin = jnp.concatenate([-jnp.sin(ang_r), jnp.sin(ang_r), -jnp.sin(ang_c), jnp.sin(ang_c)], axis=-1)
    return cos, sin


def kernel(x, c, ctx, c_ctx, ada_w, ada_b, norm1_g, norm2_g, w_in, q_norm_g, k_norm_g, pool_w, pool_scale,
           sgu_norm_g, sgu_w, sgu_b, w_out, router_w, router_b, w_gu, b_gu, w_down, b_down):
    cos, sin = _rope_tables()
    tok = (x.reshape(N_LAT, D_MODEL), ctx.reshape(N_CTX, D_MODEL))
    cc = jnp.zeros((MOD_ROWS, D_MODEL), F32).at[:BATCH].set(c).at[BATCH].set(c_ctx)
    mod = _adaln(cc, ada_w, ada_b)
    tri = jnp.triu(jnp.ones((ROUTER_TM, ROUTER_TM), BF16), k=1)
    for l in range(DEPTH):
        last = l == DEPTH - 1
        n_rows = N_LAT if last else N_TOK
        mod3 = mod[l].reshape(MOD_ROWS, 1, 6 * D_MODEL)
        proj = _in_proj(tok, norm1_g[l], mod3, w_in[l].astype(BF16))
        attn = _latent_attention(proj, cos, sin, q_norm_g[l], k_norm_g[l])
        if not last:
            attn = (attn, _context_attention(proj, q_norm_g[l], k_norm_g[l]))
        sgu_b_full = jnp.broadcast_to(sgu_b[l][:, :, None], (N_SGU_HEADS, CHUNK, LANES))
        mix = _mixers(proj, pool_w[l].astype(BF16), pool_scale[l], sgu_norm_g[l], sgu_w[l].astype(BF16), sgu_b_full)
        tok = _out_proj(attn, mix, w_out[l].astype(BF16), tok, mod3, n_rows)
        rw_hi = router_w[l].astype(BF16)
        rw_lo = (router_w[l] - rw_hi.astype(F32)).astype(BF16)
        rw_pad = (jnp.zeros((D_MODEL, 2 * LANES), BF16).at[:, :N_EXPERTS].set(rw_hi)
                  .at[:, LANES:LANES + N_EXPERTS].set(rw_lo))
        rb_pad = jnp.zeros((1, LANES), F32).at[0, :N_EXPERTS].set(router_b[l])
        hp, top_idx, rank, gates_t, counts = _router(tok, norm2_g[l], mod3, rw_pad, rb_pad, tri, n_rows)
        dest, meta, pad_meta, n_blocks = _routing(top_idx, rank, counts, n_rows)
        n_pad = n_blocks * EXPERT_TM
        xs = _dispatch(dest, pad_meta, hp.reshape(n_rows, PACK_TILES, LANES), n_pad, n_rows)
        act = _expert_gu(meta, xs.reshape(n_pad * PACK_TILES, LANES), w_gu, b_gu, l, n_blocks)
        ys = _expert_down(meta, act, w_down, b_down, l, n_blocks)
        tok = _combine(dest, ys, gates_t, tok, mod3, n_rows)
    return tok.reshape(BATCH, SEQ, D_MODEL)
```

```python
import functools

import jax
import jax.numpy as jnp
from jax import lax
from jax.experimental import pallas as pl
from jax.experimental.pallas import tpu as pltpu

F32 = jnp.float32
BF16 = jnp.bfloat16
U32 = jnp.uint32

D_MODEL = 2048
BATCH = 4
SEQ = 4096
DEPTH = 2
GRID_W = 64
CTX_LEN = 256
HEAD_DIM = 128
ATTN_W = 1024
GQA_GROUP = 4
N_KV_HEADS = 2
KV_W = 256
ROPE_THETA = 10000.0
ROPE_FREQS = 32
ATTN_SCALE = HEAD_DIM ** -0.5
LOG2_E = 1.4426950408889634
POOL_WINDOWS = (2, 4, 8, 16)
POOL_W = 512
SGU_W = 512
N_SGU_HEADS = 4
CHUNK = 128
IN_W = 3072
N_EXPERTS = 32
TOP_K = 4
EXPERT_FF = 2048
SWIGLU_LIMIT = 7.0
SWIGLU_ALPHA = 1.702
EPS = 1e-6

N_LAT = BATCH * SEQ
N_CTX = BATCH * CTX_LEN
N_TOK = N_LAT + N_CTX
MOD_ROWS = 8
LANES = 128
POOL_HALO = 8
MIX_TM = 256
EXPERT_TM = 256
EXPERT_ROW_STEPS = (64, 128, 192, 256)
ROUTER_TM = 512
HALF_D = D_MODEL // 2
PACK_TILES = HALF_D // LANES
VMEM_LIMIT = 56 * 1024 * 1024


def _cparams(sem, vmem=VMEM_LIMIT):
    return pltpu.CompilerParams(dimension_semantics=sem, vmem_limit_bytes=vmem)


def _mod_row(row_tile, tm):
    return jnp.minimum(row_tile * tm // SEQ, BATCH)


def _rms(x, g):
    return x * lax.rsqrt(jnp.mean(x * x, axis=-1, keepdims=True) + EPS) * g


def _sigmoid(x):
    return 1.0 / (1.0 + jnp.exp(-x))


def _gelu(x):
    return 0.5 * x * (1.0 + lax.erf(x * 0.7071067811865476))


def _adaln_kernel(cc_ref, w_ref, b_ref, o_ref):
    cc = cc_ref[...]
    s = (cc * _sigmoid(cc)).astype(BF16)
    o_ref[...] = jnp.dot(s, w_ref[...].astype(BF16), preferred_element_type=F32) + b_ref[...]


def _adaln(cc, ada_w, ada_b):
    tn = 1024
    n = 6 * D_MODEL
    return pl.pallas_call(
        _adaln_kernel,
        grid=(DEPTH, n // tn),
        in_specs=[
            pl.BlockSpec((MOD_ROWS, D_MODEL), lambda l, j: (0, 0)),
            pl.BlockSpec((None, D_MODEL, tn), lambda l, j: (l, 0, j)),
            pl.BlockSpec((None, 1, tn), lambda l, j: (l, 0, j)),
        ],
        out_specs=pl.BlockSpec((None, MOD_ROWS, tn), lambda l, j: (l, 0, j)),
        out_shape=jax.ShapeDtypeStruct((DEPTH, MOD_ROWS, n), F32),
        compiler_params=_cparams(("arbitrary", "arbitrary")),
        name="adaln",
    )(cc, ada_w, ada_b.reshape(DEPTH, 1, n))


def _proj_kernel(*refs, n_first):
    if n_first is None:
        (xa_ref, g_ref, sh_ref, sc_ref, w_ref, o_ref, h_s), xb_ref = refs, None
    else:
        xa_ref, xb_ref, g_ref, sh_ref, sc_ref, w_ref, o_ref, h_s = refs
    i, j = pl.program_id(0), pl.program_id(1)

    def prepare(tile, slot):
        x = xa_ref[...]
        if xb_ref is not None:
            x = jnp.where(tile < n_first, x, xb_ref[...])
        y = _rms(x, g_ref[...])
        h_s[slot] = (y * (1.0 + sc_ref[...]) + sh_ref[...]).astype(BF16)

    @pl.when(jnp.logical_and(i == 0, j == 0))
    def _():
        prepare(0, 0)

    last = j == pl.num_programs(1) - 1

    @pl.when(last)
    def _():
        o_ref[...] = jnp.dot(h_s[i % 2], w_ref[...], preferred_element_type=F32)
        prepare(i + 1, (i + 1) % 2)

    @pl.when(jnp.logical_not(last))
    def _():
        o_ref[...] = jnp.dot(h_s[i % 2], w_ref[...], preferred_element_type=F32)


def _in_proj(src, norm_g, mod3, w_bf):
    tm, tn = 512, 1024
    n_tiles, nj = N_TOK // tm, IN_W // tn

    def tile(i, j):
        return jnp.minimum(i + (j + 1) // nj, n_tiles - 1)

    if isinstance(src, tuple):
        n_first = N_LAT // tm
        x_specs = [pl.BlockSpec((tm, D_MODEL), lambda i, j: (jnp.minimum(tile(i, j), n_first - 1), 0)),
                   pl.BlockSpec((tm, D_MODEL), lambda i, j: (jnp.maximum(tile(i, j) - n_first, 0), 0))]
    else:
        n_first, src = None, (src,)
        x_specs = [pl.BlockSpec((tm, D_MODEL), lambda i, j: (tile(i, j), 0))]
    return pl.pallas_call(
        functools.partial(_proj_kernel, n_first=n_first),
        grid=(n_tiles, nj),
        in_specs=x_specs + [
            pl.BlockSpec((1, D_MODEL), lambda i, j: (0, 0)),
            pl.BlockSpec((None, 1, D_MODEL), lambda i, j: (_mod_row(tile(i, j), tm), 0, 0)),
            pl.BlockSpec((None, 1, D_MODEL), lambda i, j: (_mod_row(tile(i, j), tm), 0, 1)),
            pl.BlockSpec((D_MODEL, tn), lambda i, j: (0, j)),
        ],
        out_specs=pl.BlockSpec((tm, tn), lambda i, j: (i, j)),
        out_shape=jax.ShapeDtypeStruct((N_TOK, IN_W), F32),
        scratch_shapes=[pltpu.VMEM((2, tm, D_MODEL), BF16)],
        compiler_params=_cparams(("arbitrary", "arbitrary")),
        name="in_proj",
    )(*src, norm_g.reshape(1, D_MODEL), mod3, mod3, w_bf)


def _rope(x, cos, sin):
    lane = lax.broadcasted_iota(jnp.int32, x.shape, 1)
    first = (lane % 64) < 32
    partner = jnp.where(first, pltpu.roll(x, 96, 1), pltpu.roll(x, 32, 1))
    return x * cos + partner * sin


def _softmax_pv(q, k, v):
    s = lax.dot_general(q, k, (((1,), (1,)), ((), ())), preferred_element_type=F32) * ATTN_SCALE
    m = jnp.max(s, axis=-1, keepdims=True)
    p = jnp.exp(s - m)
    l = jnp.sum(p, axis=-1, keepdims=True)
    return jnp.dot(p.astype(BF16), v, preferred_element_type=F32) / l


def _lat_attn_kernel(q_ref, kl_ref, vl_ref, kc_ref, vc_ref, cosq_ref, sinq_ref, cosk_ref, sink_ref,
                     qg_ref, kg_ref, o_ref, k_s, v_s, s_s):
    @pl.when(pl.program_id(2) == 0)
    def _():
        k_s[0:CTX_LEN, :] = _rms(kc_ref[...], kg_ref[...]).astype(BF16)
        kl = _rope(_rms(kl_ref[...], kg_ref[...]), cosk_ref[...], sink_ref[...])
        k_s[CTX_LEN:, :] = kl.astype(BF16)
        v_s[0:CTX_LEN, 0:HEAD_DIM] = vc_ref[...].astype(BF16)
        v_s[CTX_LEN:, 0:HEAD_DIM] = vl_ref[...].astype(BF16)
        v_s[:, HEAD_DIM:] = jnp.ones((CTX_LEN + SEQ, LANES), BF16)

    def scores(g):
        lanes = slice(g * HEAD_DIM, (g + 1) * HEAD_DIM)
        q = _rope(_rms(q_ref[:, lanes], qg_ref[...]), cosq_ref[...], sinq_ref[...])
        q = (q * (ATTN_SCALE * LOG2_E)).astype(BF16)
        s_s[g % 2] = lax.dot_general(q, k_s[...], (((1,), (1,)), ((), ())), preferred_element_type=F32)

    scores(0)
    for g in range(GQA_GROUP):
        if g + 1 < GQA_GROUP:
            scores(g + 1)
        s = s_s[g % 2]
        p = jnp.exp2(s - jnp.max(s, axis=-1, keepdims=True))
        ol = jnp.dot(p.astype(BF16), v_s[...], preferred_element_type=F32)
        o = ol[:, 0:HEAD_DIM] / ol[:, HEAD_DIM:]
        o_ref[:, g * HEAD_DIM:(g + 1) * HEAD_DIM] = o.astype(o_ref.dtype)


def _latent_attention(proj, cos, sin, q_g, k_g):
    tq = 256
    nq = SEQ // tq
    qw = GQA_GROUP * HEAD_DIM
    kcol = ATTN_W // HEAD_DIM
    vcol = (ATTN_W + KV_W) // HEAD_DIM
    ctx_blk = N_LAT // CTX_LEN
    return pl.pallas_call(
        _lat_attn_kernel,
        grid=(BATCH, N_KV_HEADS, nq),
        in_specs=[
            pl.BlockSpec((tq, qw), lambda b, h, i: (b * nq + i, h)),
            pl.BlockSpec((SEQ, HEAD_DIM), lambda b, h, i: (b, kcol + h)),
            pl.BlockSpec((SEQ, HEAD_DIM), lambda b, h, i: (b, vcol + h)),
            pl.BlockSpec((CTX_LEN, HEAD_DIM), lambda b, h, i: (ctx_blk + b, kcol + h)),
            pl.BlockSpec((CTX_LEN, HEAD_DIM), lambda b, h, i: (ctx_blk + b, vcol + h)),
            pl.BlockSpec((tq, HEAD_DIM), lambda b, h, i: (i, 0)),
            pl.BlockSpec((tq, HEAD_DIM), lambda b, h, i: (i, 0)),
            pl.BlockSpec((SEQ, HEAD_DIM), lambda b, h, i: (0, 0)),
            pl.BlockSpec((SEQ, HEAD_DIM), lambda b, h, i: (0, 0)),
            pl.BlockSpec((1, HEAD_DIM), lambda b, h, i: (0, 0)),
            pl.BlockSpec((1, HEAD_DIM), lambda b, h, i: (0, 0)),
        ],
        out_specs=pl.BlockSpec((tq, qw), lambda b, h, i: (b * nq + i, h)),
        out_shape=jax.ShapeDtypeStruct((N_LAT, ATTN_W), BF16),
        scratch_shapes=[pltpu.VMEM((CTX_LEN + SEQ, HEAD_DIM), BF16),
                        pltpu.VMEM((CTX_LEN + SEQ, HEAD_DIM + LANES), BF16),
                        pltpu.VMEM((2, tq, CTX_LEN + SEQ), F32)],
        compiler_params=_cparams(("arbitrary", "arbitrary", "arbitrary")),
        name="latent_attention",
    )(proj, proj, proj, proj, proj, cos, sin, cos, sin, q_g.reshape(1, HEAD_DIM), k_g.reshape(1, HEAD_DIM))


def _ctx_attn_kernel(q_ref, k_ref, v_ref, qg_ref, kg_ref, o_ref):
    k = _rms(k_ref[...], kg_ref[...]).astype(BF16)
    v = v_ref[...].astype(BF16)
    for g in range(GQA_GROUP):
        lanes = slice(g * HEAD_DIM, (g + 1) * HEAD_DIM)
        q = _rms(q_ref[:, lanes], qg_ref[...]).astype(BF16)
        o_ref[:, lanes] = _softmax_pv(q, k, v).astype(o_ref.dtype)


def _context_attention(proj, q_g, k_g):
    qw = GQA_GROUP * HEAD_DIM
    kcol = ATTN_W // HEAD_DIM
    vcol = (ATTN_W + KV_W) // HEAD_DIM
    ctx_blk = N_LAT // CTX_LEN
    return pl.pallas_call(
        _ctx_attn_kernel,
        grid=(BATCH, N_KV_HEADS),
        in_specs=[
            pl.BlockSpec((CTX_LEN, qw), lambda b, h: (ctx_blk + b, h)),
            pl.BlockSpec((CTX_LEN, HEAD_DIM), lambda b, h: (ctx_blk + b, kcol + h)),
            pl.BlockSpec((CTX_LEN, HEAD_DIM), lambda b, h: (ctx_blk + b, vcol + h)),
            pl.BlockSpec((1, HEAD_DIM), lambda b, h: (0, 0)),
            pl.BlockSpec((1, HEAD_DIM), lambda b, h: (0, 0)),
        ],
        out_specs=pl.BlockSpec((CTX_LEN, qw), lambda b, h: (b, h)),
        out_shape=jax.ShapeDtypeStruct((N_CTX, ATTN_W), BF16),
        compiler_params=_cparams(("arbitrary", "arbitrary")),
        name="context_attention",
    )(proj, proj, proj, q_g.reshape(1, HEAD_DIM), k_g.reshape(1, HEAD_DIM))


def _mixer_kernel(pin_ref, prev_ref, next_ref, su_ref, sv_ref, pw_ref, ps_ref, sg_ref, sw_ref, sb_ref,
                  o_ref, pad_ref):
    tm = MIX_TM
    i = pl.program_id(0)
    is_lat = i < N_LAT // tm
    pos0 = jnp.where(is_lat, (i % (SEQ // tm)) * tm, 0)
    seq_len = jnp.where(is_lat, SEQ, CTX_LEN)
    pad_ref[0:POOL_HALO, :] = jnp.where(pos0 == 0, 0.0, prev_ref[...])
    pad_ref[POOL_HALO:POOL_HALO + tm, :] = pin_ref[...]
    pad_ref[POOL_HALO + tm:, :] = jnp.where(pos0 + tm == seq_len, 0.0, next_ref[...])
    t = pos0 + lax.broadcasted_iota(jnp.int32, (tm, LANES), 0)
    for gi, w in enumerate(POOL_WINDOWS):
        lanes = slice(gi * LANES, (gi + 1) * LANES)
        acc = pad_ref[POOL_HALO - w // 2:POOL_HALO - w // 2 + tm, lanes]
        for d in range(-w // 2 + 1, w // 2):
            acc = acc + pad_ref[POOL_HALO + d:POOL_HALO + d + tm, lanes]
        cnt = (jnp.minimum(t + w // 2, seq_len) - jnp.maximum(t - w // 2, 0)).astype(F32)
        mixed = acc / cnt - pin_ref[:, lanes]
        y = jnp.dot(mixed.astype(BF16), pw_ref[gi], preferred_element_type=F32) * ps_ref[:, lanes]
        o_ref[:, lanes] = y.astype(o_ref.dtype)

    for h in range(N_SGU_HEADS):
        lanes = slice(h * LANES, (h + 1) * LANES)
        gu = _gelu(su_ref[:, lanes])
        vh = _rms(_gelu(sv_ref[:, lanes]), sg_ref[h:h + 1, :]).astype(BF16)
        for n in range(tm // CHUNK):
            rows = slice(n * CHUNK, (n + 1) * CHUNK)
            mixed = jnp.dot(sw_ref[h], vh[rows], preferred_element_type=F32) + sb_ref[h]
            o_ref[rows, POOL_W + h * LANES:POOL_W + (h + 1) * LANES] = (gu[rows] * mixed).astype(o_ref.dtype)


def _mixers(proj, pool_w_bf, pool_scale, sgu_norm_g, sgu_w_bf, sgu_b_full):
    tm = MIX_TM
    per_tile = tm // POOL_HALO
    last_halo = N_TOK // POOL_HALO - 1
    pcol = (ATTN_W + 2 * KV_W) // POOL_W
    return pl.pallas_call(
        _mixer_kernel,
        grid=(N_TOK // tm,),
        in_specs=[
            pl.BlockSpec((tm, POOL_W), lambda i: (i, pcol)),
            pl.BlockSpec((POOL_HALO, POOL_W), lambda i: (jnp.maximum(i * per_tile - 1, 0), pcol)),
            pl.BlockSpec((POOL_HALO, POOL_W), lambda i: (jnp.minimum((i + 1) * per_tile, last_halo), pcol)),
            pl.BlockSpec((tm, SGU_W), lambda i: (i, pcol + 1)),
            pl.BlockSpec((tm, SGU_W), lambda i: (i, pcol + 2)),
            pl.BlockSpec((len(POOL_WINDOWS), LANES, LANES), lambda i: (0, 0, 0)),
            pl.BlockSpec((1, POOL_W), lambda i: (0, 0)),
            pl.BlockSpec((N_SGU_HEADS, LANES), lambda i: (0, 0)),
            pl.BlockSpec((N_SGU_HEADS, CHUNK, CHUNK), lambda i: (0, 0, 0)),
            pl.BlockSpec((N_SGU_HEADS, CHUNK, LANES), lambda i: (0, 0, 0)),
        ],
        out_specs=pl.BlockSpec((tm, POOL_W + SGU_W), lambda i: (i, 0)),
        out_shape=jax.ShapeDtypeStruct((N_TOK, POOL_W + SGU_W), BF16),
        scratch_shapes=[pltpu.VMEM((tm + 2 * POOL_HALO, POOL_W), F32)],
        compiler_params=_cparams(("arbitrary",)),
        name="mixers",
    )(proj, proj, proj, proj, proj, pool_w_bf, pool_scale.reshape(1, POOL_W), sgu_norm_g, sgu_w_bf, sgu_b_full)


def _outproj_kernel(*refs, n_first):
    if n_first is None:
        a_ref, m_ref, w1_ref, w2_ref, x_ref, g_ref, o_ref = refs
        a, x = a_ref[...], x_ref[...]
    else:
        a_ref, ac_ref, m_ref, w1_ref, w2_ref, x_ref, xc_ref, g_ref, o_ref = refs
        is_first = pl.program_id(1) < n_first
        a = jnp.where(is_first, a_ref[...], ac_ref[...])
        x = jnp.where(is_first, x_ref[...], xc_ref[...])
    y = jnp.dot(a, w1_ref[...], preferred_element_type=F32)
    y = y + jnp.dot(m_ref[...], w2_ref[...], preferred_element_type=F32)
    o_ref[...] = x + g_ref[...] * y


def _out_proj(attn, mix, w_bf, src, mod3, n_rows):
    tm, tn = 512, 1024
    nj = D_MODEL // tn
    assert isinstance(attn, tuple) == isinstance(src, tuple)
    if isinstance(src, tuple):
        n_first = N_LAT // tm

        def first(i):
            return jnp.minimum(i, n_first - 1)

        def second(i):
            return jnp.maximum(i - n_first, 0)

        attn_specs = [pl.BlockSpec((tm, ATTN_W), lambda j, i: (first(i), 0)),
                      pl.BlockSpec((tm, ATTN_W), lambda j, i: (second(i), 0))]
        x_specs = [pl.BlockSpec((tm, tn), lambda j, i: (first(i), j)),
                   pl.BlockSpec((tm, tn), lambda j, i: (second(i), j))]
    else:
        n_first, attn, src = None, (attn,), (src,)
        attn_specs = [pl.BlockSpec((tm, ATTN_W), lambda j, i: (i, 0))]
        x_specs = [pl.BlockSpec((tm, tn), lambda j, i: (i, j))]
    return pl.pallas_call(
        functools.partial(_outproj_kernel, n_first=n_first),
        grid=(nj, n_rows // tm),
        in_specs=attn_specs + [
            pl.BlockSpec((tm, POOL_W + SGU_W), lambda j, i: (i, 0)),
            pl.BlockSpec((ATTN_W, tn), lambda j, i: (0, j)),
            pl.BlockSpec((POOL_W + SGU_W, tn), lambda j, i: (1, j)),
        ] + x_specs + [
            pl.BlockSpec((None, 1, tn), lambda j, i: (_mod_row(i, tm), 0, 2 * nj + j)),
        ],
        out_specs=pl.BlockSpec((tm, tn), lambda j, i: (i, j)),
        out_shape=jax.ShapeDtypeStruct((n_rows, D_MODEL), F32),
        compiler_params=_cparams(("arbitrary", "arbitrary")),
        name="out_proj",
    )(*attn, mix, w_bf, w_bf, *src, mod3)


def _router_kernel(x_ref, g_ref, sh_ref, sc_ref, rw_ref, rb_ref, tri_ref,
                   hp_ref, idx_ref, rank_ref, gate_ref, cnt_ref, run_s):
    @pl.when(pl.program_id(0) == 0)
    def _():
        run_s[...] = jnp.zeros_like(run_s)

    h = _rms(x_ref[...], g_ref[...]) * (1.0 + sc_ref[...]) + sh_ref[...]
    h_hi = h.astype(BF16)
    hb = h_hi.astype(F32)
    hi = lax.bitcast_convert_type(hb[:, :HALF_D], U32)
    lo = lax.bitcast_convert_type(hb[:, HALF_D:], U32)
    packed = hi | (lo >> 16)
    for j in range(PACK_TILES):
        hp_ref[pl.ds(j, packed.shape[0], stride=PACK_TILES), :] = packed[:, j * LANES:(j + 1) * LANES]

    h_lo = (h - hb).astype(BF16)
    both = jnp.dot(h_hi, rw_ref[...], preferred_element_type=F32)
    tail = jnp.dot(h_lo, rw_ref[:, 0:LANES], preferred_element_type=F32)
    logits = (both[:, 0:LANES] + both[:, LANES:]) + tail
    lt = (logits + rb_ref[...]).T[0:N_EXPERTS, :]
    expert = lax.broadcasted_iota(jnp.int32, lt.shape, 0).astype(F32)
    vals, idxs = [], []
    for _ in range(TOP_K):
        m = jnp.max(lt, axis=0, keepdims=True)
        idx = jnp.min(jnp.where(lt == m, expert, float(N_EXPERTS)), axis=0, keepdims=True)
        vals.append(m)
        idxs.append(idx)
        lt = jnp.where(expert == idx, -jnp.inf, lt)

    e = [jnp.exp(v - vals[0]) for v in vals]
    den = e[0] + e[1] + e[2] + e[3]
    slot = lax.broadcasted_iota(jnp.int32, (LANES, lt.shape[1]), 0)
    gates = jnp.zeros((LANES, lt.shape[1]), F32)
    for k in range(TOP_K):
        gates = jnp.where(slot == k, e[k] / den, gates)
    gate_ref[...] = gates.T

    base = run_s[...]
    for k in range(TOP_K):
        onehot = jnp.where(expert == idxs[k], 1.0, 0.0)
        before = jnp.dot(onehot.astype(BF16), tri_ref[...], preferred_element_type=F32)
        rank = jnp.sum(onehot * (before + base[:, 0:1]), axis=0, keepdims=True)
        idx_ref[k:k + 1, :] = idxs[k].astype(jnp.int32)
        rank_ref[k:k + 1, :] = rank.astype(jnp.int32)
        base = base + jnp.sum(onehot, axis=1, keepdims=True)
    run_s[...] = base
    cnt_ref[...] = base.astype(jnp.int32)


def _router(tok, norm_g, mod3, rw_pad, rb_pad, tri, n_rows):
    tm = ROUTER_TM
    return pl.pallas_call(
        _router_kernel,
        grid=(n_rows // tm,),
        in_specs=[
            pl.BlockSpec((tm, D_MODEL), lambda i: (i, 0)),
            pl.BlockSpec((1, D_MODEL), lambda i: (0, 0)),
            pl.BlockSpec((None, 1, D_MODEL), lambda i: (_mod_row(i, tm), 0, 3)),
            pl.BlockSpec((None, 1, D_MODEL), lambda i: (_mod_row(i, tm), 0, 4)),
            pl.BlockSpec((D_MODEL, 2 * LANES), lambda i: (0, 0)),
            pl.BlockSpec((1, LANES), lambda i: (0, 0)),
            pl.BlockSpec((tm, tm), lambda i: (0, 0)),
        ],
        out_specs=[
            pl.BlockSpec((tm * PACK_TILES, LANES), lambda i: (i, 0)),
            pl.BlockSpec((TOP_K, tm), lambda i: (0, i)),
            pl.BlockSpec((TOP_K, tm), lambda i: (0, i)),
            pl.BlockSpec((tm, LANES), lambda i: (i, 0)),
            pl.BlockSpec((N_EXPERTS, LANES), lambda i: (0, 0)),
        ],
        out_shape=[
            jax.ShapeDtypeStruct((n_rows * PACK_TILES, LANES), U32),
            jax.ShapeDtypeStruct((TOP_K, n_rows), jnp.int32),
            jax.ShapeDtypeStruct((TOP_K, n_rows), jnp.int32),
            jax.ShapeDtypeStruct((n_rows, LANES), F32),
            jax.ShapeDtypeStruct((N_EXPERTS, LANES), jnp.int32),
        ],
        scratch_shapes=[pltpu.VMEM((N_EXPERTS, LANES), F32)],
        compiler_params=_cparams(("arbitrary",)),
        name="router",
    )(tok, norm_g.reshape(1, D_MODEL), mod3, mod3, rw_pad, rb_pad, tri)


def _dispatch_kernel(dest_ref, pad_ref, hp_ref, xs_hbm, sem, pad_sem, *, rows, n_rows):
    i = pl.program_id(0)

    @pl.when(i == 0)
    def _():
        def per_expert(e, carry):
            lo, hi = pad_ref[e], pad_ref[N_EXPERTS + 1 + e]

            def start(p, c):
                pltpu.make_async_copy(hp_ref.at[0], xs_hbm.at[p], pad_sem).start()
                return c

            def wait(p, c):
                pltpu.make_async_copy(hp_ref.at[0], xs_hbm.at[p], pad_sem).wait()
                return c

            lax.fori_loop(lo, hi, start, 0)
            lax.fori_loop(lo, hi, wait, 0)
            return carry

        lax.fori_loop(0, N_EXPERTS + 1, per_expert, 0)

    base = i * rows

    def issue(r2, carry):
        for u in range(2):
            r = 2 * r2 + u
            for k in range(TOP_K):
                d = dest_ref[k * n_rows + base + r]
                pltpu.make_async_copy(hp_ref.at[r], xs_hbm.at[d], sem).start(priority=k % 2)
        return carry

    lax.fori_loop(0, rows // 2, issue, 0)
    for k in range(TOP_K):
        pltpu.make_async_copy(hp_ref, xs_hbm.at[pl.ds(0, rows)], sem).wait()


def _dispatch(dest_flat, pad_meta, hp, n_pad, n_rows):
    rows = 1024
    return pl.pallas_call(
        functools.partial(_dispatch_kernel, rows=rows, n_rows=n_rows),
        grid_spec=pltpu.PrefetchScalarGridSpec(
            num_scalar_prefetch=2,
            grid=(n_rows // rows,),
            in_specs=[pl.BlockSpec((rows, PACK_TILES, LANES), lambda i, d, p: (i, 0, 0))],
            out_specs=pl.BlockSpec(memory_space=pl.ANY),
            scratch_shapes=[pltpu.SemaphoreType.DMA(()), pltpu.SemaphoreType.DMA(())],
        ),
        out_shape=jax.ShapeDtypeStruct((n_pad, PACK_TILES, LANES), U32),
        compiler_params=_cparams(("arbitrary",)),
        name="dispatch",
    )(dest_flat, pad_meta, hp)


def _is_new_expert(meta_ref, i):
    prev = meta_ref[jnp.maximum(i - 1, 0)]
    return jnp.logical_or(i == 0, meta_ref[i] != prev)


def _cast_rows(src_ref, dst_ref, chunk=256):
    def body(c, carry):
        rows = pl.ds(pl.multiple_of(c * chunk, chunk), chunk)
        dst_ref[rows, :] = src_ref[rows, :].astype(dst_ref.dtype)
        return carry

    lax.fori_loop(0, src_ref.shape[0] // chunk, body, 0)


def _for_valid_rows(valid, fn):
    lo = 0
    for n in EXPERT_ROW_STEPS:
        @pl.when(jnp.logical_and(valid > lo, valid <= n))
        def _(n=n):
            fn(n)

        lo = n


def _unpack_rows(x_ref, n):
    xp = jnp.concatenate([x_ref[pl.ds(j, n, stride=PACK_TILES), :] for j in range(PACK_TILES)], axis=1)
    hi = lax.bitcast_convert_type(xp & jnp.uint32(0xFFFF0000), F32).astype(BF16)
    lo = lax.bitcast_convert_type(xp << 16, F32).astype(BF16)
    return jnp.concatenate([hi, lo], axis=1)


def _expert_gu_kernel(meta_ref, x_ref, w_hbm, bg_ref, bu_ref, o_ref, wg_f, wu_f, wg_s, wu_s, sems,
                      *, n_blocks, layer):
    j, i = pl.program_id(0), pl.program_id(1)
    tn = wg_f.shape[1]
    nj = pl.num_programs(0)

    def fetch(e, jj):
        col = pl.multiple_of(jj * tn, tn)
        return (pltpu.make_async_copy(w_hbm.at[layer, e, :, pl.ds(col, tn)], wg_f, sems.at[0]),
                pltpu.make_async_copy(w_hbm.at[layer, e, :, pl.ds(EXPERT_FF + col, tn)], wu_f, sems.at[1]))

    @pl.when(jnp.logical_and(j == 0, i == 0))
    def _():
        for cp in fetch(meta_ref[0], 0):
            cp.start()

    active = i < meta_ref[n_blocks]

    @pl.when(jnp.logical_and(active, _is_new_expert(meta_ref, i)))
    def _():
        for cp in fetch(meta_ref[i], j):
            cp.wait()
        _cast_rows(wg_f, wg_s)
        _cast_rows(wu_f, wu_s)
        nxt = meta_ref[n_blocks + 1 + i]

        @pl.when(nxt >= 0)
        def _():
            for cp in fetch(nxt, j):
                cp.start()

        @pl.when(jnp.logical_and(nxt < 0, j + 1 < nj))
        def _():
            for cp in fetch(meta_ref[0], j + 1):
                cp.start()

    def compute(n):
        x = _unpack_rows(x_ref, n)
        g = jnp.dot(x, wg_s[...], preferred_element_type=F32) + bg_ref[...]
        u = jnp.dot(x, wu_s[...], preferred_element_type=F32) + bu_ref[...]
        g = jnp.minimum(g, SWIGLU_LIMIT)
        u = jnp.clip(u, -SWIGLU_LIMIT, SWIGLU_LIMIT)
        o_ref[0:n, :] = ((u + 1.0) * (g * _sigmoid(g * SWIGLU_ALPHA))).astype(o_ref.dtype)
        if n < o_ref.shape[0]:
            o_ref[n:, :] = jnp.zeros((o_ref.shape[0] - n, o_ref.shape[1]), o_ref.dtype)

    _for_valid_rows(meta_ref[2 * n_blocks + 1 + i], compute)

    @pl.when(jnp.logical_not(active))
    def _():
        o_ref[...] = jnp.zeros_like(o_ref)


def _active_block(i, m, n_blocks):
    return jnp.minimum(i, m[n_blocks] - 1)


def _expert_gu(meta, xs, w_gu, b_gu, layer, n_blocks):
    tm, tn = EXPERT_TM, 1024
    nj = EXPERT_FF // tn
    b3 = b_gu.reshape(DEPTH, N_EXPERTS, 1, 2 * EXPERT_FF)
    return pl.pallas_call(
        functools.partial(_expert_gu_kernel, n_blocks=n_blocks, layer=layer),
        grid_spec=pltpu.PrefetchScalarGridSpec(
            num_scalar_prefetch=1,
            grid=(nj, n_blocks),
            in_specs=[
                pl.BlockSpec((tm * PACK_TILES, LANES), lambda j, i, m: (_active_block(i, m, n_blocks), 0)),
                pl.BlockSpec(memory_space=pl.ANY),
                pl.BlockSpec((None, None, 1, tn), lambda j, i, m: (layer, m[i], 0, j)),
                pl.BlockSpec((None, None, 1, tn), lambda j, i, m: (layer, m[i], 0, nj + j)),
            ],
            out_specs=pl.BlockSpec((tm, tn), lambda j, i, m: (i, j)),
            scratch_shapes=[pltpu.VMEM((D_MODEL, tn), F32), pltpu.VMEM((D_MODEL, tn), F32),
                            pltpu.VMEM((D_MODEL, tn), BF16), pltpu.VMEM((D_MODEL, tn), BF16),
                            pltpu.SemaphoreType.DMA((2,))],
        ),
        out_shape=jax.ShapeDtypeStruct((n_blocks * tm, EXPERT_FF), BF16),
        compiler_params=_cparams(("arbitrary", "arbitrary")),
        name="expert_gu",
    )(meta, xs, w_gu, b3, b3)


def _expert_down_kernel(meta_ref, a_ref, w_hbm, b_ref, o_ref, w_f, w_s, sem, *, n_blocks, layer):
    i = pl.program_id(0)

    def fetch(e):
        return pltpu.make_async_copy(w_hbm.at[layer, e], w_f, sem)

    @pl.when(i == 0)
    def _():
        fetch(meta_ref[0]).start()

    active = i < meta_ref[n_blocks]

    @pl.when(jnp.logical_and(active, _is_new_expert(meta_ref, i)))
    def _():
        fetch(meta_ref[i]).wait()
        _cast_rows(w_f, w_s)
        nxt = meta_ref[n_blocks + 1 + i]

        @pl.when(nxt >= 0)
        def _():
            fetch(nxt).start()

    def compute(n):
        o_ref[0:n, :] = jnp.dot(a_ref[0:n, :], w_s[...], preferred_element_type=F32) + b_ref[...]
        if n < o_ref.shape[0]:
            o_ref[n:, :] = jnp.zeros((o_ref.shape[0] - n, o_ref.shape[1]), o_ref.dtype)

    _for_valid_rows(meta_ref[2 * n_blocks + 1 + i], compute)

    @pl.when(jnp.logical_not(active))
    def _():
        o_ref[...] = jnp.zeros_like(o_ref)


def _expert_down(meta, act, w_down, b_down, layer, n_blocks):
    tm = EXPERT_TM
    b3 = b_down.reshape(DEPTH, N_EXPERTS, 1, D_MODEL)
    return pl.pallas_call(
        functools.partial(_expert_down_kernel, n_blocks=n_blocks, layer=layer),
        grid_spec=pltpu.PrefetchScalarGridSpec(
            num_scalar_prefetch=1,
            grid=(n_blocks,),
            in_specs=[
                pl.BlockSpec((tm, EXPERT_FF), lambda i, m: (_active_block(i, m, n_blocks), 0)),
                pl.BlockSpec(memory_space=pl.ANY),
                pl.BlockSpec((None, None, 1, D_MODEL), lambda i, m: (layer, m[i], 0, 0)),
            ],
            out_specs=pl.BlockSpec((tm, D_MODEL), lambda i, m: (i, 0)),
            scratch_shapes=[pltpu.VMEM((EXPERT_FF, D_MODEL), F32), pltpu.VMEM((EXPERT_FF, D_MODEL), BF16),
                            pltpu.SemaphoreType.DMA(())],
        ),
        out_shape=jax.ShapeDtypeStruct((n_blocks * tm, D_MODEL), F32),
        compiler_params=_cparams(("arbitrary",)),
        name="expert_down",
    )(meta, act, w_down, b3)


def _combine_kernel(dest_ref, y_hbm, gt_ref, x_ref, g_ref, o_ref, buf, sems, *, rows, n_rows):
    i = pl.program_id(0)
    n_steps = pl.num_programs(0)

    def issue(step, slot):
        base = step * rows

        def body(r8, carry):
            r0 = pl.multiple_of(r8 * 8, 8)
            for u in range(8):
                for k in range(TOP_K):
                    d = dest_ref[k * n_rows + base + r0 + u]
                    pltpu.make_async_copy(y_hbm.at[pl.ds(d, 1), :],
                                          buf.at[slot, k, pl.ds(r0, 8), :].at[pl.ds(u, 1), :],
                                          sems.at[slot]).start(priority=k % 2)
            return carry

        lax.fori_loop(0, rows // 8, body, 0)

    @pl.when(i == 0)
    def _():
        issue(0, 0)

    @pl.when(i + 1 < n_steps)
    def _():
        issue(i + 1, (i + 1) % 2)

    slot = i % 2
    for k in range(TOP_K):
        pltpu.make_async_copy(y_hbm.at[pl.ds(0, rows), :], buf.at[slot, k], sems.at[slot]).wait()
    gt = gt_ref[...]
    moe = gt[:, 0:1] * buf[slot, 0]
    for k in range(1, TOP_K):
        moe = moe + gt[:, k:k + 1] * buf[slot, k]
    o_ref[...] = x_ref[...] + g_ref[...] * moe


def _combine(dest_flat, ys, gates_t, tok, mod3, n_rows):
    rows = 128
    return pl.pallas_call(
        functools.partial(_combine_kernel, rows=rows, n_rows=n_rows),
        grid_spec=pltpu.PrefetchScalarGridSpec(
            num_scalar_prefetch=1,
            grid=(n_rows // rows,),
            in_specs=[
                pl.BlockSpec(memory_space=pl.ANY),
                pl.BlockSpec((rows, LANES), lambda i, d: (i, 0)),
                pl.BlockSpec((rows, D_MODEL), lambda i, d: (i, 0)),
                pl.BlockSpec((None, 1, D_MODEL), lambda i, d: (_mod_row(i, rows), 0, 5)),
            ],
            out_specs=pl.BlockSpec((rows, D_MODEL), lambda i, d: (i, 0)),
            scratch_shapes=[pltpu.VMEM((2, TOP_K, rows, D_MODEL), F32), pltpu.SemaphoreType.DMA((2,))],
        ),
        out_shape=jax.ShapeDtypeStruct((n_rows, D_MODEL), F32),
        compiler_params=_cparams(("arbitrary",)),
        name="combine",
    )(dest_flat, ys, gates_t, tok, mod3)


def _routing(top_idx, rank, counts, n_rows):
    tm = EXPERT_TM
    n_blocks = -(-(TOP_K * n_rows) // tm) + N_EXPERTS
    counts = counts[:, 0]
    padded = (counts + tm - 1) // tm * tm
    padded_end = jnp.cumsum(padded)
    padded_start = padded_end - padded
    experts = jnp.arange(N_EXPERTS, dtype=jnp.int32)
    start_of = jnp.sum(jnp.where(top_idx[:, :, None] == experts, padded_start, 0), axis=-1)
    dest = (start_of + rank).astype(jnp.int32).reshape(TOP_K * n_rows)
    block_start = jnp.arange(n_blocks, dtype=jnp.int32) * tm
    block_expert = jnp.minimum(jnp.sum(padded_end[None, :] <= block_start[:, None], axis=1), N_EXPERTS - 1)
    n_active = padded_end[-1:] // tm
    group_end = jnp.sum(jnp.where(block_expert[:, None] == experts, padded_end // tm, 0), axis=1)
    follower = jnp.sum(jnp.where(group_end[:, None] == jnp.arange(n_blocks)[None, :], block_expert, 0), axis=1)
    next_expert = jnp.where(group_end < n_active, follower, -1)
    is_expert = block_expert[:, None] == experts
    block_count = jnp.sum(jnp.where(is_expert, counts, 0), axis=1)
    block_first = jnp.sum(jnp.where(is_expert, padded_start, 0), axis=1)
    block_valid = jnp.clip(block_count - (block_start - block_first), 0, tm)
    meta = jnp.concatenate([block_expert, n_active, next_expert, block_valid]).astype(jnp.int32)
    n_pad = jnp.full((1,), n_blocks * tm, jnp.int32)
    pad_meta = jnp.concatenate([padded_start + counts, padded_end[-1:], padded_end, n_pad]).astype(jnp.int32)
    return dest, meta, pad_meta, n_blocks


def _rope_tables():
    rows = SEQ // GRID_W
    row = jnp.broadcast_to(jnp.arange(rows, dtype=F32)[:, None], (rows, GRID_W)).reshape(-1)
    col = jnp.broadcast_to(jnp.arange(GRID_W, dtype=F32)[None, :], (rows, GRID_W)).reshape(-1)
    inv_freq = ROPE_THETA ** (-jnp.arange(ROPE_FREQS, dtype=F32) / ROPE_FREQS)
    ang_r = row[:, None] * inv_freq
    ang_c = col[:, None] * inv_freq
    cos = jnp.concatenate([jnp.cos(ang_r), jnp.cos(ang_r), jnp.cos(ang_c), jnp.cos(ang_c)], axis=-1)
    sin = jnp.concatenate([-jnp.sin(ang_r), jnp.sin(ang_r), -jnp.sin(ang_c), jnp.sin(ang_c)], axis=-1)
    return cos, sin


def kernel(x, c, ctx, c_ctx, ada_w, ada_b, norm1_g, norm2_g, w_in, q_norm_g, k_norm_g, pool_w, pool_scale,
           sgu_norm_g, sgu_w, sgu_b, w_out, router_w, router_b, w_gu, b_gu, w_down, b_down):
    cos, sin = _rope_tables()
    tok = (x.reshape(N_LAT, D_MODEL), ctx.reshape(N_CTX, D_MODEL))
    cc = jnp.zeros((MOD_ROWS, D_MODEL), F32).at[:BATCH].set(c).at[BATCH].set(c_ctx)
    mod = _adaln(cc, ada_w, ada_b)
    tri = jnp.triu(jnp.ones((ROUTER_TM, ROUTER_TM), BF16), k=1)
    for l in range(DEPTH):
        last = l == DEPTH - 1
        n_rows = N_LAT if last else N_TOK
        mod3 = mod[l].reshape(MOD_ROWS, 1, 6 * D_MODEL)
        proj = _in_proj(tok, norm1_g[l], mod3, w_in[l].astype(BF16))
        attn = _latent_attention(proj, cos, sin, q_norm_g[l], k_norm_g[l])
        if not last:
            attn = (attn, _context_attention(proj, q_norm_g[l], k_norm_g[l]))
        sgu_b_full = jnp.broadcast_to(sgu_b[l][:, :, None], (N_SGU_HEADS, CHUNK, LANES))
        mix = _mixers(proj, pool_w[l].astype(BF16), pool_scale[l], sgu_norm_g[l], sgu_w[l].astype(BF16), sgu_b_full)
        tok = _out_proj(attn, mix, w_out[l].astype(BF16), tok, mod3, n_rows)
        rw_hi = router_w[l].astype(BF16)
        rw_lo = (router_w[l] - rw_hi.astype(F32)).astype(BF16)
        rw_pad = (jnp.zeros((D_MODEL, 2 * LANES), BF16).at[:, :N_EXPERTS].set(rw_hi)
                  .at[:, LANES:LANES + N_EXPERTS].set(rw_lo))
        rb_pad = jnp.zeros((1, LANES), F32).at[0, :N_EXPERTS].set(router_b[l])
        hp, top_idx, rank, gates_t, counts = _router(tok, norm2_g[l], mod3, rw_pad, rb_pad, tri, n_rows)
        dest, meta, pad_meta, n_blocks = _routing(top_idx, rank, counts, n_rows)
        n_pad = n_blocks * EXPERT_TM
        xs = _dispatch(dest, pad_meta, hp.reshape(n_rows, PACK_TILES, LANES), n_pad, n_rows)
        act = _expert_gu(meta, xs.reshape(n_pad * PACK_TILES, LANES), w_gu, b_gu, l, n_blocks)
        ys = _expert_down(meta, act, w_down, b_down, l, n_blocks)
        tok = _combine(dest, ys, gates_t, tok, mod3, n_rows)
    return tok.reshape(BATCH, SEQ, D_MODEL)
```

```python
import functools

import jax
import jax.numpy as jnp
from jax import lax
from jax.experimental import pallas as pl
from jax.experimental.pallas import tpu as pltpu

F32 = jnp.float32
BF16 = jnp.bfloat16
U32 = jnp.uint32

D_MODEL = 2048
BATCH = 4
SEQ = 4096
DEPTH = 2
GRID_W = 64
CTX_LEN = 256
HEAD_DIM = 128
ATTN_W = 1024
GQA_GROUP = 4
N_KV_HEADS = 2
KV_W = 256
ROPE_THETA = 10000.0
ROPE_FREQS = 32
ATTN_SCALE = HEAD_DIM ** -0.5
LOG2_E = 1.4426950408889634
POOL_WINDOWS = (2, 4, 8, 16)
POOL_W = 512
SGU_W = 512
N_SGU_HEADS = 4
CHUNK = 128
IN_W = 3072
N_EXPERTS = 32
TOP_K = 4
EXPERT_FF = 2048
SWIGLU_LIMIT = 7.0
SWIGLU_ALPHA = 1.702
EPS = 1e-6

N_LAT = BATCH * SEQ
N_CTX = BATCH * CTX_LEN
N_TOK = N_LAT + N_CTX
MOD_ROWS = 8
LANES = 128
POOL_HALO = 8
MIX_TM = 256
EXPERT_TM = 256
EXPERT_ROW_STEPS = (64, 128, 192, 256)
ROUTER_TM = 512
DOWN_CAST_TILES = 4
GU_CAST_TILES = 4
HALF_D = D_MODEL // 2
PACK_TILES = HALF_D // LANES
VMEM_LIMIT = 56 * 1024 * 1024


def _cparams(sem, vmem=VMEM_LIMIT):
    return pltpu.CompilerParams(dimension_semantics=sem, vmem_limit_bytes=vmem)


def _mod_row(row_tile, tm):
    return jnp.minimum(row_tile * tm // SEQ, BATCH)


def _rms(x, g):
    return x * lax.rsqrt(jnp.mean(x * x, axis=-1, keepdims=True) + EPS) * g


def _sigmoid(x):
    return 1.0 / (1.0 + jnp.exp(-x))


def _gelu(x):
    return 0.5 * x * (1.0 + lax.erf(x * 0.7071067811865476))


def _adaln_kernel(cc_ref, w_ref, b_ref, o_ref):
    cc = cc_ref[...]
    s = (cc * _sigmoid(cc)).astype(BF16)
    o_ref[...] = jnp.dot(s, w_ref[...].astype(BF16), preferred_element_type=F32) + b_ref[...]


def _adaln(cc, ada_w, ada_b):
    tn = 1024
    n = 6 * D_MODEL
    return pl.pallas_call(
        _adaln_kernel,
        grid=(DEPTH, n // tn),
        in_specs=[
            pl.BlockSpec((MOD_ROWS, D_MODEL), lambda l, j: (0, 0)),
            pl.BlockSpec((None, D_MODEL, tn), lambda l, j: (l, 0, j)),
            pl.BlockSpec((None, 1, tn), lambda l, j: (l, 0, j)),
        ],
        out_specs=pl.BlockSpec((None, MOD_ROWS, tn), lambda l, j: (l, 0, j)),
        out_shape=jax.ShapeDtypeStruct((DEPTH, MOD_ROWS, n), F32),
        compiler_params=_cparams(("arbitrary", "arbitrary")),
        name="adaln",
    )(cc, ada_w, ada_b.reshape(DEPTH, 1, n))


def _proj_kernel(*refs, n_first):
    if n_first is None:
        (xa_ref, g_ref, sh_ref, sc_ref, w_ref, o_ref, h_s), xb_ref = refs, None
    else:
        xa_ref, xb_ref, g_ref, sh_ref, sc_ref, w_ref, o_ref, h_s = refs
    i, j = pl.program_id(0), pl.program_id(1)

    def prepare(tile, slot):
        x = xa_ref[...]
        if xb_ref is not None:
            x = jnp.where(tile < n_first, x, xb_ref[...])
        y = _rms(x, g_ref[...])
        h_s[slot] = (y * (1.0 + sc_ref[...]) + sh_ref[...]).astype(BF16)

    @pl.when(jnp.logical_and(i == 0, j == 0))
    def _():
        prepare(0, 0)

    last = j == pl.num_programs(1) - 1

    @pl.when(last)
    def _():
        o_ref[...] = jnp.dot(h_s[i % 2], w_ref[...], preferred_element_type=F32)
        prepare(i + 1, (i + 1) % 2)

    @pl.when(jnp.logical_not(last))
    def _():
        o_ref[...] = jnp.dot(h_s[i % 2], w_ref[...], preferred_element_type=F32)


def _in_proj(src, norm_g, mod3, w_bf):
    tm, tn = 512, 1024
    n_tiles, nj = N_TOK // tm, IN_W // tn

    def tile(i, j):
        return jnp.minimum(i + (j + 1) // nj, n_tiles - 1)

    if isinstance(src, tuple):
        n_first = N_LAT // tm
        x_specs = [pl.BlockSpec((tm, D_MODEL), lambda i, j: (jnp.minimum(tile(i, j), n_first - 1), 0)),
                   pl.BlockSpec((tm, D_MODEL), lambda i, j: (jnp.maximum(tile(i, j) - n_first, 0), 0))]
    else:
        n_first, src = None, (src,)
        x_specs = [pl.BlockSpec((tm, D_MODEL), lambda i, j: (tile(i, j), 0))]
    return pl.pallas_call(
        functools.partial(_proj_kernel, n_first=n_first),
        grid=(n_tiles, nj),
        in_specs=x_specs + [
            pl.BlockSpec((1, D_MODEL), lambda i, j: (0, 0)),
            pl.BlockSpec((None, 1, D_MODEL), lambda i, j: (_mod_row(tile(i, j), tm), 0, 0)),
            pl.BlockSpec((None, 1, D_MODEL), lambda i, j: (_mod_row(tile(i, j), tm), 0, 1)),
            pl.BlockSpec((D_MODEL, tn), lambda i, j: (0, j)),
        ],
        out_specs=pl.BlockSpec((tm, tn), lambda i, j: (i, j)),
        out_shape=jax.ShapeDtypeStruct((N_TOK, IN_W), F32),
        scratch_shapes=[pltpu.VMEM((2, tm, D_MODEL), BF16)],
        compiler_params=_cparams(("arbitrary", "arbitrary")),
        name="in_proj",
    )(*src, norm_g.reshape(1, D_MODEL), mod3, mod3, w_bf)


def _rope(x, cos, sin):
    lane = lax.broadcasted_iota(jnp.int32, x.shape, 1)
    first = (lane % 64) < 32
    partner = jnp.where(first, pltpu.roll(x, 96, 1), pltpu.roll(x, 32, 1))
    return x * cos + partner * sin


def _softmax_pv(q, k, v):
    s = lax.dot_general(q, k, (((1,), (1,)), ((), ())), preferred_element_type=F32) * ATTN_SCALE
    m = jnp.max(s, axis=-1, keepdims=True)
    p = jnp.exp(s - m)
    l = jnp.sum(p, axis=-1, keepdims=True)
    return jnp.dot(p.astype(BF16), v, preferred_element_type=F32) / l


def _lat_attn_kernel(q_ref, kl_ref, vl_ref, kc_ref, vc_ref, cosq_ref, sinq_ref, cosk_ref, sink_ref,
                     qg_ref, kg_ref, o_ref, k_s, v_s, s_s):
    @pl.when(pl.program_id(2) == 0)
    def _():
        k_s[0:CTX_LEN, :] = _rms(kc_ref[...], kg_ref[...]).astype(BF16)
        kl = _rope(_rms(kl_ref[...], kg_ref[...]), cosk_ref[...], sink_ref[...])
        k_s[CTX_LEN:, :] = kl.astype(BF16)
        v_s[0:CTX_LEN, 0:HEAD_DIM] = vc_ref[...].astype(BF16)
        v_s[CTX_LEN:, 0:HEAD_DIM] = vl_ref[...].astype(BF16)
        v_s[:, HEAD_DIM:] = jnp.ones((CTX_LEN + SEQ, LANES), BF16)

    def scores(g):
        lanes = slice(g * HEAD_DIM, (g + 1) * HEAD_DIM)
        q = _rope(_rms(q_ref[:, lanes], qg_ref[...]), cosq_ref[...], sinq_ref[...])
        q = (q * (ATTN_SCALE * LOG2_E)).astype(BF16)
        s_s[g % 2] = lax.dot_general(q, k_s[...], (((1,), (1,)), ((), ())), preferred_element_type=F32)

    scores(0)
    for g in range(GQA_GROUP):
        if g + 1 < GQA_GROUP:
            scores(g + 1)
        s = s_s[g % 2]
        p = jnp.exp2(s - jnp.max(s, axis=-1, keepdims=True))
        ol = jnp.dot(p.astype(BF16), v_s[...], preferred_element_type=F32)
        o = ol[:, 0:HEAD_DIM] / ol[:, HEAD_DIM:]
        o_ref[:, g * HEAD_DIM:(g + 1) * HEAD_DIM] = o.astype(o_ref.dtype)


def _latent_attention(proj, cos, sin, q_g, k_g):
    tq = 256
    nq = SEQ // tq
    qw = GQA_GROUP * HEAD_DIM
    kcol = ATTN_W // HEAD_DIM
    vcol = (ATTN_W + KV_W) // HEAD_DIM
    ctx_blk = N_LAT // CTX_LEN
    return pl.pallas_call(
        _lat_attn_kernel,
        grid=(BATCH, N_KV_HEADS, nq),
        in_specs=[
            pl.BlockSpec((tq, qw), lambda b, h, i: (b * nq + i, h)),
            pl.BlockSpec((SEQ, HEAD_DIM), lambda b, h, i: (b, kcol + h)),
            pl.BlockSpec((SEQ, HEAD_DIM), lambda b, h, i: (b, vcol + h)),
            pl.BlockSpec((CTX_LEN, HEAD_DIM), lambda b, h, i: (ctx_blk + b, kcol + h)),
            pl.BlockSpec((CTX_LEN, HEAD_DIM), lambda b, h, i: (ctx_blk + b, vcol + h)),
            pl.BlockSpec((tq, HEAD_DIM), lambda b, h, i: (i, 0)),
            pl.BlockSpec((tq, HEAD_DIM), lambda b, h, i: (i, 0)),
            pl.BlockSpec((SEQ, HEAD_DIM), lambda b, h, i: (0, 0)),
            pl.BlockSpec((SEQ, HEAD_DIM), lambda b, h, i: (0, 0)),
            pl.BlockSpec((1, HEAD_DIM), lambda b, h, i: (0, 0)),
            pl.BlockSpec((1, HEAD_DIM), lambda b, h, i: (0, 0)),
        ],
        out_specs=pl.BlockSpec((tq, qw), lambda b, h, i: (b * nq + i, h)),
        out_shape=jax.ShapeDtypeStruct((N_LAT, ATTN_W), BF16),
        scratch_shapes=[pltpu.VMEM((CTX_LEN + SEQ, HEAD_DIM), BF16),
                        pltpu.VMEM((CTX_LEN + SEQ, HEAD_DIM + LANES), BF16),
                        pltpu.VMEM((2, tq, CTX_LEN + SEQ), F32)],
        compiler_params=_cparams(("arbitrary", "arbitrary", "arbitrary")),
        name="latent_attention",
    )(proj, proj, proj, proj, proj, cos, sin, cos, sin, q_g.reshape(1, HEAD_DIM), k_g.reshape(1, HEAD_DIM))


def _ctx_attn_kernel(q_ref, k_ref, v_ref, qg_ref, kg_ref, o_ref):
    k = _rms(k_ref[...], kg_ref[...]).astype(BF16)
    v = v_ref[...].astype(BF16)
    for g in range(GQA_GROUP):
        lanes = slice(g * HEAD_DIM, (g + 1) * HEAD_DIM)
        q = _rms(q_ref[:, lanes], qg_ref[...]).astype(BF16)
        o_ref[:, lanes] = _softmax_pv(q, k, v).astype(o_ref.dtype)


def _context_attention(proj, q_g, k_g):
    qw = GQA_GROUP * HEAD_DIM
    kcol = ATTN_W // HEAD_DIM
    vcol = (ATTN_W + KV_W) // HEAD_DIM
    ctx_blk = N_LAT // CTX_LEN
    return pl.pallas_call(
        _ctx_attn_kernel,
        grid=(BATCH, N_KV_HEADS),
        in_specs=[
            pl.BlockSpec((CTX_LEN, qw), lambda b, h: (ctx_blk + b, h)),
            pl.BlockSpec((CTX_LEN, HEAD_DIM), lambda b, h: (ctx_blk + b, kcol + h)),
            pl.BlockSpec((CTX_LEN, HEAD_DIM), lambda b, h: (ctx_blk + b, vcol + h)),
            pl.BlockSpec((1, HEAD_DIM), lambda b, h: (0, 0)),
            pl.BlockSpec((1, HEAD_DIM), lambda b, h: (0, 0)),
        ],
        out_specs=pl.BlockSpec((CTX_LEN, qw), lambda b, h: (b, h)),
        out_shape=jax.ShapeDtypeStruct((N_CTX, ATTN_W), BF16),
        compiler_params=_cparams(("arbitrary", "arbitrary")),
        name="context_attention",
    )(proj, proj, proj, q_g.reshape(1, HEAD_DIM), k_g.reshape(1, HEAD_DIM))


def _mixer_kernel(pin_ref, prev_ref, next_ref, su_ref, sv_ref, pw_ref, ps_ref, sg_ref, sw_ref, sb_ref,
                  o_ref, pad_ref):
    tm = MIX_TM
    i = pl.program_id(0)
    is_lat = i < N_LAT // tm
    pos0 = jnp.where(is_lat, (i % (SEQ // tm)) * tm, 0)
    seq_len = jnp.where(is_lat, SEQ, CTX_LEN)
    pad_ref[0:POOL_HALO, :] = jnp.where(pos0 == 0, 0.0, prev_ref[...])
    pad_ref[POOL_HALO:POOL_HALO + tm, :] = pin_ref[...]
    pad_ref[POOL_HALO + tm:, :] = jnp.where(pos0 + tm == seq_len, 0.0, next_ref[...])
    t = pos0 + lax.broadcasted_iota(jnp.int32, (tm, LANES), 0)
    for gi, w in enumerate(POOL_WINDOWS):
        lanes = slice(gi * LANES, (gi + 1) * LANES)
        acc = pad_ref[POOL_HALO - w // 2:POOL_HALO - w // 2 + tm, lanes]
        for d in range(-w // 2 + 1, w // 2):
            acc = acc + pad_ref[POOL_HALO + d:POOL_HALO + d + tm, lanes]
        cnt = (jnp.minimum(t + w // 2, seq_len) - jnp.maximum(t - w // 2, 0)).astype(F32)
        mixed = acc / cnt - pin_ref[:, lanes]
        y = jnp.dot(mixed.astype(BF16), pw_ref[gi], preferred_element_type=F32) * ps_ref[:, lanes]
        o_ref[:, lanes] = y.astype(o_ref.dtype)

    for h in range(N_SGU_HEADS):
        lanes = slice(h * LANES, (h + 1) * LANES)
        gu = _gelu(su_ref[:, lanes])
        vh = _rms(_gelu(sv_ref[:, lanes]), sg_ref[h:h + 1, :]).astype(BF16)
        for n in range(tm // CHUNK):
            rows = slice(n * CHUNK, (n + 1) * CHUNK)
            mixed = jnp.dot(sw_ref[h], vh[rows], preferred_element_type=F32) + sb_ref[h]
            o_ref[rows, POOL_W + h * LANES:POOL_W + (h + 1) * LANES] = (gu[rows] * mixed).astype(o_ref.dtype)


def _mixers(proj, pool_w_bf, pool_scale, sgu_norm_g, sgu_w_bf, sgu_b_full):
    tm = MIX_TM
    per_tile = tm // POOL_HALO
    last_halo = N_TOK // POOL_HALO - 1
    pcol = (ATTN_W + 2 * KV_W) // POOL_W
    return pl.pallas_call(
        _mixer_kernel,
        grid=(N_TOK // tm,),
        in_specs=[
            pl.BlockSpec((tm, POOL_W), lambda i: (i, pcol)),
            pl.BlockSpec((POOL_HALO, POOL_W), lambda i: (jnp.maximum(i * per_tile - 1, 0), pcol)),
            pl.BlockSpec((POOL_HALO, POOL_W), lambda i: (jnp.minimum((i + 1) * per_tile, last_halo), pcol)),
            pl.BlockSpec((tm, SGU_W), lambda i: (i, pcol + 1)),
            pl.BlockSpec((tm, SGU_W), lambda i: (i, pcol + 2)),
            pl.BlockSpec((len(POOL_WINDOWS), LANES, LANES), lambda i: (0, 0, 0)),
            pl.BlockSpec((1, POOL_W), lambda i: (0, 0)),
            pl.BlockSpec((N_SGU_HEADS, LANES), lambda i: (0, 0)),
            pl.BlockSpec((N_SGU_HEADS, CHUNK, CHUNK), lambda i: (0, 0, 0)),
            pl.BlockSpec((N_SGU_HEADS, CHUNK, LANES), lambda i: (0, 0, 0)),
        ],
        out_specs=pl.BlockSpec((tm, POOL_W + SGU_W), lambda i: (i, 0)),
        out_shape=jax.ShapeDtypeStruct((N_TOK, POOL_W + SGU_W), BF16),
        scratch_shapes=[pltpu.VMEM((tm + 2 * POOL_HALO, POOL_W), F32)],
        compiler_params=_cparams(("arbitrary",)),
        name="mixers",
    )(proj, proj, proj, proj, proj, pool_w_bf, pool_scale.reshape(1, POOL_W), sgu_norm_g, sgu_w_bf, sgu_b_full)


def _outproj_kernel(*refs, n_first):
    if n_first is None:
        a_ref, m_ref, w1_ref, w2_ref, x_ref, g_ref, o_ref = refs
        a, x = a_ref[...], x_ref[...]
    else:
        a_ref, ac_ref, m_ref, w1_ref, w2_ref, x_ref, xc_ref, g_ref, o_ref = refs
        is_first = pl.program_id(1) < n_first
        a = jnp.where(is_first, a_ref[...], ac_ref[...])
        x = jnp.where(is_first, x_ref[...], xc_ref[...])
    y = jnp.dot(a, w1_ref[...], preferred_element_type=F32)
    y = y + jnp.dot(m_ref[...], w2_ref[...], preferred_element_type=F32)
    o_ref[...] = x + g_ref[...] * y


def _out_proj(attn, mix, w_bf, src, mod3, n_rows):
    tm, tn = 512, 1024
    nj = D_MODEL // tn
    assert isinstance(attn, tuple) == isinstance(src, tuple)
    if isinstance(src, tuple):
        n_first = N_LAT // tm

        def first(i):
            return jnp.minimum(i, n_first - 1)

        def second(i):
            return jnp.maximum(i - n_first, 0)

        attn_specs = [pl.BlockSpec((tm, ATTN_W), lambda j, i: (first(i), 0)),
                      pl.BlockSpec((tm, ATTN_W), lambda j, i: (second(i), 0))]
        x_specs = [pl.BlockSpec((tm, tn), lambda j, i: (first(i), j)),
                   pl.BlockSpec((tm, tn), lambda j, i: (second(i), j))]
    else:
        n_first, attn, src = None, (attn,), (src,)
        attn_specs = [pl.BlockSpec((tm, ATTN_W), lambda j, i: (i, 0))]
        x_specs = [pl.BlockSpec((tm, tn), lambda j, i: (i, j))]
    return pl.pallas_call(
        functools.partial(_outproj_kernel, n_first=n_first),
        grid=(nj, n_rows // tm),
        in_specs=attn_specs + [
            pl.BlockSpec((tm, POOL_W + SGU_W), lambda j, i: (i, 0)),
            pl.BlockSpec((ATTN_W, tn), lambda j, i: (0, j)),
            pl.BlockSpec((POOL_W + SGU_W, tn), lambda j, i: (1, j)),
        ] + x_specs + [
            pl.BlockSpec((None, 1, tn), lambda j, i: (_mod_row(i, tm), 0, 2 * nj + j)),
        ],
        out_specs=pl.BlockSpec((tm, tn), lambda j, i: (i, j)),
        out_shape=jax.ShapeDtypeStruct((n_rows, D_MODEL), F32),
        compiler_params=_cparams(("arbitrary", "arbitrary")),
        name="out_proj",
    )(*attn, mix, w_bf, w_bf, *src, mod3)


def _router_kernel(x_ref, g_ref, sh_ref, sc_ref, rw_ref, rb_ref, tri_ref,
                   hp_ref, idx_ref, rank_ref, gate_ref, cnt_ref, run_s):
    @pl.when(pl.program_id(0) == 0)
    def _():
        run_s[...] = jnp.zeros_like(run_s)

    h = _rms(x_ref[...], g_ref[...]) * (1.0 + sc_ref[...]) + sh_ref[...]
    h_hi = h.astype(BF16)
    hb = h_hi.astype(F32)
    hi = lax.bitcast_convert_type(hb[:, :HALF_D], U32)
    lo = lax.bitcast_convert_type(hb[:, HALF_D:], U32)
    packed = hi | (lo >> 16)
    for j in range(PACK_TILES):
        hp_ref[pl.ds(j, packed.shape[0], stride=PACK_TILES), :] = packed[:, j * LANES:(j + 1) * LANES]

    h_lo = (h - hb).astype(BF16)
    both = jnp.dot(h_hi, rw_ref[...], preferred_element_type=F32)
    tail = jnp.dot(h_lo, rw_ref[:, 0:LANES], preferred_element_type=F32)
    logits = (both[:, 0:LANES] + both[:, LANES:]) + tail
    lt = (logits + rb_ref[...]).T[0:N_EXPERTS, :]
    expert = lax.broadcasted_iota(jnp.int32, lt.shape, 0).astype(F32)
    vals, idxs = [], []
    for _ in range(TOP_K):
        m = jnp.max(lt, axis=0, keepdims=True)
        idx = jnp.min(jnp.where(lt == m, expert, float(N_EXPERTS)), axis=0, keepdims=True)
        vals.append(m)
        idxs.append(idx)
        lt = jnp.where(expert == idx, -jnp.inf, lt)

    e = [jnp.exp(v - vals[0]) for v in vals]
    den = e[0] + e[1] + e[2] + e[3]
    slot = lax.broadcasted_iota(jnp.int32, (LANES, lt.shape[1]), 0)
    gates = jnp.zeros((LANES, lt.shape[1]), F32)
    for k in range(TOP_K):
        gates = jnp.where(slot == k, e[k] / den, gates)
    gate_ref[...] = gates.T

    base = run_s[...]
    for k in range(TOP_K):
        onehot = jnp.where(expert == idxs[k], 1.0, 0.0)
        before = jnp.dot(onehot.astype(BF16), tri_ref[...], preferred_element_type=F32)
        rank = jnp.sum(onehot * (before + base[:, 0:1]), axis=0, keepdims=True)
        idx_ref[k:k + 1, :] = idxs[k].astype(jnp.int32)
        rank_ref[k:k + 1, :] = rank.astype(jnp.int32)
        base = base + jnp.sum(onehot, axis=1, keepdims=True)
    run_s[...] = base
    cnt_ref[...] = base.astype(jnp.int32)


def _router(tok, norm_g, mod3, rw_pad, rb_pad, tri, n_rows):
    tm = ROUTER_TM
    return pl.pallas_call(
        _router_kernel,
        grid=(n_rows // tm,),
        in_specs=[
            pl.BlockSpec((tm, D_MODEL), lambda i: (i, 0)),
            pl.BlockSpec((1, D_MODEL), lambda i: (0, 0)),
            pl.BlockSpec((None, 1, D_MODEL), lambda i: (_mod_row(i, tm), 0, 3)),
            pl.BlockSpec((None, 1, D_MODEL), lambda i: (_mod_row(i, tm), 0, 4)),
            pl.BlockSpec((D_MODEL, 2 * LANES), lambda i: (0, 0)),
            pl.BlockSpec((1, LANES), lambda i: (0, 0)),
            pl.BlockSpec((tm, tm), lambda i: (0, 0)),
        ],
        out_specs=[
            pl.BlockSpec((tm * PACK_TILES, LANES), lambda i: (i, 0)),
            pl.BlockSpec((TOP_K, tm), lambda i: (0, i)),
            pl.BlockSpec((TOP_K, tm), lambda i: (0, i)),
            pl.BlockSpec((tm, LANES), lambda i: (i, 0)),
            pl.BlockSpec((N_EXPERTS, LANES), lambda i: (0, 0)),
        ],
        out_shape=[
            jax.ShapeDtypeStruct((n_rows * PACK_TILES, LANES), U32),
            jax.ShapeDtypeStruct((TOP_K, n_rows), jnp.int32),
            jax.ShapeDtypeStruct((TOP_K, n_rows), jnp.int32),
            jax.ShapeDtypeStruct((n_rows, LANES), F32),
            jax.ShapeDtypeStruct((N_EXPERTS, LANES), jnp.int32),
        ],
        scratch_shapes=[pltpu.VMEM((N_EXPERTS, LANES), F32)],
        compiler_params=_cparams(("arbitrary",)),
        name="router",
    )(tok, norm_g.reshape(1, D_MODEL), mod3, mod3, rw_pad, rb_pad, tri)


def _dispatch_kernel(dest_ref, pad_ref, hp_ref, xs_hbm, sem, pad_sem, *, rows, n_rows):
    i = pl.program_id(0)

    @pl.when(i == 0)
    def _():
        def per_expert(e, carry):
            lo, hi = pad_ref[e], pad_ref[N_EXPERTS + 1 + e]

            def start(p, c):
                pltpu.make_async_copy(hp_ref.at[0], xs_hbm.at[p], pad_sem).start()
                return c

            def wait(p, c):
                pltpu.make_async_copy(hp_ref.at[0], xs_hbm.at[p], pad_sem).wait()
                return c

            lax.fori_loop(lo, hi, start, 0)
            lax.fori_loop(lo, hi, wait, 0)
            return carry

        lax.fori_loop(0, N_EXPERTS + 1, per_expert, 0)

    base = i * rows

    def issue(r2, carry):
        for u in range(2):
            r = 2 * r2 + u
            for k in range(TOP_K):
                d = dest_ref[k * n_rows + base + r]
                pltpu.make_async_copy(hp_ref.at[r], xs_hbm.at[d], sem).start(priority=k % 2)
        return carry

    lax.fori_loop(0, rows // 2, issue, 0)
    for k in range(TOP_K):
        pltpu.make_async_copy(hp_ref, xs_hbm.at[pl.ds(0, rows)], sem).wait()


def _dispatch(dest_flat, pad_meta, hp, n_pad, n_rows):
    rows = 1024
    return pl.pallas_call(
        functools.partial(_dispatch_kernel, rows=rows, n_rows=n_rows),
        grid_spec=pltpu.PrefetchScalarGridSpec(
            num_scalar_prefetch=2,
            grid=(n_rows // rows,),
            in_specs=[pl.BlockSpec((rows, PACK_TILES, LANES), lambda i, d, p: (i, 0, 0))],
            out_specs=pl.BlockSpec(memory_space=pl.ANY),
            scratch_shapes=[pltpu.SemaphoreType.DMA(()), pltpu.SemaphoreType.DMA(())],
        ),
        out_shape=jax.ShapeDtypeStruct((n_pad, PACK_TILES, LANES), U32),
        compiler_params=_cparams(("arbitrary",)),
        name="dispatch",
    )(dest_flat, pad_meta, hp)


def _is_new_expert(meta_ref, i):
    prev = meta_ref[jnp.maximum(i - 1, 0)]
    return jnp.logical_or(i == 0, meta_ref[i] != prev)


def _cast_rows(src_ref, dst_ref, chunk=256):
    def body(c, carry):
        rows = pl.ds(pl.multiple_of(c * chunk, chunk), chunk)
        dst_ref[rows, :] = src_ref[rows, :].astype(dst_ref.dtype)
        return carry

    lax.fori_loop(0, src_ref.shape[0] // chunk, body, 0)


def _for_valid_rows(valid, fn):
    lo = 0
    for n in EXPERT_ROW_STEPS:
        @pl.when(jnp.logical_and(valid > lo, valid <= n))
        def _(n=n):
            fn(n)

        lo = n


def _unpack_rows(x_ref, n):
    xp = jnp.concatenate([x_ref[pl.ds(j, n, stride=PACK_TILES), :] for j in range(PACK_TILES)], axis=1)
    hi = lax.bitcast_convert_type(xp & jnp.uint32(0xFFFF0000), F32).astype(BF16)
    lo = lax.bitcast_convert_type(xp << 16, F32).astype(BF16)
    return jnp.concatenate([hi, lo], axis=1)


def _expert_gu_kernel(meta_ref, x_ref, w_hbm, bg_ref, bu_ref, o_ref, wg_f, wu_f, wg_s, wu_s, sems,
                      *, n_blocks, layer):
    j, i = pl.program_id(0), pl.program_id(1)
    tn = wg_f.shape[1]
    nj = pl.num_programs(0)

    def fetch(e, jj):
        col = pl.multiple_of(jj * tn, tn)
        return (pltpu.make_async_copy(w_hbm.at[layer, e, :, pl.ds(col, tn)], wg_f, sems.at[0]),
                pltpu.make_async_copy(w_hbm.at[layer, e, :, pl.ds(EXPERT_FF + col, tn)], wu_f, sems.at[1]))

    @pl.when(jnp.logical_and(j == 0, i == 0))
    def _():
        for cp in fetch(meta_ref[0], 0):
            cp.start()

    active = i < meta_ref[n_blocks]
    valid = meta_ref[2 * n_blocks + 1 + i]
    fresh = jnp.logical_and(active, _is_new_expert(meta_ref, i))
    full = valid == o_ref.shape[0]

    def wait_fetch():
        for cp in fetch(meta_ref[i], j):
            cp.wait()

    def start_next():
        nxt = meta_ref[n_blocks + 1 + i]

        @pl.when(nxt >= 0)
        def _():
            for cp in fetch(nxt, j):
                cp.start()

        @pl.when(jnp.logical_and(nxt < 0, j + 1 < nj))
        def _():
            for cp in fetch(meta_ref[0], j + 1):
                cp.start()

    def swiglu(x, cols):
        g = jnp.dot(x, wg_s[:, cols], preferred_element_type=F32) + bg_ref[:, cols]
        u = jnp.dot(x, wu_s[:, cols], preferred_element_type=F32) + bu_ref[:, cols]
        g = jnp.minimum(g, SWIGLU_LIMIT)
        u = jnp.clip(u, -SWIGLU_LIMIT, SWIGLU_LIMIT)
        return ((u + 1.0) * (g * _sigmoid(g * SWIGLU_ALPHA))).astype(o_ref.dtype)

    @pl.when(jnp.logical_and(fresh, full))
    def _():
        wait_fetch()
        x = _unpack_rows(x_ref, o_ref.shape[0])
        tc = tn // GU_CAST_TILES
        for c in range(GU_CAST_TILES):
            cols = slice(c * tc, (c + 1) * tc)
            wg_s[:, cols] = wg_f[:, cols].astype(BF16)
            wu_s[:, cols] = wu_f[:, cols].astype(BF16)
            o_ref[:, cols] = swiglu(x, cols)
        start_next()

    @pl.when(jnp.logical_and(fresh, jnp.logical_not(full)))
    def _():
        wait_fetch()
        _cast_rows(wg_f, wg_s)
        _cast_rows(wu_f, wu_s)
        start_next()

    def compute(n):
        o_ref[0:n, :] = swiglu(_unpack_rows(x_ref, n), slice(None))
        if n < o_ref.shape[0]:
            o_ref[n:, :] = jnp.zeros((o_ref.shape[0] - n, o_ref.shape[1]), o_ref.dtype)

    _for_valid_rows(jnp.where(jnp.logical_and(fresh, full), 0, valid), compute)

    @pl.when(jnp.logical_not(active))
    def _():
        o_ref[...] = jnp.zeros_like(o_ref)


def _active_block(i, m, n_blocks):
    return jnp.minimum(i, m[n_blocks] - 1)


def _expert_gu(meta, xs, w_gu, b_gu, layer, n_blocks):
    tm, tn = EXPERT_TM, 1024
    nj = EXPERT_FF // tn
    b3 = b_gu.reshape(DEPTH, N_EXPERTS, 1, 2 * EXPERT_FF)
    return pl.pallas_call(
        functools.partial(_expert_gu_kernel, n_blocks=n_blocks, layer=layer),
        grid_spec=pltpu.PrefetchScalarGridSpec(
            num_scalar_prefetch=1,
            grid=(nj, n_blocks),
            in_specs=[
                pl.BlockSpec((tm * PACK_TILES, LANES), lambda j, i, m: (_active_block(i, m, n_blocks), 0)),
                pl.BlockSpec(memory_space=pl.ANY),
                pl.BlockSpec((None, None, 1, tn), lambda j, i, m: (layer, m[i], 0, j)),
                pl.BlockSpec((None, None, 1, tn), lambda j, i, m: (layer, m[i], 0, nj + j)),
            ],
            out_specs=pl.BlockSpec((tm, tn), lambda j, i, m: (i, j)),
            scratch_shapes=[pltpu.VMEM((D_MODEL, tn), F32), pltpu.VMEM((D_MODEL, tn), F32),
                            pltpu.VMEM((D_MODEL, tn), BF16), pltpu.VMEM((D_MODEL, tn), BF16),
                            pltpu.SemaphoreType.DMA((2,))],
        ),
        out_shape=jax.ShapeDtypeStruct((n_blocks * tm, EXPERT_FF), BF16),
        compiler_params=_cparams(("arbitrary", "arbitrary")),
        name="expert_gu",
    )(meta, xs, w_gu, b3, b3)


def _expert_down_kernel(meta_ref, a_ref, w_hbm, b_ref, o_ref, w_f, w_s, sem, *, n_blocks, layer):
    i = pl.program_id(0)

    def fetch(e):
        return pltpu.make_async_copy(w_hbm.at[layer, e], w_f, sem)

    @pl.when(i == 0)
    def _():
        fetch(meta_ref[0]).start()

    active = i < meta_ref[n_blocks]
    valid = meta_ref[2 * n_blocks + 1 + i]
    fresh = jnp.logical_and(active, _is_new_expert(meta_ref, i))
    full = valid == o_ref.shape[0]

    def start_next():
        nxt = meta_ref[n_blocks + 1 + i]

        @pl.when(nxt >= 0)
        def _():
            fetch(nxt).start()

    @pl.when(jnp.logical_and(fresh, full))
    def _():
        fetch(meta_ref[i]).wait()
        a = a_ref[...]
        tn = D_MODEL // DOWN_CAST_TILES
        for c in range(DOWN_CAST_TILES):
            cols = slice(c * tn, (c + 1) * tn)
            w_s[:, cols] = w_f[:, cols].astype(BF16)
            o_ref[:, cols] = jnp.dot(a, w_s[:, cols], preferred_element_type=F32) + b_ref[:, cols]
        start_next()

    @pl.when(jnp.logical_and(fresh, jnp.logical_not(full)))
    def _():
        fetch(meta_ref[i]).wait()
        _cast_rows(w_f, w_s)
        start_next()

    def compute(n):
        o_ref[0:n, :] = jnp.dot(a_ref[0:n, :], w_s[...], preferred_element_type=F32) + b_ref[...]
        if n < o_ref.shape[0]:
            o_ref[n:, :] = jnp.zeros((o_ref.shape[0] - n, o_ref.shape[1]), o_ref.dtype)

    _for_valid_rows(jnp.where(jnp.logical_and(fresh, full), 0, valid), compute)

    @pl.when(jnp.logical_not(active))
    def _():
        o_ref[...] = jnp.zeros_like(o_ref)


def _expert_down(meta, act, w_down, b_down, layer, n_blocks):
    tm = EXPERT_TM
    b3 = b_down.reshape(DEPTH, N_EXPERTS, 1, D_MODEL)
    return pl.pallas_call(
        functools.partial(_expert_down_kernel, n_blocks=n_blocks, layer=layer),
        grid_spec=pltpu.PrefetchScalarGridSpec(
            num_scalar_prefetch=1,
            grid=(n_blocks,),
            in_specs=[
                pl.BlockSpec((tm, EXPERT_FF), lambda i, m: (_active_block(i, m, n_blocks), 0)),
                pl.BlockSpec(memory_space=pl.ANY),
                pl.BlockSpec((None, None, 1, D_MODEL), lambda i, m: (layer, m[i], 0, 0)),
            ],
            out_specs=pl.BlockSpec((tm, D_MODEL), lambda i, m: (i, 0)),
            scratch_shapes=[pltpu.VMEM((EXPERT_FF, D_MODEL), F32), pltpu.VMEM((EXPERT_FF, D_MODEL), BF16),
                            pltpu.SemaphoreType.DMA(())],
        ),
        out_shape=jax.ShapeDtypeStruct((n_blocks * tm, D_MODEL), F32),
        compiler_params=_cparams(("arbitrary",)),
        name="expert_down",
    )(meta, act, w_down, b3)


def _combine_kernel(dest_ref, y_hbm, gt_ref, x_ref, g_ref, o_ref, buf, sems, *, rows, n_rows):
    i = pl.program_id(0)
    n_steps = pl.num_programs(0)

    def issue(step, slot):
        base = step * rows

        def body(r8, carry):
            r0 = pl.multiple_of(r8 * 8, 8)
            for u in range(8):
                for k in range(TOP_K):
                    d = dest_ref[k * n_rows + base + r0 + u]
                    pltpu.make_async_copy(y_hbm.at[pl.ds(d, 1), :],
                                          buf.at[slot, k, pl.ds(r0, 8), :].at[pl.ds(u, 1), :],
                                          sems.at[slot]).start(priority=k % 2)
            return carry

        lax.fori_loop(0, rows // 8, body, 0)

    @pl.when(i == 0)
    def _():
        issue(0, 0)

    @pl.when(i + 1 < n_steps)
    def _():
        issue(i + 1, (i + 1) % 2)

    slot = i % 2
    for k in range(TOP_K):
        pltpu.make_async_copy(y_hbm.at[pl.ds(0, rows), :], buf.at[slot, k], sems.at[slot]).wait()
    gt = gt_ref[...]
    moe = gt[:, 0:1] * buf[slot, 0]
    for k in range(1, TOP_K):
        moe = moe + gt[:, k:k + 1] * buf[slot, k]
    o_ref[...] = x_ref[...] + g_ref[...] * moe


def _combine(dest_flat, ys, gates_t, tok, mod3, n_rows):
    rows = 128
    return pl.pallas_call(
        functools.partial(_combine_kernel, rows=rows, n_rows=n_rows),
        grid_spec=pltpu.PrefetchScalarGridSpec(
            num_scalar_prefetch=1,
            grid=(n_rows // rows,),
            in_specs=[
                pl.BlockSpec(memory_space=pl.ANY),
                pl.BlockSpec((rows, LANES), lambda i, d: (i, 0)),
                pl.BlockSpec((rows, D_MODEL), lambda i, d: (i, 0)),
                pl.BlockSpec((None, 1, D_MODEL), lambda i, d: (_mod_row(i, rows), 0, 5)),
            ],
            out_specs=pl.BlockSpec((rows, D_MODEL), lambda i, d: (i, 0)),
            scratch_shapes=[pltpu.VMEM((2, TOP_K, rows, D_MODEL), F32), pltpu.SemaphoreType.DMA((2,))],
        ),
        out_shape=jax.ShapeDtypeStruct((n_rows, D_MODEL), F32),
        compiler_params=_cparams(("arbitrary",)),
        name="combine",
    )(dest_flat, ys, gates_t, tok, mod3)


def _routing(top_idx, rank, counts, n_rows):
    tm = EXPERT_TM
    n_blocks = -(-(TOP_K * n_rows) // tm) + N_EXPERTS
    counts = counts[:, 0]
    padded = (counts + tm - 1) // tm * tm
    padded_end = jnp.cumsum(padded)
    padded_start = padded_end - padded
    experts = jnp.arange(N_EXPERTS, dtype=jnp.int32)
    start_of = jnp.sum(jnp.where(top_idx[:, :, None] == experts, padded_start, 0), axis=-1)
    dest = (start_of + rank).astype(jnp.int32).reshape(TOP_K * n_rows)
    block_start = jnp.arange(n_blocks, dtype=jnp.int32) * tm
    block_expert = jnp.minimum(jnp.sum(padded_end[None, :] <= block_start[:, None], axis=1), N_EXPERTS - 1)
    n_active = padded_end[-1:] // tm
    group_end = jnp.sum(jnp.where(block_expert[:, None] == experts, padded_end // tm, 0), axis=1)
    follower = jnp.sum(jnp.where(group_end[:, None] == jnp.arange(n_blocks)[None, :], block_expert, 0), axis=1)
    next_expert = jnp.where(group_end < n_active, follower, -1)
    is_expert = block_expert[:, None] == experts
    block_count = jnp.sum(jnp.where(is_expert, counts, 0), axis=1)
    block_first = jnp.sum(jnp.where(is_expert, padded_start, 0), axis=1)
    block_valid = jnp.clip(block_count - (block_start - block_first), 0, tm)
    meta = jnp.concatenate([block_expert, n_active, next_expert, block_valid]).astype(jnp.int32)
    n_pad = jnp.full((1,), n_blocks * tm, jnp.int32)
    pad_meta = jnp.concatenate([padded_start + counts, padded_end[-1:], padded_end, n_pad]).astype(jnp.int32)
    return dest, meta, pad_meta, n_blocks


def _rope_tables():
    rows = SEQ // GRID_W
    row = jnp.broadcast_to(jnp.arange(rows, dtype=F32)[:, None], (rows, GRID_W)).reshape(-1)
    col = jnp.broadcast_to(jnp.arange(GRID_W, dtype=F32)[None, :], (rows, GRID_W)).reshape(-1)
    inv_freq = ROPE_THETA ** (-jnp.arange(ROPE_FREQS, dtype=F32) / ROPE_FREQS)
    ang_r = row[:, None] * inv_freq
    ang_c = col[:, None] * inv_freq
    cos = jnp.concatenate([jnp.cos(ang_r), jnp.cos(ang_r), jnp.cos(ang_c), jnp.cos(ang_c)], axis=-1)
    sin = jnp.concatenate([-jnp.sin(ang_r), jnp.sin(ang_r), -jnp.sin(ang_c), jnp.sin(ang_c)], axis=-1)
    return cos, sin


def kernel(x, c, ctx, c_ctx, ada_w, ada_b, norm1_g, norm2_g, w_in, q_norm_g, k_norm_g, pool_w, pool_scale,
           sgu_norm_g, sgu_w, sgu_b, w_out, router_w, router_b, w_gu, b_gu, w_down, b_down):
    cos, sin = _rope_tables()
    tok = (x.reshape(N_LAT, D_MODEL), ctx.reshape(N_CTX, D_MODEL))
    cc = jnp.zeros((MOD_ROWS, D_MODEL), F32).at[:BATCH].set(c).at[BATCH].set(c_ctx)
    mod = _adaln(cc, ada_w, ada_b)
    tri = jnp.triu(jnp.ones((ROUTER_TM, ROUTER_TM), BF16), k=1)
    for l in range(DEPTH):
        last = l == DEPTH - 1
        n_rows = N_LAT if last else N_TOK
        mod3 = mod[l].reshape(MOD_ROWS, 1, 6 * D_MODEL)
        proj = _in_proj(tok, norm1_g[l], mod3, w_in[l].astype(BF16))
        attn = _latent_attention(proj, cos, sin, q_norm_g[l], k_norm_g[l])
        if not last:
            attn = (attn, _context_attention(proj, q_norm_g[l], k_norm_g[l]))
        sgu_b_full = jnp.broadcast_to(sgu_b[l][:, :, None], (N_SGU_HEADS, CHUNK, LANES))
        mix = _mixers(proj, pool_w[l].astype(BF16), pool_scale[l], sgu_norm_g[l], sgu_w[l].astype(BF16), sgu_b_full)
        tok = _out_proj(attn, mix, w_out[l].astype(BF16), tok, mod3, n_rows)
        rw_hi = router_w[l].astype(BF16)
        rw_lo = (router_w[l] - rw_hi.astype(F32)).astype(BF16)
        rw_pad = (jnp.zeros((D_MODEL, 2 * LANES), BF16).at[:, :N_EXPERTS].set(rw_hi)
                  .at[:, LANES:LANES + N_EXPERTS].set(rw_lo))
        rb_pad = jnp.zeros((1, LANES), F32).at[0, :N_EXPERTS].set(router_b[l])
        hp, top_idx, rank, gates_t, counts = _router(tok, norm2_g[l], mod3, rw_pad, rb_pad, tri, n_rows)
        dest, meta, pad_meta, n_blocks = _routing(top_idx, rank, counts, n_rows)
        n_pad = n_blocks * EXPERT_TM
        xs = _dispatch(dest, pad_meta, hp.reshape(n_rows, PACK_TILES, LANES), n_pad, n_rows)
        act = _expert_gu(meta, xs.reshape(n_pad * PACK_TILES, LANES), w_gu, b_gu, l, n_blocks)
        ys = _expert_down(meta, act, w_down, b_down, l, n_blocks)
        tok = _combine(dest, ys, gates_t, tok, mod3, n_rows)
    return tok.reshape(BATCH, SEQ, D_MODEL)
```

```python
import functools

import jax
import jax.numpy as jnp
from jax import lax
from jax.experimental import pallas as pl
from jax.experimental.pallas import tpu as pltpu

F32 = jnp.float32
BF16 = jnp.bfloat16
U32 = jnp.uint32

D_MODEL = 2048
BATCH = 4
SEQ = 4096
DEPTH = 2
GRID_W = 64
CTX_LEN = 256
HEAD_DIM = 128
ATTN_W = 1024
GQA_GROUP = 4
N_KV_HEADS = 2
KV_W = 256
ROPE_THETA = 10000.0
ROPE_FREQS = 32
ATTN_SCALE = HEAD_DIM ** -0.5
LOG2_E = 1.4426950408889634
POOL_WINDOWS = (2, 4, 8, 16)
POOL_W = 512
SGU_W = 512
N_SGU_HEADS = 4
CHUNK = 128
IN_W = 3072
N_EXPERTS = 32
TOP_K = 4
EXPERT_FF = 2048
SWIGLU_LIMIT = 7.0
SWIGLU_ALPHA = 1.702
EPS = 1e-6

N_LAT = BATCH * SEQ
N_CTX = BATCH * CTX_LEN
N_TOK = N_LAT + N_CTX
MOD_ROWS = 8
LANES = 128
POOL_HALO = 8
MIX_TM = 256
EXPERT_TM = 256
EXPERT_ROW_STEPS = (64, 128, 192, 256)
ROUTER_TM = 512
HALF_D = D_MODEL // 2
PACK_TILES = HALF_D // LANES
VMEM_LIMIT = 56 * 1024 * 1024


def _cparams(sem, vmem=VMEM_LIMIT):
    return pltpu.CompilerParams(dimension_semantics=sem, vmem_limit_bytes=vmem)


def _mod_row(row_tile, tm):
    return jnp.minimum(row_tile * tm // SEQ, BATCH)


def _rms(x, g):
    return x * lax.rsqrt(jnp.mean(x * x, axis=-1, keepdims=True) + EPS) * g


def _sigmoid(x):
    return 1.0 / (1.0 + jnp.exp(-x))


def _gelu(x):
    return 0.5 * x * (1.0 + lax.erf(x * 0.7071067811865476))


def _adaln_kernel(cc_ref, w_ref, b_ref, o_ref):
    cc = cc_ref[...]
    s = (cc * _sigmoid(cc)).astype(BF16)
    o_ref[...] = jnp.dot(s, w_ref[...].astype(BF16), preferred_element_type=F32) + b_ref[...]


def _adaln(cc, ada_w, ada_b):
    tn = 1024
    n = 6 * D_MODEL
    return pl.pallas_call(
        _adaln_kernel,
        grid=(DEPTH, n // tn),
        in_specs=[
            pl.BlockSpec((MOD_ROWS, D_MODEL), lambda l, j: (0, 0)),
            pl.BlockSpec((None, D_MODEL, tn), lambda l, j: (l, 0, j)),
            pl.BlockSpec((None, 1, tn), lambda l, j: (l, 0, j)),
        ],
        out_specs=pl.BlockSpec((None, MOD_ROWS, tn), lambda l, j: (l, 0, j)),
        out_shape=jax.ShapeDtypeStruct((DEPTH, MOD_ROWS, n), F32),
        compiler_params=_cparams(("arbitrary", "arbitrary")),
        name="adaln",
    )(cc, ada_w, ada_b.reshape(DEPTH, 1, n))


def _proj_kernel(*refs, n_first):
    if n_first is None:
        (xa_ref, g_ref, sh_ref, sc_ref, w_ref, o_ref, h_s), xb_ref = refs, None
    else:
        xa_ref, xb_ref, g_ref, sh_ref, sc_ref, w_ref, o_ref, h_s = refs
    i, j = pl.program_id(0), pl.program_id(1)

    def prepare(tile, slot):
        x = xa_ref[...]
        if xb_ref is not None:
            x = jnp.where(tile < n_first, x, xb_ref[...])
        y = _rms(x, g_ref[...])
        h_s[slot] = (y * (1.0 + sc_ref[...]) + sh_ref[...]).astype(BF16)

    @pl.when(jnp.logical_and(i == 0, j == 0))
    def _():
        prepare(0, 0)

    last = j == pl.num_programs(1) - 1

    @pl.when(last)
    def _():
        o_ref[...] = jnp.dot(h_s[i % 2], w_ref[...], preferred_element_type=F32)
        prepare(i + 1, (i + 1) % 2)

    @pl.when(jnp.logical_not(last))
    def _():
        o_ref[...] = jnp.dot(h_s[i % 2], w_ref[...], preferred_element_type=F32)


def _in_proj(src, norm_g, mod3, w_bf):
    tm, tn = 512, 1024
    n_tiles, nj = N_TOK // tm, IN_W // tn

    def tile(i, j):
        return jnp.minimum(i + (j + 1) // nj, n_tiles - 1)

    def col(i, j):
        return jnp.where(i % 2 == 0, j, nj - 1 - j)

    if isinstance(src, tuple):
        n_first = N_LAT // tm
        x_specs = [pl.BlockSpec((tm, D_MODEL), lambda i, j: (jnp.minimum(tile(i, j), n_first - 1), 0)),
                   pl.BlockSpec((tm, D_MODEL), lambda i, j: (jnp.maximum(tile(i, j) - n_first, 0), 0))]
    else:
        n_first, src = None, (src,)
        x_specs = [pl.BlockSpec((tm, D_MODEL), lambda i, j: (tile(i, j), 0))]
    return pl.pallas_call(
        functools.partial(_proj_kernel, n_first=n_first),
        grid=(n_tiles, nj),
        in_specs=x_specs + [
            pl.BlockSpec((1, D_MODEL), lambda i, j: (0, 0)),
            pl.BlockSpec((None, 1, D_MODEL), lambda i, j: (_mod_row(tile(i, j), tm), 0, 0)),
            pl.BlockSpec((None, 1, D_MODEL), lambda i, j: (_mod_row(tile(i, j), tm), 0, 1)),
            pl.BlockSpec((D_MODEL, tn), lambda i, j: (0, col(i, j))),
        ],
        out_specs=pl.BlockSpec((tm, tn), lambda i, j: (i, col(i, j))),
        out_shape=jax.ShapeDtypeStruct((N_TOK, IN_W), F32),
        scratch_shapes=[pltpu.VMEM((2, tm, D_MODEL), BF16)],
        compiler_params=_cparams(("arbitrary", "arbitrary")),
        name="in_proj",
    )(*src, norm_g.reshape(1, D_MODEL), mod3, mod3, w_bf)


def _rope(x, cos, sin):
    lane = lax.broadcasted_iota(jnp.int32, x.shape, 1)
    first = (lane % 64) < 32
    partner = jnp.where(first, pltpu.roll(x, 96, 1), pltpu.roll(x, 32, 1))
    return x * cos + partner * sin


def _softmax_pv(q, k, v):
    s = lax.dot_general(q, k, (((1,), (1,)), ((), ())), preferred_element_type=F32) * ATTN_SCALE
    m = jnp.max(s, axis=-1, keepdims=True)
    p = jnp.exp(s - m)
    l = jnp.sum(p, axis=-1, keepdims=True)
    return jnp.dot(p.astype(BF16), v, preferred_element_type=F32) / l


def _lat_attn_kernel(q_ref, kl_ref, vl_ref, kc_ref, vc_ref, cosq_ref, sinq_ref, cosk_ref, sink_ref,
                     qg_ref, kg_ref, o_ref, k_s, v_s, s_s):
    @pl.when(pl.program_id(2) == 0)
    def _():
        k_s[0:CTX_LEN, :] = _rms(kc_ref[...], kg_ref[...]).astype(BF16)
        kl = _rope(_rms(kl_ref[...], kg_ref[...]), cosk_ref[...], sink_ref[...])
        k_s[CTX_LEN:, :] = kl.astype(BF16)
        v_s[0:CTX_LEN, 0:HEAD_DIM] = vc_ref[...].astype(BF16)
        v_s[CTX_LEN:, 0:HEAD_DIM] = vl_ref[...].astype(BF16)
        v_s[:, HEAD_DIM:] = jnp.ones((CTX_LEN + SEQ, LANES), BF16)

    def scores(g):
        lanes = slice(g * HEAD_DIM, (g + 1) * HEAD_DIM)
        q = _rope(_rms(q_ref[:, lanes], qg_ref[...]), cosq_ref[...], sinq_ref[...])
        q = (q * (ATTN_SCALE * LOG2_E)).astype(BF16)
        s_s[g % 2] = lax.dot_general(q, k_s[...], (((1,), (1,)), ((), ())), preferred_element_type=F32)

    scores(0)
    for g in range(GQA_GROUP):
        if g + 1 < GQA_GROUP:
            scores(g + 1)
        s = s_s[g % 2]
        p = jnp.exp2(s - jnp.max(s, axis=-1, keepdims=True))
        ol = jnp.dot(p.astype(BF16), v_s[...], preferred_element_type=F32)
        o = ol[:, 0:HEAD_DIM] / ol[:, HEAD_DIM:]
        o_ref[:, g * HEAD_DIM:(g + 1) * HEAD_DIM] = o.astype(o_ref.dtype)


def _latent_attention(proj, cos, sin, q_g, k_g):
    tq = 256
    nq = SEQ // tq
    qw = GQA_GROUP * HEAD_DIM
    kcol = ATTN_W // HEAD_DIM
    vcol = (ATTN_W + KV_W) // HEAD_DIM
    ctx_blk = N_LAT // CTX_LEN
    return pl.pallas_call(
        _lat_attn_kernel,
        grid=(BATCH, N_KV_HEADS, nq),
        in_specs=[
            pl.BlockSpec((tq, qw), lambda b, h, i: (b * nq + i, h)),
            pl.BlockSpec((SEQ, HEAD_DIM), lambda b, h, i: (b, kcol + h)),
            pl.BlockSpec((SEQ, HEAD_DIM), lambda b, h, i: (b, vcol + h)),
            pl.BlockSpec((CTX_LEN, HEAD_DIM), lambda b, h, i: (ctx_blk + b, kcol + h)),
            pl.BlockSpec((CTX_LEN, HEAD_DIM), lambda b, h, i: (ctx_blk + b, vcol + h)),
            pl.BlockSpec((tq, HEAD_DIM), lambda b, h, i: (i, 0)),
            pl.BlockSpec((tq, HEAD_DIM), lambda b, h, i: (i, 0)),
            pl.BlockSpec((SEQ, HEAD_DIM), lambda b, h, i: (0, 0)),
            pl.BlockSpec((SEQ, HEAD_DIM), lambda b, h, i: (0, 0)),
            pl.BlockSpec((1, HEAD_DIM), lambda b, h, i: (0, 0)),
            pl.BlockSpec((1, HEAD_DIM), lambda b, h, i: (0, 0)),
        ],
        out_specs=pl.BlockSpec((tq, qw), lambda b, h, i: (b * nq + i, h)),
        out_shape=jax.ShapeDtypeStruct((N_LAT, ATTN_W), BF16),
        scratch_shapes=[pltpu.VMEM((CTX_LEN + SEQ, HEAD_DIM), BF16),
                        pltpu.VMEM((CTX_LEN + SEQ, HEAD_DIM + LANES), BF16),
                        pltpu.VMEM((2, tq, CTX_LEN + SEQ), F32)],
        compiler_params=_cparams(("arbitrary", "arbitrary", "arbitrary")),
        name="latent_attention",
    )(proj, proj, proj, proj, proj, cos, sin, cos, sin, q_g.reshape(1, HEAD_DIM), k_g.reshape(1, HEAD_DIM))


def _ctx_attn_kernel(q_ref, k_ref, v_ref, qg_ref, kg_ref, o_ref):
    k = _rms(k_ref[...], kg_ref[...]).astype(BF16)
    v = v_ref[...].astype(BF16)
    for g in range(GQA_GROUP):
        lanes = slice(g * HEAD_DIM, (g + 1) * HEAD_DIM)
        q = _rms(q_ref[:, lanes], qg_ref[...]).astype(BF16)
        o_ref[:, lanes] = _softmax_pv(q, k, v).astype(o_ref.dtype)


def _context_attention(proj, q_g, k_g):
    qw = GQA_GROUP * HEAD_DIM
    kcol = ATTN_W // HEAD_DIM
    vcol = (ATTN_W + KV_W) // HEAD_DIM
    ctx_blk = N_LAT // CTX_LEN
    return pl.pallas_call(
        _ctx_attn_kernel,
        grid=(BATCH, N_KV_HEADS),
        in_specs=[
            pl.BlockSpec((CTX_LEN, qw), lambda b, h: (ctx_blk + b, h)),
            pl.BlockSpec((CTX_LEN, HEAD_DIM), lambda b, h: (ctx_blk + b, kcol + h)),
            pl.BlockSpec((CTX_LEN, HEAD_DIM), lambda b, h: (ctx_blk + b, vcol + h)),
            pl.BlockSpec((1, HEAD_DIM), lambda b, h: (0, 0)),
            pl.BlockSpec((1, HEAD_DIM), lambda b, h: (0, 0)),
        ],
        out_specs=pl.BlockSpec((CTX_LEN, qw), lambda b, h: (b, h)),
        out_shape=jax.ShapeDtypeStruct((N_CTX, ATTN_W), BF16),
        compiler_params=_cparams(("arbitrary", "arbitrary")),
        name="context_attention",
    )(proj, proj, proj, q_g.reshape(1, HEAD_DIM), k_g.reshape(1, HEAD_DIM))


def _mixer_kernel(pin_ref, prev_ref, next_ref, su_ref, sv_ref, pw_ref, ps_ref, sg_ref, sw_ref, sb_ref,
                  o_ref, pad_ref):
    tm = MIX_TM
    i = pl.program_id(0)
    is_lat = i < N_LAT // tm
    pos0 = jnp.where(is_lat, (i % (SEQ // tm)) * tm, 0)
    seq_len = jnp.where(is_lat, SEQ, CTX_LEN)
    pad_ref[0:POOL_HALO, :] = jnp.where(pos0 == 0, 0.0, prev_ref[...])
    pad_ref[POOL_HALO:POOL_HALO + tm, :] = pin_ref[...]
    pad_ref[POOL_HALO + tm:, :] = jnp.where(pos0 + tm == seq_len, 0.0, next_ref[...])
    t = pos0 + lax.broadcasted_iota(jnp.int32, (tm, LANES), 0)
    for gi, w in enumerate(POOL_WINDOWS):
        lanes = slice(gi * LANES, (gi + 1) * LANES)
        acc = pad_ref[POOL_HALO - w // 2:POOL_HALO - w // 2 + tm, lanes]
        for d in range(-w // 2 + 1, w // 2):
            acc = acc + pad_ref[POOL_HALO + d:POOL_HALO + d + tm, lanes]
        cnt = (jnp.minimum(t + w // 2, seq_len) - jnp.maximum(t - w // 2, 0)).astype(F32)
        mixed = acc / cnt - pin_ref[:, lanes]
        y = jnp.dot(mixed.astype(BF16), pw_ref[gi], preferred_element_type=F32) * ps_ref[:, lanes]
        o_ref[:, lanes] = y.astype(o_ref.dtype)

    for h in range(N_SGU_HEADS):
        lanes = slice(h * LANES, (h + 1) * LANES)
        gu = _gelu(su_ref[:, lanes])
        vh = _rms(_gelu(sv_ref[:, lanes]), sg_ref[h:h + 1, :]).astype(BF16)
        for n in range(tm // CHUNK):
            rows = slice(n * CHUNK, (n + 1) * CHUNK)
            mixed = jnp.dot(sw_ref[h], vh[rows], preferred_element_type=F32) + sb_ref[h]
            o_ref[rows, POOL_W + h * LANES:POOL_W + (h + 1) * LANES] = (gu[rows] * mixed).astype(o_ref.dtype)


def _mixers(proj, pool_w_bf, pool_scale, sgu_norm_g, sgu_w_bf, sgu_b_full):
    tm = MIX_TM
    per_tile = tm // POOL_HALO
    last_halo = N_TOK // POOL_HALO - 1
    pcol = (ATTN_W + 2 * KV_W) // POOL_W
    return pl.pallas_call(
        _mixer_kernel,
        grid=(N_TOK // tm,),
        in_specs=[
            pl.BlockSpec((tm, POOL_W), lambda i: (i, pcol)),
            pl.BlockSpec((POOL_HALO, POOL_W), lambda i: (jnp.maximum(i * per_tile - 1, 0), pcol)),
            pl.BlockSpec((POOL_HALO, POOL_W), lambda i: (jnp.minimum((i + 1) * per_tile, last_halo), pcol)),
            pl.BlockSpec((tm, SGU_W), lambda i: (i, pcol + 1)),
            pl.BlockSpec((tm, SGU_W), lambda i: (i, pcol + 2)),
            pl.BlockSpec((len(POOL_WINDOWS), LANES, LANES), lambda i: (0, 0, 0)),
            pl.BlockSpec((1, POOL_W), lambda i: (0, 0)),
            pl.BlockSpec((N_SGU_HEADS, LANES), lambda i: (0, 0)),
            pl.BlockSpec((N_SGU_HEADS, CHUNK, CHUNK), lambda i: (0, 0, 0)),
            pl.BlockSpec((N_SGU_HEADS, CHUNK, LANES), lambda i: (0, 0, 0)),
        ],
        out_specs=pl.BlockSpec((tm, POOL_W + SGU_W), lambda i: (i, 0)),
        out_shape=jax.ShapeDtypeStruct((N_TOK, POOL_W + SGU_W), BF16),
        scratch_shapes=[pltpu.VMEM((tm + 2 * POOL_HALO, POOL_W), F32)],
        compiler_params=_cparams(("arbitrary",)),
        name="mixers",
    )(proj, proj, proj, proj, proj, pool_w_bf, pool_scale.reshape(1, POOL_W), sgu_norm_g, sgu_w_bf, sgu_b_full)


def _outproj_kernel(*refs, n_first):
    if n_first is None:
        a_ref, m_ref, w1_ref, w2_ref, x_ref, g_ref, o_ref = refs
        a, x = a_ref[...], x_ref[...]
    else:
        a_ref, ac_ref, m_ref, w1_ref, w2_ref, x_ref, xc_ref, g_ref, o_ref = refs
        is_first = pl.program_id(1) < n_first
        a = jnp.where(is_first, a_ref[...], ac_ref[...])
        x = jnp.where(is_first, x_ref[...], xc_ref[...])
    y = jnp.dot(a, w1_ref[...], preferred_element_type=F32)
    y = y + jnp.dot(m_ref[...], w2_ref[...], preferred_element_type=F32)
    o_ref[...] = x + g_ref[...] * y


def _out_proj(attn, mix, w_bf, src, mod3, n_rows):
    tm, tn = 512, 2048
    nj = D_MODEL // tn
    assert isinstance(attn, tuple) == isinstance(src, tuple)
    if isinstance(src, tuple):
        n_first = N_LAT // tm

        def first(i):
            return jnp.minimum(i, n_first - 1)

        def second(i):
            return jnp.maximum(i - n_first, 0)

        attn_specs = [pl.BlockSpec((tm, ATTN_W), lambda j, i: (first(i), 0)),
                      pl.BlockSpec((tm, ATTN_W), lambda j, i: (second(i), 0))]
        x_specs = [pl.BlockSpec((tm, tn), lambda j, i: (first(i), j)),
                   pl.BlockSpec((tm, tn), lambda j, i: (second(i), j))]
    else:
        n_first, attn, src = None, (attn,), (src,)
        attn_specs = [pl.BlockSpec((tm, ATTN_W), lambda j, i: (i, 0))]
        x_specs = [pl.BlockSpec((tm, tn), lambda j, i: (i, j))]
    return pl.pallas_call(
        functools.partial(_outproj_kernel, n_first=n_first),
        grid=(nj, n_rows // tm),
        in_specs=attn_specs + [
            pl.BlockSpec((tm, POOL_W + SGU_W), lambda j, i: (i, 0)),
            pl.BlockSpec((ATTN_W, tn), lambda j, i: (0, j)),
            pl.BlockSpec((POOL_W + SGU_W, tn), lambda j, i: (1, j)),
        ] + x_specs + [
            pl.BlockSpec((None, 1, tn), lambda j, i: (_mod_row(i, tm), 0, 2 * nj + j)),
        ],
        out_specs=pl.BlockSpec((tm, tn), lambda j, i: (i, j)),
        out_shape=jax.ShapeDtypeStruct((n_rows, D_MODEL), F32),
        compiler_params=_cparams(("arbitrary", "arbitrary")),
        name="out_proj",
    )(*attn, mix, w_bf, w_bf, *src, mod3)


def _router_kernel(x_ref, g_ref, sh_ref, sc_ref, rw_ref, rb_ref, tri_ref,
                   hp_ref, idx_ref, rank_ref, gate_ref, cnt_ref, run_s):
    @pl.when(pl.program_id(0) == 0)
    def _():
        run_s[...] = jnp.zeros_like(run_s)

    h = _rms(x_ref[...], g_ref[...]) * (1.0 + sc_ref[...]) + sh_ref[...]
    h_hi = h.astype(BF16)
    hb = h_hi.astype(F32)
    hi = lax.bitcast_convert_type(hb[:, :HALF_D], U32)
    lo = lax.bitcast_convert_type(hb[:, HALF_D:], U32)
    packed = hi | (lo >> 16)
    for j in range(PACK_TILES):
        hp_ref[pl.ds(j, packed.shape[0], stride=PACK_TILES), :] = packed[:, j * LANES:(j + 1) * LANES]

    h_lo = (h - hb).astype(BF16)
    both = jnp.dot(h_hi, rw_ref[...], preferred_element_type=F32)
    tail = jnp.dot(h_lo, rw_ref[:, 0:LANES], preferred_element_type=F32)
    logits = (both[:, 0:LANES] + both[:, LANES:]) + tail
    lt = (logits + rb_ref[...]).T[0:N_EXPERTS, :]
    expert = lax.broadcasted_iota(jnp.int32, lt.shape, 0).astype(F32)
    vals, idxs = [], []
    for _ in range(TOP_K):
        m = jnp.max(lt, axis=0, keepdims=True)
        idx = jnp.min(jnp.where(lt == m, expert, float(N_EXPERTS)), axis=0, keepdims=True)
        vals.append(m)
        idxs.append(idx)
        lt = jnp.where(expert == idx, -jnp.inf, lt)

    e = [jnp.exp(v - vals[0]) for v in vals]
    den = e[0] + e[1] + e[2] + e[3]
    slot = lax.broadcasted_iota(jnp.int32, (LANES, lt.shape[1]), 0)
    gates = jnp.zeros((LANES, lt.shape[1]), F32)
    for k in range(TOP_K):
        gates = jnp.where(slot == k, e[k] / den, gates)
    gate_ref[...] = gates.T

    base = run_s[...]
    for k in range(TOP_K):
        onehot = jnp.where(expert == idxs[k], 1.0, 0.0)
        before = jnp.dot(onehot.astype(BF16), tri_ref[...], preferred_element_type=F32)
        rank = jnp.sum(onehot * (before + base[:, 0:1]), axis=0, keepdims=True)
        idx_ref[k:k + 1, :] = idxs[k].astype(jnp.int32)
        rank_ref[k:k + 1, :] = rank.astype(jnp.int32)
        base = base + jnp.sum(onehot, axis=1, keepdims=True)
    run_s[...] = base
    cnt_ref[...] = base.astype(jnp.int32)


def _router(tok, norm_g, mod3, rw_pad, rb_pad, tri, n_rows):
    tm = ROUTER_TM
    return pl.pallas_call(
        _router_kernel,
        grid=(n_rows // tm,),
        in_specs=[
            pl.BlockSpec((tm, D_MODEL), lambda i: (i, 0)),
            pl.BlockSpec((1, D_MODEL), lambda i: (0, 0)),
            pl.BlockSpec((None, 1, D_MODEL), lambda i: (_mod_row(i, tm), 0, 3)),
            pl.BlockSpec((None, 1, D_MODEL), lambda i: (_mod_row(i, tm), 0, 4)),
            pl.BlockSpec((D_MODEL, 2 * LANES), lambda i: (0, 0)),
            pl.BlockSpec((1, LANES), lambda i: (0, 0)),
            pl.BlockSpec((tm, tm), lambda i: (0, 0)),
        ],
        out_specs=[
            pl.BlockSpec((tm * PACK_TILES, LANES), lambda i: (i, 0)),
            pl.BlockSpec((TOP_K, tm), lambda i: (0, i)),
            pl.BlockSpec((TOP_K, tm), lambda i: (0, i)),
            pl.BlockSpec((tm, LANES), lambda i: (i, 0)),
            pl.BlockSpec((N_EXPERTS, LANES), lambda i: (0, 0)),
        ],
        out_shape=[
            jax.ShapeDtypeStruct((n_rows * PACK_TILES, LANES), U32),
            jax.ShapeDtypeStruct((TOP_K, n_rows), jnp.int32),
            jax.ShapeDtypeStruct((TOP_K, n_rows), jnp.int32),
            jax.ShapeDtypeStruct((n_rows, LANES), F32),
            jax.ShapeDtypeStruct((N_EXPERTS, LANES), jnp.int32),
        ],
        scratch_shapes=[pltpu.VMEM((N_EXPERTS, LANES), F32)],
        compiler_params=_cparams(("arbitrary",)),
        name="router",
    )(tok, norm_g.reshape(1, D_MODEL), mod3, mod3, rw_pad, rb_pad, tri)


def _dispatch_kernel(dest_ref, pad_ref, hp_ref, xs_hbm, sem, pad_sem, *, rows, n_rows):
    i = pl.program_id(0)

    @pl.when(i == 0)
    def _():
        def per_expert(e, carry):
            lo, hi = pad_ref[e], pad_ref[N_EXPERTS + 1 + e]

            def start(p, c):
                pltpu.make_async_copy(hp_ref.at[0], xs_hbm.at[p], pad_sem).start()
                return c

            def wait(p, c):
                pltpu.make_async_copy(hp_ref.at[0], xs_hbm.at[p], pad_sem).wait()
                return c

            lax.fori_loop(lo, hi, start, 0)
            lax.fori_loop(lo, hi, wait, 0)
            return carry

        lax.fori_loop(0, N_EXPERTS + 1, per_expert, 0)

    base = i * rows

    def issue(r2, carry):
        for u in range(2):
            r = 2 * r2 + u
            for k in range(TOP_K):
                d = dest_ref[k * n_rows + base + r]
                pltpu.make_async_copy(hp_ref.at[r], xs_hbm.at[d], sem).start(priority=k % 2)
        return carry

    lax.fori_loop(0, rows // 2, issue, 0)
    for k in range(TOP_K):
        pltpu.make_async_copy(hp_ref, xs_hbm.at[pl.ds(0, rows)], sem).wait()


def _dispatch(dest_flat, pad_meta, hp, n_pad, n_rows):
    rows = 1024
    return pl.pallas_call(
        functools.partial(_dispatch_kernel, rows=rows, n_rows=n_rows),
        grid_spec=pltpu.PrefetchScalarGridSpec(
            num_scalar_prefetch=2,
            grid=(n_rows // rows,),
            in_specs=[pl.BlockSpec((rows, PACK_TILES, LANES), lambda i, d, p: (i, 0, 0))],
            out_specs=pl.BlockSpec(memory_space=pl.ANY),
            scratch_shapes=[pltpu.SemaphoreType.DMA(()), pltpu.SemaphoreType.DMA(())],
        ),
        out_shape=jax.ShapeDtypeStruct((n_pad, PACK_TILES, LANES), U32),
        compiler_params=_cparams(("arbitrary",)),
        name="dispatch",
    )(dest_flat, pad_meta, hp)


def _is_new_expert(meta_ref, i):
    prev = meta_ref[jnp.maximum(i - 1, 0)]
    return jnp.logical_or(i == 0, meta_ref[i] != prev)


def _cast_rows(src_ref, dst_ref, chunk=256):
    def body(c, carry):
        rows = pl.ds(pl.multiple_of(c * chunk, chunk), chunk)
        dst_ref[rows, :] = src_ref[rows, :].astype(dst_ref.dtype)
        return carry

    lax.fori_loop(0, src_ref.shape[0] // chunk, body, 0)


def _for_valid_rows(valid, fn):
    lo = 0
    for n in EXPERT_ROW_STEPS:
        @pl.when(jnp.logical_and(valid > lo, valid <= n))
        def _(n=n):
            fn(n)

        lo = n


def _unpack_rows(x_ref, n):
    xp = jnp.concatenate([x_ref[pl.ds(j, n, stride=PACK_TILES), :] for j in range(PACK_TILES)], axis=1)
    hi = lax.bitcast_convert_type(xp & jnp.uint32(0xFFFF0000), F32).astype(BF16)
    lo = lax.bitcast_convert_type(xp << 16, F32).astype(BF16)
    return jnp.concatenate([hi, lo], axis=1)


def _expert_gu_kernel(meta_ref, x_ref, w_hbm, bg_ref, bu_ref, o_ref, wg_f, wu_f, wg_s, wu_s, sems,
                      *, n_blocks, layer):
    j, i = pl.program_id(0), pl.program_id(1)
    tn = wg_f.shape[1]
    nj = pl.num_programs(0)

    def fetch(e, jj):
        col = pl.multiple_of(jj * tn, tn)
        return (pltpu.make_async_copy(w_hbm.at[layer, e, :, pl.ds(col, tn)], wg_f, sems.at[0]),
                pltpu.make_async_copy(w_hbm.at[layer, e, :, pl.ds(EXPERT_FF + col, tn)], wu_f, sems.at[1]))

    @pl.when(jnp.logical_and(j == 0, i == 0))
    def _():
        for cp in fetch(meta_ref[0], 0):
            cp.start()

    active = i < meta_ref[n_blocks]

    @pl.when(jnp.logical_and(active, _is_new_expert(meta_ref, i)))
    def _():
        for cp in fetch(meta_ref[i], j):
            cp.wait()
        _cast_rows(wg_f, wg_s)
        _cast_rows(wu_f, wu_s)
        nxt = meta_ref[n_blocks + 1 + i]

        @pl.when(nxt >= 0)
        def _():
            for cp in fetch(nxt, j):
                cp.start()

        @pl.when(jnp.logical_and(nxt < 0, j + 1 < nj))
        def _():
            for cp in fetch(meta_ref[0], j + 1):
                cp.start()

    def compute(n):
        x = _unpack_rows(x_ref, n)
        g = jnp.dot(x, wg_s[...], preferred_element_type=F32) + bg_ref[...]
        u = jnp.dot(x, wu_s[...], preferred_element_type=F32) + bu_ref[...]
        g = jnp.minimum(g, SWIGLU_LIMIT)
        u = jnp.clip(u, -SWIGLU_LIMIT, SWIGLU_LIMIT)
        o_ref[0:n, :] = ((u + 1.0) * (g * _sigmoid(g * SWIGLU_ALPHA))).astype(o_ref.dtype)
        if n < o_ref.shape[0]:
            o_ref[n:, :] = jnp.zeros((o_ref.shape[0] - n, o_ref.shape[1]), o_ref.dtype)

    _for_valid_rows(meta_ref[2 * n_blocks + 1 + i], compute)

    @pl.when(jnp.logical_not(active))
    def _():
        o_ref[...] = jnp.zeros_like(o_ref)


def _active_block(i, m, n_blocks):
    return jnp.minimum(i, m[n_blocks] - 1)


def _expert_gu(meta, xs, w_gu, b_gu, layer, n_blocks):
    tm, tn = EXPERT_TM, 1024
    nj = EXPERT_FF // tn
    b3 = b_gu.reshape(DEPTH, N_EXPERTS, 1, 2 * EXPERT_FF)
    return pl.pallas_call(
        functools.partial(_expert_gu_kernel, n_blocks=n_blocks, layer=layer),
        grid_spec=pltpu.PrefetchScalarGridSpec(
            num_scalar_prefetch=1,
            grid=(nj, n_blocks),
            in_specs=[
                pl.BlockSpec((tm * PACK_TILES, LANES), lambda j, i, m: (_active_block(i, m, n_blocks), 0)),
                pl.BlockSpec(memory_space=pl.ANY),
                pl.BlockSpec((None, None, 1, tn), lambda j, i, m: (layer, m[i], 0, j)),
                pl.BlockSpec((None, None, 1, tn), lambda j, i, m: (layer, m[i], 0, nj + j)),
            ],
            out_specs=pl.BlockSpec((tm, tn), lambda j, i, m: (i, j)),
            scratch_shapes=[pltpu.VMEM((D_MODEL, tn), F32), pltpu.VMEM((D_MODEL, tn), F32),
                            pltpu.VMEM((D_MODEL, tn), BF16), pltpu.VMEM((D_MODEL, tn), BF16),
                            pltpu.SemaphoreType.DMA((2,))],
        ),
        out_shape=jax.ShapeDtypeStruct((n_blocks * tm, EXPERT_FF), BF16),
        compiler_params=_cparams(("arbitrary", "arbitrary")),
        name="expert_gu",
    )(meta, xs, w_gu, b3, b3)


def _expert_down_kernel(meta_ref, a_ref, w_hbm, b_ref, o_ref, w_f, w_s, sem, *, n_blocks, layer):
    i = pl.program_id(0)

    def fetch(e):
        return pltpu.make_async_copy(w_hbm.at[layer, e], w_f, sem)

    @pl.when(i == 0)
    def _():
        fetch(meta_ref[0]).start()

    active = i < meta_ref[n_blocks]

    @pl.when(jnp.logical_and(active, _is_new_expert(meta_ref, i)))
    def _():
        fetch(meta_ref[i]).wait()
        _cast_rows(w_f, w_s)
        nxt = meta_ref[n_blocks + 1 + i]

        @pl.when(nxt >= 0)
        def _():
            fetch(nxt).start()

    def compute(n):
        o_ref[0:n, :] = jnp.dot(a_ref[0:n, :], w_s[...], preferred_element_type=F32) + b_ref[...]
        if n < o_ref.shape[0]:
            o_ref[n:, :] = jnp.zeros((o_ref.shape[0] - n, o_ref.shape[1]), o_ref.dtype)

    _for_valid_rows(meta_ref[2 * n_blocks + 1 + i], compute)

    @pl.when(jnp.logical_not(active))
    def _():
        o_ref[...] = jnp.zeros_like(o_ref)


def _expert_down(meta, act, w_down, b_down, layer, n_blocks):
    tm = EXPERT_TM
    b3 = b_down.reshape(DEPTH, N_EXPERTS, 1, D_MODEL)
    return pl.pallas_call(
        functools.partial(_expert_down_kernel, n_blocks=n_blocks, layer=layer),
        grid_spec=pltpu.PrefetchScalarGridSpec(
            num_scalar_prefetch=1,
            grid=(n_blocks,),
            in_specs=[
                pl.BlockSpec((tm, EXPERT_FF), lambda i, m: (_active_block(i, m, n_blocks), 0)),
                pl.BlockSpec(memory_space=pl.ANY),
                pl.BlockSpec((None, None, 1, D_MODEL), lambda i, m: (layer, m[i], 0, 0)),
            ],
            out_specs=pl.BlockSpec((tm, D_MODEL), lambda i, m: (i, 0)),
            scratch_shapes=[pltpu.VMEM((EXPERT_FF, D_MODEL), F32), pltpu.VMEM((EXPERT_FF, D_MODEL), BF16),
                            pltpu.SemaphoreType.DMA(())],
        ),
        out_shape=jax.ShapeDtypeStruct((n_blocks * tm, D_MODEL), F32),
        compiler_params=_cparams(("arbitrary",)),
        name="expert_down",
    )(meta, act, w_down, b3)


def _combine_kernel(dest_ref, y_hbm, gt_ref, x_ref, g_ref, o_ref, buf, sems, *, rows, n_rows):
    i = pl.program_id(0)
    n_steps = pl.num_programs(0)

    def issue(step, slot):
        base = step * rows

        def body(r8, carry):
            r0 = pl.multiple_of(r8 * 8, 8)
            for u in range(8):
                for k in range(TOP_K):
                    d = dest_ref[k * n_rows + base + r0 + u]
                    pltpu.make_async_copy(y_hbm.at[pl.ds(d, 1), :],
                                          buf.at[slot, k, pl.ds(r0, 8), :].at[pl.ds(u, 1), :],
                                          sems.at[slot]).start(priority=k % 2)
            return carry

        lax.fori_loop(0, rows // 8, body, 0)

    @pl.when(i == 0)
    def _():
        issue(0, 0)

    @pl.when(i + 1 < n_steps)
    def _():
        issue(i + 1, (i + 1) % 2)

    slot = i % 2
    for k in range(TOP_K):
        pltpu.make_async_copy(y_hbm.at[pl.ds(0, rows), :], buf.at[slot, k], sems.at[slot]).wait()
    gt = gt_ref[...]
    moe = gt[:, 0:1] * buf[slot, 0]
    for k in range(1, TOP_K):
        moe = moe + gt[:, k:k + 1] * buf[slot, k]
    o_ref[...] = x_ref[...] + g_ref[...] * moe


def _combine(dest_flat, ys, gates_t, tok, mod3, n_rows):
    rows = 128
    return pl.pallas_call(
        functools.partial(_combine_kernel, rows=rows, n_rows=n_rows),
        grid_spec=pltpu.PrefetchScalarGridSpec(
            num_scalar_prefetch=1,
            grid=(n_rows // rows,),
            in_specs=[
                pl.BlockSpec(memory_space=pl.ANY),
                pl.BlockSpec((rows, LANES), lambda i, d: (i, 0)),
                pl.BlockSpec((rows, D_MODEL), lambda i, d: (i, 0)),
                pl.BlockSpec((None, 1, D_MODEL), lambda i, d: (_mod_row(i, rows), 0, 5)),
            ],
            out_specs=pl.BlockSpec((rows, D_MODEL), lambda i, d: (i, 0)),
            scratch_shapes=[pltpu.VMEM((2, TOP_K, rows, D_MODEL), F32), pltpu.SemaphoreType.DMA((2,))],
        ),
        out_shape=jax.ShapeDtypeStruct((n_rows, D_MODEL), F32),
        compiler_params=_cparams(("arbitrary",)),
        name="combine",
    )(dest_flat, ys, gates_t, tok, mod3)


def _routing(top_idx, rank, counts, n_rows):
    tm = EXPERT_TM
    n_blocks = -(-(TOP_K * n_rows) // tm) + N_EXPERTS
    counts = counts[:, 0]
    padded = (counts + tm - 1) // tm * tm
    padded_end = jnp.cumsum(padded)
    padded_start = padded_end - padded
    experts = jnp.arange(N_EXPERTS, dtype=jnp.int32)
    start_of = jnp.sum(jnp.where(top_idx[:, :, None] == experts, padded_start, 0), axis=-1)
    dest = (start_of + rank).astype(jnp.int32).reshape(TOP_K * n_rows)
    block_start = jnp.arange(n_blocks, dtype=jnp.int32) * tm
    block_expert = jnp.minimum(jnp.sum(padded_end[None, :] <= block_start[:, None], axis=1), N_EXPERTS - 1)
    n_active = padded_end[-1:] // tm
    group_end = jnp.sum(jnp.where(block_expert[:, None] == experts, padded_end // tm, 0), axis=1)
    follower = jnp.sum(jnp.where(group_end[:, None] == jnp.arange(n_blocks)[None, :], block_expert, 0), axis=1)
    next_expert = jnp.where(group_end < n_active, follower, -1)
    is_expert = block_expert[:, None] == experts
    block_count = jnp.sum(jnp.where(is_expert, counts, 0), axis=1)
    block_first = jnp.sum(jnp.where(is_expert, padded_start, 0), axis=1)
    block_valid = jnp.clip(block_count - (block_start - block_first), 0, tm)
    meta = jnp.concatenate([block_expert, n_active, next_expert, block_valid]).astype(jnp.int32)
    n_pad = jnp.full((1,), n_blocks * tm, jnp.int32)
    pad_meta = jnp.concatenate([padded_start + counts, padded_end[-1:], padded_end, n_pad]).astype(jnp.int32)
    return dest, meta, pad_meta, n_blocks


def _rope_tables():
    rows = SEQ // GRID_W
    row = jnp.broadcast_to(jnp.arange(rows, dtype=F32)[:, None], (rows, GRID_W)).reshape(-1)
    col = jnp.broadcast_to(jnp.arange(GRID_W, dtype=F32)[None, :], (rows, GRID_W)).reshape(-1)
    inv_freq = ROPE_THETA ** (-jnp.arange(ROPE_FREQS, dtype=F32) / ROPE_FREQS)
    ang_r = row[:, None] * inv_freq
    ang_c = col[:, None] * inv_freq
    cos = jnp.concatenate([jnp.cos(ang_r), jnp.cos(ang_r), jnp.cos(ang_c), jnp.cos(ang_c)], axis=-1)
    sin = jnp.concatenate([-jnp.sin(ang_r), jnp.sin(ang_r), -jnp.sin(ang_c), jnp.sin(ang_c)], axis=-1)
    return cos, sin


def kernel(x, c, ctx, c_ctx, ada_w, ada_b, norm1_g, norm2_g, w_in, q_norm_g, k_norm_g, pool_w, pool_scale,
           sgu_norm_g, sgu_w, sgu_b, w_out, router_w, router_b, w_gu, b_gu, w_down, b_down):
    cos, sin = _rope_tables()
    tok = (x.reshape(N_LAT, D_MODEL), ctx.reshape(N_CTX, D_MODEL))
    cc = jnp.zeros((MOD_ROWS, D_MODEL), F32).at[:BATCH].set(c).at[BATCH].set(c_ctx)
    mod = _adaln(cc, ada_w, ada_b)
    tri = jnp.triu(jnp.ones((ROUTER_TM, ROUTER_TM), BF16), k=1)
    for l in range(DEPTH):
        last = l == DEPTH - 1
        n_rows = N_LAT if last else N_TOK
        mod3 = mod[l].reshape(MOD_ROWS, 1, 6 * D_MODEL)
        proj = _in_proj(tok, norm1_g[l], mod3, w_in[l].astype(BF16))
        attn = _latent_attention(proj, cos, sin, q_norm_g[l], k_norm_g[l])
        if not last:
            attn = (attn, _context_attention(proj, q_norm_g[l], k_norm_g[l]))
        sgu_b_full = jnp.broadcast_to(sgu_b[l][:, :, None], (N_SGU_HEADS, CHUNK, LANES))
        mix = _mixers(proj, pool_w[l].astype(BF16), pool_scale[l], sgu_norm_g[l], sgu_w[l].astype(BF16), sgu_b_full)
        tok = _out_proj(attn, mix, w_out[l].astype(BF16), tok, mod3, n_rows)
        rw_hi = router_w[l].astype(BF16)
        rw_lo = (router_w[l] - rw_hi.astype(F32)).astype(BF16)
        rw_pad = (jnp.zeros((D_MODEL, 2 * LANES), BF16).at[:, :N_EXPERTS].set(rw_hi)
                  .at[:, LANES:LANES + N_EXPERTS].set(rw_lo))
        rb_pad = jnp.zeros((1, LANES), F32).at[0, :N_EXPERTS].set(router_b[l])
        hp, top_idx, rank, gates_t, counts = _router(tok, norm2_g[l], mod3, rw_pad, rb_pad, tri, n_rows)
        dest, meta, pad_meta, n_blocks = _routing(top_idx, rank, counts, n_rows)
        n_pad = n_blocks * EXPERT_TM
        xs = _dispatch(dest, pad_meta, hp.reshape(n_rows, PACK_TILES, LANES), n_pad, n_rows)
        act = _expert_gu(meta, xs.reshape(n_pad * PACK_TILES, LANES), w_gu, b_gu, l, n_blocks)
        ys = _expert_down(meta, act, w_down, b_down, l, n_blocks)
        tok = _combine(dest, ys, gates_t, tok, mod3, n_rows)
    return tok.reshape(BATCH, SEQ, D_MODEL)
```

```python
import functools

import jax
import jax.numpy as jnp
from jax import lax
from jax.experimental import pallas as pl
from jax.experimental.pallas import tpu as pltpu

F32 = jnp.float32
BF16 = jnp.bfloat16
U32 = jnp.uint32

D_MODEL = 2048
BATCH = 4
SEQ = 4096
DEPTH = 2
GRID_W = 64
CTX_LEN = 256
HEAD_DIM = 128
ATTN_W = 1024
GQA_GROUP = 4
N_KV_HEADS = 2
KV_W = 256
ROPE_THETA = 10000.0
ROPE_FREQS = 32
ATTN_SCALE = HEAD_DIM ** -0.5
LOG2_E = 1.4426950408889634
POOL_WINDOWS = (2, 4, 8, 16)
POOL_W = 512
SGU_W = 512
N_SGU_HEADS = 4
CHUNK = 128
IN_W = 3072
N_EXPERTS = 32
TOP_K = 4
EXPERT_FF = 2048
SWIGLU_LIMIT = 7.0
SWIGLU_ALPHA = 1.702
EPS = 1e-6

N_LAT = BATCH * SEQ
N_CTX = BATCH * CTX_LEN
N_TOK = N_LAT + N_CTX
MOD_ROWS = 8
LANES = 128
POOL_HALO = 8
MIX_TM = 256
EXPERT_TM = 256
EXPERT_ROW_STEPS = (64, 128, 192, 256)
ROUTER_TM = 512
HALF_D = D_MODEL // 2
PACK_TILES = HALF_D // LANES
VMEM_LIMIT = 56 * 1024 * 1024


def _cparams(sem, vmem=VMEM_LIMIT):
    return pltpu.CompilerParams(dimension_semantics=sem, vmem_limit_bytes=vmem)


def _mod_row(row_tile, tm):
    return jnp.minimum(row_tile * tm // SEQ, BATCH)


def _rms(x, g):
    return x * lax.rsqrt(jnp.mean(x * x, axis=-1, keepdims=True) + EPS) * g


def _sigmoid(x):
    return 1.0 / (1.0 + jnp.exp(-x))


def _gelu(x):
    return 0.5 * x * (1.0 + lax.erf(x * 0.7071067811865476))


def _adaln_kernel(cc_ref, w_ref, b_ref, o_ref):
    cc = cc_ref[...]
    s = (cc * _sigmoid(cc)).astype(BF16)
    o_ref[...] = jnp.dot(s, w_ref[...].astype(BF16), preferred_element_type=F32) + b_ref[...]


def _adaln(cc, ada_w, ada_b):
    tn = 1024
    n = 6 * D_MODEL
    return pl.pallas_call(
        _adaln_kernel,
        grid=(DEPTH, n // tn),
        in_specs=[
            pl.BlockSpec((MOD_ROWS, D_MODEL), lambda l, j: (0, 0)),
            pl.BlockSpec((None, D_MODEL, tn), lambda l, j: (l, 0, j)),
            pl.BlockSpec((None, 1, tn), lambda l, j: (l, 0, j)),
        ],
        out_specs=pl.BlockSpec((None, MOD_ROWS, tn), lambda l, j: (l, 0, j)),
        out_shape=jax.ShapeDtypeStruct((DEPTH, MOD_ROWS, n), F32),
        compiler_params=_cparams(("arbitrary", "arbitrary")),
        name="adaln",
    )(cc, ada_w, ada_b.reshape(DEPTH, 1, n))


def _proj_kernel(*refs, n_first):
    if n_first is None:
        (xa_ref, g_ref, sh_ref, sc_ref, w_ref, o_ref, h_s), xb_ref = refs, None
    else:
        xa_ref, xb_ref, g_ref, sh_ref, sc_ref, w_ref, o_ref, h_s = refs
    i, j = pl.program_id(0), pl.program_id(1)

    def prepare(tile, slot):
        x = xa_ref[...]
        if xb_ref is not None:
            x = jnp.where(tile < n_first, x, xb_ref[...])
        y = _rms(x, g_ref[...])
        h_s[slot] = (y * (1.0 + sc_ref[...]) + sh_ref[...]).astype(BF16)

    @pl.when(jnp.logical_and(i == 0, j == 0))
    def _():
        prepare(0, 0)

    last = j == pl.num_programs(1) - 1

    @pl.when(last)
    def _():
        o_ref[...] = jnp.dot(h_s[i % 2], w_ref[...], preferred_element_type=F32)
        prepare(i + 1, (i + 1) % 2)

    @pl.when(jnp.logical_not(last))
    def _():
        o_ref[...] = jnp.dot(h_s[i % 2], w_ref[...], preferred_element_type=F32)


def _in_proj(src, norm_g, mod3, w_bf):
    tm, tn = 512, 1024
    n_tiles, nj = N_TOK // tm, IN_W // tn

    def tile(i, j):
        return jnp.minimum(i + (j + 1) // nj, n_tiles - 1)

    if isinstance(src, tuple):
        n_first = N_LAT // tm
        x_specs = [pl.BlockSpec((tm, D_MODEL), lambda i, j: (jnp.minimum(tile(i, j), n_first - 1), 0)),
                   pl.BlockSpec((tm, D_MODEL), lambda i, j: (jnp.maximum(tile(i, j) - n_first, 0), 0))]
    else:
        n_first, src = None, (src,)
        x_specs = [pl.BlockSpec((tm, D_MODEL), lambda i, j: (tile(i, j), 0))]
    return pl.pallas_call(
        functools.partial(_proj_kernel, n_first=n_first),
        grid=(n_tiles, nj),
        in_specs=x_specs + [
            pl.BlockSpec((1, D_MODEL), lambda i, j: (0, 0)),
            pl.BlockSpec((None, 1, D_MODEL), lambda i, j: (_mod_row(tile(i, j), tm), 0, 0)),
            pl.BlockSpec((None, 1, D_MODEL), lambda i, j: (_mod_row(tile(i, j), tm), 0, 1)),
            pl.BlockSpec((D_MODEL, tn), lambda i, j: (0, j)),
        ],
        out_specs=pl.BlockSpec((tm, tn), lambda i, j: (i, j)),
        out_shape=jax.ShapeDtypeStruct((N_TOK, IN_W), F32),
        scratch_shapes=[pltpu.VMEM((2, tm, D_MODEL), BF16)],
        compiler_params=_cparams(("arbitrary", "arbitrary")),
        name="in_proj",
    )(*src, norm_g.reshape(1, D_MODEL), mod3, mod3, w_bf)


def _rope(x, cos, sin):
    lane = lax.broadcasted_iota(jnp.int32, x.shape, 1)
    first = (lane % 64) < 32
    partner = jnp.where(first, pltpu.roll(x, 96, 1), pltpu.roll(x, 32, 1))
    return x * cos + partner * sin


def _softmax_pv(q, k, v):
    s = lax.dot_general(q, k, (((1,), (1,)), ((), ())), preferred_element_type=F32) * ATTN_SCALE
    m = jnp.max(s, axis=-1, keepdims=True)
    p = jnp.exp(s - m)
    l = jnp.sum(p, axis=-1, keepdims=True)
    return jnp.dot(p.astype(BF16), v, preferred_element_type=F32) / l


def _lat_attn_kernel(q_ref, kl_ref, vl_ref, kc_ref, vc_ref, cosq_ref, sinq_ref, cosk_ref, sink_ref,
                     qg_ref, kg_ref, o_ref, k_s, v_s, s_s):
    @pl.when(pl.program_id(2) == 0)
    def _():
        k_s[0:CTX_LEN, :] = _rms(kc_ref[...], kg_ref[...]).astype(BF16)
        kl = _rope(_rms(kl_ref[...], kg_ref[...]), cosk_ref[...], sink_ref[...])
        k_s[CTX_LEN:, :] = kl.astype(BF16)
        v_s[0:CTX_LEN, 0:HEAD_DIM] = vc_ref[...].astype(BF16)
        v_s[CTX_LEN:, 0:HEAD_DIM] = vl_ref[...].astype(BF16)
        v_s[:, HEAD_DIM:] = jnp.ones((CTX_LEN + SEQ, LANES), BF16)

    def scores(g):
        lanes = slice(g * HEAD_DIM, (g + 1) * HEAD_DIM)
        q = _rope(_rms(q_ref[:, lanes], qg_ref[...]), cosq_ref[...], sinq_ref[...])
        q = (q * (ATTN_SCALE * LOG2_E)).astype(BF16)
        s_s[g % 2] = lax.dot_general(q, k_s[...], (((1,), (1,)), ((), ())), preferred_element_type=F32)

    scores(0)
    for g in range(GQA_GROUP):
        if g + 1 < GQA_GROUP:
            scores(g + 1)
        s = s_s[g % 2]
        p = jnp.exp2(s - jnp.max(s, axis=-1, keepdims=True))
        ol = jnp.dot(p.astype(BF16), v_s[...], preferred_element_type=F32)
        o = ol[:, 0:HEAD_DIM] / ol[:, HEAD_DIM:]
        o_ref[:, g * HEAD_DIM:(g + 1) * HEAD_DIM] = o.astype(o_ref.dtype)


def _latent_attention(proj, cos, sin, q_g, k_g):
    tq = 256
    nq = SEQ // tq
    qw = GQA_GROUP * HEAD_DIM
    kcol = ATTN_W // HEAD_DIM
    vcol = (ATTN_W + KV_W) // HEAD_DIM
    ctx_blk = N_LAT // CTX_LEN
    return pl.pallas_call(
        _lat_attn_kernel,
        grid=(BATCH, N_KV_HEADS, nq),
        in_specs=[
            pl.BlockSpec((tq, qw), lambda b, h, i: (b * nq + i, h)),
            pl.BlockSpec((SEQ, HEAD_DIM), lambda b, h, i: (b, kcol + h)),
            pl.BlockSpec((SEQ, HEAD_DIM), lambda b, h, i: (b, vcol + h)),
            pl.BlockSpec((CTX_LEN, HEAD_DIM), lambda b, h, i: (ctx_blk + b, kcol + h)),
            pl.BlockSpec((CTX_LEN, HEAD_DIM), lambda b, h, i: (ctx_blk + b, vcol + h)),
            pl.BlockSpec((tq, HEAD_DIM), lambda b, h, i: (i, 0)),
            pl.BlockSpec((tq, HEAD_DIM), lambda b, h, i: (i, 0)),
            pl.BlockSpec((SEQ, HEAD_DIM), lambda b, h, i: (0, 0)),
            pl.BlockSpec((SEQ, HEAD_DIM), lambda b, h, i: (0, 0)),
            pl.BlockSpec((1, HEAD_DIM), lambda b, h, i: (0, 0)),
            pl.BlockSpec((1, HEAD_DIM), lambda b, h, i: (0, 0)),
        ],
        out_specs=pl.BlockSpec((tq, qw), lambda b, h, i: (b * nq + i, h)),
        out_shape=jax.ShapeDtypeStruct((N_LAT, ATTN_W), BF16),
        scratch_shapes=[pltpu.VMEM((CTX_LEN + SEQ, HEAD_DIM), BF16),
                        pltpu.VMEM((CTX_LEN + SEQ, HEAD_DIM + LANES), BF16),
                        pltpu.VMEM((2, tq, CTX_LEN + SEQ), F32)],
        compiler_params=_cparams(("arbitrary", "arbitrary", "arbitrary")),
        name="latent_attention",
    )(proj, proj, proj, proj, proj, cos, sin, cos, sin, q_g.reshape(1, HEAD_DIM), k_g.reshape(1, HEAD_DIM))


def _ctx_attn_kernel(q_ref, k_ref, v_ref, qg_ref, kg_ref, o_ref):
    k = _rms(k_ref[...], kg_ref[...]).astype(BF16)
    v = v_ref[...].astype(BF16)
    for g in range(GQA_GROUP):
        lanes = slice(g * HEAD_DIM, (g + 1) * HEAD_DIM)
        q = _rms(q_ref[:, lanes], qg_ref[...]).astype(BF16)
        o_ref[:, lanes] = _softmax_pv(q, k, v).astype(o_ref.dtype)


def _context_attention(proj, q_g, k_g):
    qw = GQA_GROUP * HEAD_DIM
    kcol = ATTN_W // HEAD_DIM
    vcol = (ATTN_W + KV_W) // HEAD_DIM
    ctx_blk = N_LAT // CTX_LEN
    return pl.pallas_call(
        _ctx_attn_kernel,
        grid=(BATCH, N_KV_HEADS),
        in_specs=[
            pl.BlockSpec((CTX_LEN, qw), lambda b, h: (ctx_blk + b, h)),
            pl.BlockSpec((CTX_LEN, HEAD_DIM), lambda b, h: (ctx_blk + b, kcol + h)),
            pl.BlockSpec((CTX_LEN, HEAD_DIM), lambda b, h: (ctx_blk + b, vcol + h)),
            pl.BlockSpec((1, HEAD_DIM), lambda b, h: (0, 0)),
            pl.BlockSpec((1, HEAD_DIM), lambda b, h: (0, 0)),
        ],
        out_specs=pl.BlockSpec((CTX_LEN, qw), lambda b, h: (b, h)),
        out_shape=jax.ShapeDtypeStruct((N_CTX, ATTN_W), BF16),
        compiler_params=_cparams(("arbitrary", "arbitrary")),
        name="context_attention",
    )(proj, proj, proj, q_g.reshape(1, HEAD_DIM), k_g.reshape(1, HEAD_DIM))


def _mixer_kernel(pin_ref, prev_ref, next_ref, su_ref, sv_ref, pw_ref, ps_ref, sg_ref, sw_ref, sb_ref,
                  o_ref, pad_ref):
    tm = MIX_TM
    i = pl.program_id(0)
    is_lat = i < N_LAT // tm
    pos0 = jnp.where(is_lat, (i % (SEQ // tm)) * tm, 0)
    seq_len = jnp.where(is_lat, SEQ, CTX_LEN)
    pad_ref[0:POOL_HALO, :] = jnp.where(pos0 == 0, 0.0, prev_ref[...])
    pad_ref[POOL_HALO:POOL_HALO + tm, :] = pin_ref[...]
    pad_ref[POOL_HALO + tm:, :] = jnp.where(pos0 + tm == seq_len, 0.0, next_ref[...])
    t = pos0 + lax.broadcasted_iota(jnp.int32, (tm, LANES), 0)
    for gi, w in enumerate(POOL_WINDOWS):
        lanes = slice(gi * LANES, (gi + 1) * LANES)
        acc = pad_ref[POOL_HALO - w // 2:POOL_HALO - w // 2 + tm, lanes]
        for d in range(-w // 2 + 1, w // 2):
            acc = acc + pad_ref[POOL_HALO + d:POOL_HALO + d + tm, lanes]
        cnt = (jnp.minimum(t + w // 2, seq_len) - jnp.maximum(t - w // 2, 0)).astype(F32)
        mixed = acc / cnt - pin_ref[:, lanes]
        y = jnp.dot(mixed.astype(BF16), pw_ref[gi], preferred_element_type=F32) * ps_ref[:, lanes]
        o_ref[:, lanes] = y.astype(o_ref.dtype)

    for h in range(N_SGU_HEADS):
        lanes = slice(h * LANES, (h + 1) * LANES)
        gu = _gelu(su_ref[:, lanes])
        vh = _rms(_gelu(sv_ref[:, lanes]), sg_ref[h:h + 1, :]).astype(BF16)
        for n in range(tm // CHUNK):
            rows = slice(n * CHUNK, (n + 1) * CHUNK)
            mixed = jnp.dot(sw_ref[h], vh[rows], preferred_element_type=F32) + sb_ref[h]
            o_ref[rows, POOL_W + h * LANES:POOL_W + (h + 1) * LANES] = (gu[rows] * mixed).astype(o_ref.dtype)


def _mixers(proj, pool_w_bf, pool_scale, sgu_norm_g, sgu_w_bf, sgu_b_full):
    tm = MIX_TM
    per_tile = tm // POOL_HALO
    last_halo = N_TOK // POOL_HALO - 1
    pcol = (ATTN_W + 2 * KV_W) // POOL_W
    return pl.pallas_call(
        _mixer_kernel,
        grid=(N_TOK // tm,),
        in_specs=[
            pl.BlockSpec((tm, POOL_W), lambda i: (i, pcol)),
            pl.BlockSpec((POOL_HALO, POOL_W), lambda i: (jnp.maximum(i * per_tile - 1, 0), pcol)),
            pl.BlockSpec((POOL_HALO, POOL_W), lambda i: (jnp.minimum((i + 1) * per_tile, last_halo), pcol)),
            pl.BlockSpec((tm, SGU_W), lambda i: (i, pcol + 1)),
            pl.BlockSpec((tm, SGU_W), lambda i: (i, pcol + 2)),
            pl.BlockSpec((len(POOL_WINDOWS), LANES, LANES), lambda i: (0, 0, 0)),
            pl.BlockSpec((1, POOL_W), lambda i: (0, 0)),
            pl.BlockSpec((N_SGU_HEADS, LANES), lambda i: (0, 0)),
            pl.BlockSpec((N_SGU_HEADS, CHUNK, CHUNK), lambda i: (0, 0, 0)),
            pl.BlockSpec((N_SGU_HEADS, CHUNK, LANES), lambda i: (0, 0, 0)),
        ],
        out_specs=pl.BlockSpec((tm, POOL_W + SGU_W), lambda i: (i, 0)),
        out_shape=jax.ShapeDtypeStruct((N_TOK, POOL_W + SGU_W), BF16),
        scratch_shapes=[pltpu.VMEM((tm + 2 * POOL_HALO, POOL_W), F32)],
        compiler_params=_cparams(("arbitrary",)),
        name="mixers",
    )(proj, proj, proj, proj, proj, pool_w_bf, pool_scale.reshape(1, POOL_W), sgu_norm_g, sgu_w_bf, sgu_b_full)


def _outproj_kernel(*refs, n_first):
    if n_first is None:
        a_ref, m_ref, w1_ref, w2_ref, x_ref, g_ref, o_ref = refs
        a, x = a_ref[...], x_ref[...]
    else:
        a_ref, ac_ref, m_ref, w1_ref, w2_ref, x_ref, xc_ref, g_ref, o_ref = refs
        is_first = pl.program_id(1) < n_first
        a = jnp.where(is_first, a_ref[...], ac_ref[...])
        x = jnp.where(is_first, x_ref[...], xc_ref[...])
    y = jnp.dot(a, w1_ref[...], preferred_element_type=F32)
    y = y + jnp.dot(m_ref[...], w2_ref[...], preferred_element_type=F32)
    o_ref[...] = x + g_ref[...] * y


def _out_proj(attn, mix, w_bf, src, mod3, n_rows):
    tm, tn = 512, 2048
    nj = D_MODEL // tn
    assert isinstance(attn, tuple) == isinstance(src, tuple)
    if isinstance(src, tuple):
        n_first = N_LAT // tm

        def first(i):
            return jnp.minimum(i, n_first - 1)

        def second(i):
            return jnp.maximum(i - n_first, 0)

        attn_specs = [pl.BlockSpec((tm, ATTN_W), lambda j, i: (first(i), 0)),
                      pl.BlockSpec((tm, ATTN_W), lambda j, i: (second(i), 0))]
        x_specs = [pl.BlockSpec((tm, tn), lambda j, i: (first(i), j)),
                   pl.BlockSpec((tm, tn), lambda j, i: (second(i), j))]
    else:
        n_first, attn, src = None, (attn,), (src,)
        attn_specs = [pl.BlockSpec((tm, ATTN_W), lambda j, i: (i, 0))]
        x_specs = [pl.BlockSpec((tm, tn), lambda j, i: (i, j))]
    return pl.pallas_call(
        functools.partial(_outproj_kernel, n_first=n_first),
        grid=(nj, n_rows // tm),
        in_specs=attn_specs + [
            pl.BlockSpec((tm, POOL_W + SGU_W), lambda j, i: (i, 0)),
            pl.BlockSpec((ATTN_W, tn), lambda j, i: (0, j)),
            pl.BlockSpec((POOL_W + SGU_W, tn), lambda j, i: (1, j)),
        ] + x_specs + [
            pl.BlockSpec((None, 1, tn), lambda j, i: (_mod_row(i, tm), 0, 2 * nj + j)),
        ],
        out_specs=pl.BlockSpec((tm, tn), lambda j, i: (i, j)),
        out_shape=jax.ShapeDtypeStruct((n_rows, D_MODEL), F32),
        compiler_params=_cparams(("arbitrary", "arbitrary")),
        name="out_proj",
    )(*attn, mix, w_bf, w_bf, *src, mod3)


def _router_kernel(x_ref, g_ref, sh_ref, sc_ref, rw_ref, rb_ref, tri_ref,
                   hp_ref, idx_ref, rank_ref, gate_ref, cnt_ref, run_s):
    @pl.when(pl.program_id(0) == 0)
    def _():
        run_s[...] = jnp.zeros_like(run_s)

    h = _rms(x_ref[...], g_ref[...]) * (1.0 + sc_ref[...]) + sh_ref[...]
    h_hi = h.astype(BF16)
    hb = h_hi.astype(F32)
    hi = lax.bitcast_convert_type(hb[:, :HALF_D], U32)
    lo = lax.bitcast_convert_type(hb[:, HALF_D:], U32)
    packed = hi | (lo >> 16)
    for j in range(PACK_TILES):
        hp_ref[pl.ds(j, packed.shape[0], stride=PACK_TILES), :] = packed[:, j * LANES:(j + 1) * LANES]

    h_lo = (h - hb).astype(BF16)
    both = jnp.dot(h_hi, rw_ref[...], preferred_element_type=F32)
    tail = jnp.dot(h_lo, rw_ref[:, 0:LANES], preferred_element_type=F32)
    logits = (both[:, 0:LANES] + both[:, LANES:]) + tail
    lt = (logits + rb_ref[...]).T[0:N_EXPERTS, :]
    expert = lax.broadcasted_iota(jnp.int32, lt.shape, 0).astype(F32)
    vals, idxs = [], []
    for _ in range(TOP_K):
        m = jnp.max(lt, axis=0, keepdims=True)
        idx = jnp.min(jnp.where(lt == m, expert, float(N_EXPERTS)), axis=0, keepdims=True)
        vals.append(m)
        idxs.append(idx)
        lt = jnp.where(expert == idx, -jnp.inf, lt)

    e = [jnp.exp(v - vals[0]) for v in vals]
    den = e[0] + e[1] + e[2] + e[3]
    slot = lax.broadcasted_iota(jnp.int32, (LANES, lt.shape[1]), 0)
    gates = jnp.zeros((LANES, lt.shape[1]), F32)
    for k in range(TOP_K):
        gates = jnp.where(slot == k, e[k] / den, gates)
    gate_ref[...] = gates.T

    base = run_s[...]
    for k in range(TOP_K):
        onehot = jnp.where(expert == idxs[k], 1.0, 0.0)
        before = jnp.dot(onehot.astype(BF16), tri_ref[...], preferred_element_type=F32)
        rank = jnp.sum(onehot * (before + base[:, 0:1]), axis=0, keepdims=True)
        idx_ref[k:k + 1, :] = idxs[k].astype(jnp.int32)
        rank_ref[k:k + 1, :] = rank.astype(jnp.int32)
        base = base + jnp.sum(onehot, axis=1, keepdims=True)
    run_s[...] = base
    cnt_ref[...] = base.astype(jnp.int32)


def _router(tok, norm_g, mod3, rw_pad, rb_pad, tri, n_rows):
    tm = ROUTER_TM
    return pl.pallas_call(
        _router_kernel,
        grid=(n_rows // tm,),
        in_specs=[
            pl.BlockSpec((tm, D_MODEL), lambda i: (i, 0)),
            pl.BlockSpec((1, D_MODEL), lambda i: (0, 0)),
            pl.BlockSpec((None, 1, D_MODEL), lambda i: (_mod_row(i, tm), 0, 3)),
            pl.BlockSpec((None, 1, D_MODEL), lambda i: (_mod_row(i, tm), 0, 4)),
            pl.BlockSpec((D_MODEL, 2 * LANES), lambda i: (0, 0)),
            pl.BlockSpec((1, LANES), lambda i: (0, 0)),
            pl.BlockSpec((tm, tm), lambda i: (0, 0)),
        ],
        out_specs=[
            pl.BlockSpec((tm * PACK_TILES, LANES), lambda i: (i, 0)),
            pl.BlockSpec((TOP_K, tm), lambda i: (0, i)),
            pl.BlockSpec((TOP_K, tm), lambda i: (0, i)),
            pl.BlockSpec((tm, LANES), lambda i: (i, 0)),
            pl.BlockSpec((N_EXPERTS, LANES), lambda i: (0, 0)),
        ],
        out_shape=[
            jax.ShapeDtypeStruct((n_rows * PACK_TILES, LANES), U32),
            jax.ShapeDtypeStruct((TOP_K, n_rows), jnp.int32),
            jax.ShapeDtypeStruct((TOP_K, n_rows), jnp.int32),
            jax.ShapeDtypeStruct((n_rows, LANES), F32),
            jax.ShapeDtypeStruct((N_EXPERTS, LANES), jnp.int32),
        ],
        scratch_shapes=[pltpu.VMEM((N_EXPERTS, LANES), F32)],
        compiler_params=_cparams(("arbitrary",)),
        name="router",
    )(tok, norm_g.reshape(1, D_MODEL), mod3, mod3, rw_pad, rb_pad, tri)


def _dispatch_kernel(dest_ref, pad_ref, hp_ref, xs_hbm, sem, pad_sem, *, rows, n_rows):
    i = pl.program_id(0)

    @pl.when(i == 0)
    def _():
        def per_expert(e, carry):
            lo, hi = pad_ref[e], pad_ref[N_EXPERTS + 1 + e]

            def start(p, c):
                pltpu.make_async_copy(hp_ref.at[0], xs_hbm.at[p], pad_sem).start()
                return c

            def wait(p, c):
                pltpu.make_async_copy(hp_ref.at[0], xs_hbm.at[p], pad_sem).wait()
                return c

            lax.fori_loop(lo, hi, start, 0)
            lax.fori_loop(lo, hi, wait, 0)
            return carry

        lax.fori_loop(0, N_EXPERTS + 1, per_expert, 0)

    base = i * rows

    def issue(r2, carry):
        for u in range(2):
            r = 2 * r2 + u
            for k in range(TOP_K):
                d = dest_ref[k * n_rows + base + r]
                pltpu.make_async_copy(hp_ref.at[r], xs_hbm.at[d], sem).start(priority=k % 2)
        return carry

    lax.fori_loop(0, rows // 2, issue, 0)
    for k in range(TOP_K):
        pltpu.make_async_copy(hp_ref, xs_hbm.at[pl.ds(0, rows)], sem).wait()


def _dispatch(dest_flat, pad_meta, hp, n_pad, n_rows):
    rows = 1024
    return pl.pallas_call(
        functools.partial(_dispatch_kernel, rows=rows, n_rows=n_rows),
        grid_spec=pltpu.PrefetchScalarGridSpec(
            num_scalar_prefetch=2,
            grid=(n_rows // rows,),
            in_specs=[pl.BlockSpec((rows, PACK_TILES, LANES), lambda i, d, p: (i, 0, 0))],
            out_specs=pl.BlockSpec(memory_space=pl.ANY),
            scratch_shapes=[pltpu.SemaphoreType.DMA(()), pltpu.SemaphoreType.DMA(())],
        ),
        out_shape=jax.ShapeDtypeStruct((n_pad, PACK_TILES, LANES), U32),
        compiler_params=_cparams(("arbitrary",)),
        name="dispatch",
    )(dest_flat, pad_meta, hp)


def _is_new_expert(meta_ref, i):
    prev = meta_ref[jnp.maximum(i - 1, 0)]
    return jnp.logical_or(i == 0, meta_ref[i] != prev)


def _cast_rows(src_ref, dst_ref, chunk=256):
    def body(c, carry):
        rows = pl.ds(pl.multiple_of(c * chunk, chunk), chunk)
        dst_ref[rows, :] = src_ref[rows, :].astype(dst_ref.dtype)
        return carry

    lax.fori_loop(0, src_ref.shape[0] // chunk, body, 0)


def _for_valid_rows(valid, fn):
    lo = 0
    for n in EXPERT_ROW_STEPS:
        @pl.when(jnp.logical_and(valid > lo, valid <= n))
        def _(n=n):
            fn(n)

        lo = n


def _unpack_rows(x_ref, n):
    xp = jnp.concatenate([x_ref[pl.ds(j, n, stride=PACK_TILES), :] for j in range(PACK_TILES)], axis=1)
    hi = lax.bitcast_convert_type(xp & jnp.uint32(0xFFFF0000), F32).astype(BF16)
    lo = lax.bitcast_convert_type(xp << 16, F32).astype(BF16)
    return jnp.concatenate([hi, lo], axis=1)


def _expert_gu_kernel(meta_ref, x_ref, w_hbm, bg_ref, bu_ref, o_ref, wg_f, wu_f, wg_s, wu_s, sems,
                      *, n_blocks, layer):
    j, i = pl.program_id(0), pl.program_id(1)
    tn = wg_f.shape[1]
    nj = pl.num_programs(0)

    def fetch(e, jj):
        col = pl.multiple_of(jj * tn, tn)
        return (pltpu.make_async_copy(w_hbm.at[layer, e, :, pl.ds(col, tn)], wg_f, sems.at[0]),
                pltpu.make_async_copy(w_hbm.at[layer, e, :, pl.ds(EXPERT_FF + col, tn)], wu_f, sems.at[1]))

    @pl.when(jnp.logical_and(j == 0, i == 0))
    def _():
        for cp in fetch(meta_ref[0], 0):
            cp.start()

    active = i < meta_ref[n_blocks]

    @pl.when(jnp.logical_and(active, _is_new_expert(meta_ref, i)))
    def _():
        for cp in fetch(meta_ref[i], j):
            cp.wait()
        _cast_rows(wg_f, wg_s)
        _cast_rows(wu_f, wu_s)
        nxt = meta_ref[n_blocks + 1 + i]

        @pl.when(nxt >= 0)
        def _():
            for cp in fetch(nxt, j):
                cp.start()

        @pl.when(jnp.logical_and(nxt < 0, j + 1 < nj))
        def _():
            for cp in fetch(meta_ref[0], j + 1):
                cp.start()

    def compute(n):
        x = _unpack_rows(x_ref, n)
        g = jnp.dot(x, wg_s[...], preferred_element_type=F32) + bg_ref[...]
        u = jnp.dot(x, wu_s[...], preferred_element_type=F32) + bu_ref[...]
        g = jnp.minimum(g, SWIGLU_LIMIT)
        u = jnp.clip(u, -SWIGLU_LIMIT, SWIGLU_LIMIT)
        o_ref[0:n, :] = ((u + 1.0) * (g * _sigmoid(g * SWIGLU_ALPHA))).astype(o_ref.dtype)
        if n < o_ref.shape[0]:
            o_ref[n:, :] = jnp.zeros((o_ref.shape[0] - n, o_ref.shape[1]), o_ref.dtype)

    _for_valid_rows(meta_ref[2 * n_blocks + 1 + i], compute)

    @pl.when(jnp.logical_not(active))
    def _():
        o_ref[...] = jnp.zeros_like(o_ref)


def _active_block(i, m, n_blocks):
    return jnp.minimum(i, m[n_blocks] - 1)


def _expert_gu(meta, xs, w_gu, b_gu, layer, n_blocks):
    tm, tn = EXPERT_TM, 1024
    nj = EXPERT_FF // tn
    b3 = b_gu.reshape(DEPTH, N_EXPERTS, 1, 2 * EXPERT_FF)
    return pl.pallas_call(
        functools.partial(_expert_gu_kernel, n_blocks=n_blocks, layer=layer),
        grid_spec=pltpu.PrefetchScalarGridSpec(
            num_scalar_prefetch=1,
            grid=(nj, n_blocks),
            in_specs=[
                pl.BlockSpec((tm * PACK_TILES, LANES), lambda j, i, m: (_active_block(i, m, n_blocks), 0)),
                pl.BlockSpec(memory_space=pl.ANY),
                pl.BlockSpec((None, None, 1, tn), lambda j, i, m: (layer, m[i], 0, j)),
                pl.BlockSpec((None, None, 1, tn), lambda j, i, m: (layer, m[i], 0, nj + j)),
            ],
            out_specs=pl.BlockSpec((tm, tn), lambda j, i, m: (i, j)),
            scratch_shapes=[pltpu.VMEM((D_MODEL, tn), F32), pltpu.VMEM((D_MODEL, tn), F32),
                            pltpu.VMEM((D_MODEL, tn), BF16), pltpu.VMEM((D_MODEL, tn), BF16),
                            pltpu.SemaphoreType.DMA((2,))],
        ),
        out_shape=jax.ShapeDtypeStruct((n_blocks * tm, EXPERT_FF), BF16),
        compiler_params=_cparams(("arbitrary", "arbitrary")),
        name="expert_gu",
    )(meta, xs, w_gu, b3, b3)


def _expert_down_kernel(meta_ref, a_ref, w_hbm, b_ref, o_ref, w_f, w_s, sem, *, n_blocks, layer):
    i = pl.program_id(0)

    def fetch(e):
        return pltpu.make_async_copy(w_hbm.at[layer, e], w_f, sem)

    @pl.when(i == 0)
    def _():
        fetch(meta_ref[0]).start()

    active = i < meta_ref[n_blocks]

    @pl.when(jnp.logical_and(active, _is_new_expert(meta_ref, i)))
    def _():
        fetch(meta_ref[i]).wait()
        _cast_rows(w_f, w_s)
        nxt = meta_ref[n_blocks + 1 + i]

        @pl.when(nxt >= 0)
        def _():
            fetch(nxt).start()

    def compute(n):
        o_ref[0:n, :] = jnp.dot(a_ref[0:n, :], w_s[...], preferred_element_type=F32) + b_ref[...]
        if n < o_ref.shape[0]:
            o_ref[n:, :] = jnp.zeros((o_ref.shape[0] - n, o_ref.shape[1]), o_ref.dtype)

    _for_valid_rows(meta_ref[2 * n_blocks + 1 + i], compute)

    @pl.when(jnp.logical_not(active))
    def _():
        o_ref[...] = jnp.zeros_like(o_ref)


def _expert_down(meta, act, w_down, b_down, layer, n_blocks):
    tm = EXPERT_TM
    b3 = b_down.reshape(DEPTH, N_EXPERTS, 1, D_MODEL)
    return pl.pallas_call(
        functools.partial(_expert_down_kernel, n_blocks=n_blocks, layer=layer),
        grid_spec=pltpu.PrefetchScalarGridSpec(
            num_scalar_prefetch=1,
            grid=(n_blocks,),
            in_specs=[
                pl.BlockSpec((tm, EXPERT_FF), lambda i, m: (_active_block(i, m, n_blocks), 0)),
                pl.BlockSpec(memory_space=pl.ANY),
                pl.BlockSpec((None, None, 1, D_MODEL), lambda i, m: (layer, m[i], 0, 0)),
            ],
            out_specs=pl.BlockSpec((tm, D_MODEL), lambda i, m: (i, 0)),
            scratch_shapes=[pltpu.VMEM((EXPERT_FF, D_MODEL), F32), pltpu.VMEM((EXPERT_FF, D_MODEL), BF16),
                            pltpu.SemaphoreType.DMA(())],
        ),
        out_shape=jax.ShapeDtypeStruct((n_blocks * tm, D_MODEL), F32),
        compiler_params=_cparams(("arbitrary",)),
        name="expert_down",
    )(meta, act, w_down, b3)


def _combine_kernel(dest_ref, y_hbm, gt_ref, x_ref, g_ref, o_ref, buf, sems, *, rows, n_rows):
    i = pl.program_id(0)
    n_steps = pl.num_programs(0)

    def issue(step, slot):
        base = step * rows

        def body(r8, carry):
            r0 = pl.multiple_of(r8 * 8, 8)
            for u in range(8):
                for k in range(TOP_K):
                    d = dest_ref[k * n_rows + base + r0 + u]
                    pltpu.make_async_copy(y_hbm.at[pl.ds(d, 1), :],
                                          buf.at[slot, k, pl.ds(r0, 8), :].at[pl.ds(u, 1), :],
                                          sems.at[slot]).start(priority=k % 2)
            return carry

        lax.fori_loop(0, rows // 8, body, 0)

    @pl.when(i == 0)
    def _():
        issue(0, 0)

    @pl.when(i + 1 < n_steps)
    def _():
        issue(i + 1, (i + 1) % 2)

    slot = i % 2
    for k in range(TOP_K):
        pltpu.make_async_copy(y_hbm.at[pl.ds(0, rows), :], buf.at[slot, k], sems.at[slot]).wait()
    gt = gt_ref[...]
    moe = gt[:, 0:1] * buf[slot, 0]
    for k in range(1, TOP_K):
        moe = moe + gt[:, k:k + 1] * buf[slot, k]
    o_ref[...] = x_ref[...] + g_ref[...] * moe


def _combine(dest_flat, ys, gates_t, tok, mod3, n_rows):
    rows = 128
    return pl.pallas_call(
        functools.partial(_combine_kernel, rows=rows, n_rows=n_rows),
        grid_spec=pltpu.PrefetchScalarGridSpec(
            num_scalar_prefetch=1,
            grid=(n_rows // rows,),
            in_specs=[
                pl.BlockSpec(memory_space=pl.ANY),
                pl.BlockSpec((rows, LANES), lambda i, d: (i, 0)),
                pl.BlockSpec((rows, D_MODEL), lambda i, d: (i, 0)),
                pl.BlockSpec((None, 1, D_MODEL), lambda i, d: (_mod_row(i, rows), 0, 5)),
            ],
            out_specs=pl.BlockSpec((rows, D_MODEL), lambda i, d: (i, 0)),
            scratch_shapes=[pltpu.VMEM((2, TOP_K, rows, D_MODEL), F32), pltpu.SemaphoreType.DMA((2,))],
        ),
        out_shape=jax.ShapeDtypeStruct((n_rows, D_MODEL), F32),
        compiler_params=_cparams(("arbitrary",)),
        name="combine",
    )(dest_flat, ys, gates_t, tok, mod3)


def _routing(top_idx, rank, counts, n_rows):
    tm = EXPERT_TM
    n_blocks = -(-(TOP_K * n_rows) // tm) + N_EXPERTS
    counts = counts[:, 0]
    padded = (counts + tm - 1) // tm * tm
    padded_end = jnp.cumsum(padded)
    padded_start = padded_end - padded
    experts = jnp.arange(N_EXPERTS, dtype=jnp.int32)
    start_of = jnp.sum(jnp.where(top_idx[:, :, None] == experts, padded_start, 0), axis=-1)
    dest = (start_of + rank).astype(jnp.int32).reshape(TOP_K * n_rows)
    block_start = jnp.arange(n_blocks, dtype=jnp.int32) * tm
    block_expert = jnp.minimum(jnp.sum(padded_end[None, :] <= block_start[:, None], axis=1), N_EXPERTS - 1)
    n_active = padded_end[-1:] // tm
    group_end = jnp.sum(jnp.where(block_expert[:, None] == experts, padded_end // tm, 0), axis=1)
    follower = jnp.sum(jnp.where(group_end[:, None] == jnp.arange(n_blocks)[None, :], block_expert, 0), axis=1)
    next_expert = jnp.where(group_end < n_active, follower, -1)
    is_expert = block_expert[:, None] == experts
    block_count = jnp.sum(jnp.where(is_expert, counts, 0), axis=1)
    block_first = jnp.sum(jnp.where(is_expert, padded_start, 0), axis=1)
    block_valid = jnp.clip(block_count - (block_start - block_first), 0, tm)
    meta = jnp.concatenate([block_expert, n_active, next_expert, block_valid]).astype(jnp.int32)
    n_pad = jnp.full((1,), n_blocks * tm, jnp.int32)
    pad_meta = jnp.concatenate([padded_start + counts, padded_end[-1:], padded_end, n_pad]).astype(jnp.int32)
    return dest, meta, pad_meta, n_blocks


def _rope_tables():
    rows = SEQ // GRID_W
    row = jnp.broadcast_to(jnp.arange(rows, dtype=F32)[:, None], (rows, GRID_W)).reshape(-1)
    col = jnp.broadcast_to(jnp.arange(GRID_W, dtype=F32)[None, :], (rows, GRID_W)).reshape(-1)
    inv_freq = ROPE_THETA ** (-jnp.arange(ROPE_FREQS, dtype=F32) / ROPE_FREQS)
    ang_r = row[:, None] * inv_freq
    ang_c = col[:, None] * inv_freq
    cos = jnp.concatenate([jnp.cos(ang_r), jnp.cos(ang_r), jnp.cos(ang_c), jnp.cos(ang_c)], axis=-1)
    sin = jnp.concatenate([-jnp.sin(ang_r), jnp.sin(ang_r), -jnp.sin(ang_c), jnp.sin(ang_c)], axis=-1)
    return cos, sin


def kernel(x, c, ctx, c_ctx, ada_w, ada_b, norm1_g, norm2_g, w_in, q_norm_g, k_norm_g, pool_w, pool_scale,
           sgu_norm_g, sgu_w, sgu_b, w_out, router_w, router_b, w_gu, b_gu, w_down, b_down):
    cos, sin = _rope_tables()
    tok = (x.reshape(N_LAT, D_MODEL), ctx.reshape(N_CTX, D_MODEL))
    cc = jnp.zeros((MOD_ROWS, D_MODEL), F32).at[:BATCH].set(c).at[BATCH].set(c_ctx)
    mod = _adaln(cc, ada_w, ada_b)
    tri = jnp.triu(jnp.ones((ROUTER_TM, ROUTER_TM), BF16), k=1)
    for l in range(DEPTH):
        last = l == DEPTH - 1
        n_rows = N_LAT if last else N_TOK
        mod3 = mod[l].reshape(MOD_ROWS, 1, 6 * D_MODEL)
        proj = _in_proj(tok, norm1_g[l], mod3, w_in[l].astype(BF16))
        attn = _latent_attention(proj, cos, sin, q_norm_g[l], k_norm_g[l])
        if not last:
            attn = (attn, _context_attention(proj, q_norm_g[l], k_norm_g[l]))
        sgu_b_full = jnp.broadcast_to(sgu_b[l][:, :, None], (N_SGU_HEADS, CHUNK, LANES))
        mix = _mixers(proj, pool_w[l].astype(BF16), pool_scale[l], sgu_norm_g[l], sgu_w[l].astype(BF16), sgu_b_full)
        tok = _out_proj(attn, mix, w_out[l].astype(BF16), tok, mod3, n_rows)
        rw_hi = router_w[l].astype(BF16)
        rw_lo = (router_w[l] - rw_hi.astype(F32)).astype(BF16)
        rw_pad = (jnp.zeros((D_MODEL, 2 * LANES), BF16).at[:, :N_EXPERTS].set(rw_hi)
                  .at[:, LANES:LANES + N_EXPERTS].set(rw_lo))
        rb_pad = jnp.zeros((1, LANES), F32).at[0, :N_EXPERTS].set(router_b[l])
        hp, top_idx, rank, gates_t, counts = _router(tok, norm2_g[l], mod3, rw_pad, rb_pad, tri, n_rows)
        dest, meta, pad_meta, n_blocks = _routing(top_idx, rank, counts, n_rows)
        n_pad = n_blocks * EXPERT_TM
        xs = _dispatch(dest, pad_meta, hp.reshape(n_rows, PACK_TILES, LANES), n_pad, n_rows)
        act = _expert_gu(meta, xs.reshape(n_pad * PACK_TILES, LANES), w_gu, b_gu, l, n_blocks)
        ys = _expert_down(meta, act, w_down, b_down, l, n_blocks)
        tok = _combine(dest, ys, gates_t, tok, mod3, n_rows)
    return tok.reshape(BATCH, SEQ, D_MODEL)
```

```python
import functools

import jax
import jax.numpy as jnp
from jax import lax
from jax.experimental import pallas as pl
from jax.experimental.pallas import tpu as pltpu

F32 = jnp.float32
BF16 = jnp.bfloat16
U32 = jnp.uint32

D_MODEL = 2048
BATCH = 4
SEQ = 4096
DEPTH = 2
GRID_W = 64
CTX_LEN = 256
HEAD_DIM = 128
ATTN_W = 1024
GQA_GROUP = 4
N_KV_HEADS = 2
KV_W = 256
ROPE_THETA = 10000.0
ROPE_FREQS = 32
ATTN_SCALE = HEAD_DIM ** -0.5
LOG2_E = 1.4426950408889634
POOL_WINDOWS = (2, 4, 8, 16)
POOL_W = 512
SGU_W = 512
N_SGU_HEADS = 4
CHUNK = 128
IN_W = 3072
N_EXPERTS = 32
TOP_K = 4
EXPERT_FF = 2048
SWIGLU_LIMIT = 7.0
SWIGLU_ALPHA = 1.702
EPS = 1e-6

N_LAT = BATCH * SEQ
N_CTX = BATCH * CTX_LEN
N_TOK = N_LAT + N_CTX
MOD_ROWS = 8
LANES = 128
POOL_HALO = 8
MIX_TM = 256
EXPERT_TM = 256
EXPERT_ROW_STEPS = (64, 128, 192, 256)
ROUTER_TM = 512
HALF_D = D_MODEL // 2
PACK_TILES = HALF_D // LANES
VMEM_LIMIT = 56 * 1024 * 1024


def _cparams(sem, vmem=VMEM_LIMIT):
    return pltpu.CompilerParams(dimension_semantics=sem, vmem_limit_bytes=vmem)


def _mod_row(row_tile, tm):
    return jnp.minimum(row_tile * tm // SEQ, BATCH)


def _rms(x, g):
    return x * lax.rsqrt(jnp.mean(x * x, axis=-1, keepdims=True) + EPS) * g


def _sigmoid(x):
    return 1.0 / (1.0 + jnp.exp(-x))


def _gelu(x):
    return 0.5 * x * (1.0 + lax.erf(x * 0.7071067811865476))


def _adaln_kernel(cc_ref, w_ref, b_ref, o_ref):
    cc = cc_ref[...]
    s = (cc * _sigmoid(cc)).astype(BF16)
    o_ref[...] = jnp.dot(s, w_ref[...].astype(BF16), preferred_element_type=F32) + b_ref[...]


def _adaln(cc, ada_w, ada_b):
    tn = 1024
    n = 6 * D_MODEL
    return pl.pallas_call(
        _adaln_kernel,
        grid=(DEPTH, n // tn),
        in_specs=[
            pl.BlockSpec((MOD_ROWS, D_MODEL), lambda l, j: (0, 0)),
            pl.BlockSpec((None, D_MODEL, tn), lambda l, j: (l, 0, j)),
            pl.BlockSpec((None, 1, tn), lambda l, j: (l, 0, j)),
        ],
        out_specs=pl.BlockSpec((None, MOD_ROWS, tn), lambda l, j: (l, 0, j)),
        out_shape=jax.ShapeDtypeStruct((DEPTH, MOD_ROWS, n), F32),
        compiler_params=_cparams(("arbitrary", "arbitrary")),
        name="adaln",
    )(cc, ada_w, ada_b.reshape(DEPTH, 1, n))


def _proj_kernel(*refs, n_first):
    if n_first is None:
        (xa_ref, g_ref, sh_ref, sc_ref, w_ref, o_ref, h_s), xb_ref = refs, None
    else:
        xa_ref, xb_ref, g_ref, sh_ref, sc_ref, w_ref, o_ref, h_s = refs
    i, j = pl.program_id(0), pl.program_id(1)

    def prepare(tile, slot):
        x = xa_ref[...]
        if xb_ref is not None:
            x = jnp.where(tile < n_first, x, xb_ref[...])
        y = _rms(x, g_ref[...])
        h_s[slot] = (y * (1.0 + sc_ref[...]) + sh_ref[...]).astype(BF16)

    @pl.when(jnp.logical_and(i == 0, j == 0))
    def _():
        prepare(0, 0)

    last = j == pl.num_programs(1) - 1

    @pl.when(last)
    def _():
        o_ref[...] = jnp.dot(h_s[i % 2], w_ref[...], preferred_element_type=F32)
        prepare(i + 1, (i + 1) % 2)

    @pl.when(jnp.logical_not(last))
    def _():
        o_ref[...] = jnp.dot(h_s[i % 2], w_ref[...], preferred_element_type=F32)


def _in_proj(src, norm_g, mod3, w_bf):
    tm, tn = 512, 1536
    n_tiles, nj = N_TOK // tm, IN_W // tn

    def tile(i, j):
        return jnp.minimum(i + (j + 1) // nj, n_tiles - 1)

    if isinstance(src, tuple):
        n_first = N_LAT // tm
        x_specs = [pl.BlockSpec((tm, D_MODEL), lambda i, j: (jnp.minimum(tile(i, j), n_first - 1), 0)),
                   pl.BlockSpec((tm, D_MODEL), lambda i, j: (jnp.maximum(tile(i, j) - n_first, 0), 0))]
    else:
        n_first, src = None, (src,)
        x_specs = [pl.BlockSpec((tm, D_MODEL), lambda i, j: (tile(i, j), 0))]
    return pl.pallas_call(
        functools.partial(_proj_kernel, n_first=n_first),
        grid=(n_tiles, nj),
        in_specs=x_specs + [
            pl.BlockSpec((1, D_MODEL), lambda i, j: (0, 0)),
            pl.BlockSpec((None, 1, D_MODEL), lambda i, j: (_mod_row(tile(i, j), tm), 0, 0)),
            pl.BlockSpec((None, 1, D_MODEL), lambda i, j: (_mod_row(tile(i, j), tm), 0, 1)),
            pl.BlockSpec((D_MODEL, tn), lambda i, j: (0, j)),
        ],
        out_specs=pl.BlockSpec((tm, tn), lambda i, j: (i, j)),
        out_shape=jax.ShapeDtypeStruct((N_TOK, IN_W), F32),
        scratch_shapes=[pltpu.VMEM((2, tm, D_MODEL), BF16)],
        compiler_params=_cparams(("arbitrary", "arbitrary")),
        name="in_proj",
    )(*src, norm_g.reshape(1, D_MODEL), mod3, mod3, w_bf)


def _rope(x, cos, sin):
    lane = lax.broadcasted_iota(jnp.int32, x.shape, 1)
    first = (lane % 64) < 32
    partner = jnp.where(first, pltpu.roll(x, 96, 1), pltpu.roll(x, 32, 1))
    return x * cos + partner * sin


def _softmax_pv(q, k, v):
    s = lax.dot_general(q, k, (((1,), (1,)), ((), ())), preferred_element_type=F32) * ATTN_SCALE
    m = jnp.max(s, axis=-1, keepdims=True)
    p = jnp.exp(s - m)
    l = jnp.sum(p, axis=-1, keepdims=True)
    return jnp.dot(p.astype(BF16), v, preferred_element_type=F32) / l


def _lat_attn_kernel(q_ref, kl_ref, vl_ref, kc_ref, vc_ref, cosq_ref, sinq_ref, cosk_ref, sink_ref,
                     qg_ref, kg_ref, o_ref, k_s, v_s, s_s):
    @pl.when(pl.program_id(2) == 0)
    def _():
        k_s[0:CTX_LEN, :] = _rms(kc_ref[...], kg_ref[...]).astype(BF16)
        kl = _rope(_rms(kl_ref[...], kg_ref[...]), cosk_ref[...], sink_ref[...])
        k_s[CTX_LEN:, :] = kl.astype(BF16)
        v_s[0:CTX_LEN, 0:HEAD_DIM] = vc_ref[...].astype(BF16)
        v_s[CTX_LEN:, 0:HEAD_DIM] = vl_ref[...].astype(BF16)
        v_s[:, HEAD_DIM:] = jnp.ones((CTX_LEN + SEQ, LANES), BF16)

    def scores(g):
        lanes = slice(g * HEAD_DIM, (g + 1) * HEAD_DIM)
        q = _rope(_rms(q_ref[:, lanes], qg_ref[...]), cosq_ref[...], sinq_ref[...])
        q = (q * (ATTN_SCALE * LOG2_E)).astype(BF16)
        s_s[g % 2] = lax.dot_general(q, k_s[...], (((1,), (1,)), ((), ())), preferred_element_type=F32)

    scores(0)
    for g in range(GQA_GROUP):
        if g + 1 < GQA_GROUP:
            scores(g + 1)
        s = s_s[g % 2]
        p = jnp.exp2(s - jnp.max(s, axis=-1, keepdims=True))
        ol = jnp.dot(p.astype(BF16), v_s[...], preferred_element_type=F32)
        o = ol[:, 0:HEAD_DIM] / ol[:, HEAD_DIM:]
        o_ref[:, g * HEAD_DIM:(g + 1) * HEAD_DIM] = o.astype(o_ref.dtype)


def _latent_attention(proj, cos, sin, q_g, k_g):
    tq = 256
    nq = SEQ // tq
    qw = GQA_GROUP * HEAD_DIM
    kcol = ATTN_W // HEAD_DIM
    vcol = (ATTN_W + KV_W) // HEAD_DIM
    ctx_blk = N_LAT // CTX_LEN
    return pl.pallas_call(
        _lat_attn_kernel,
        grid=(BATCH, N_KV_HEADS, nq),
        in_specs=[
            pl.BlockSpec((tq, qw), lambda b, h, i: (b * nq + i, h)),
            pl.BlockSpec((SEQ, HEAD_DIM), lambda b, h, i: (b, kcol + h)),
            pl.BlockSpec((SEQ, HEAD_DIM), lambda b, h, i: (b, vcol + h)),
            pl.BlockSpec((CTX_LEN, HEAD_DIM), lambda b, h, i: (ctx_blk + b, kcol + h)),
            pl.BlockSpec((CTX_LEN, HEAD_DIM), lambda b, h, i: (ctx_blk + b, vcol + h)),
            pl.BlockSpec((tq, HEAD_DIM), lambda b, h, i: (i, 0)),
            pl.BlockSpec((tq, HEAD_DIM), lambda b, h, i: (i, 0)),
            pl.BlockSpec((SEQ, HEAD_DIM), lambda b, h, i: (0, 0)),
            pl.BlockSpec((SEQ, HEAD_DIM), lambda b, h, i: (0, 0)),
            pl.BlockSpec((1, HEAD_DIM), lambda b, h, i: (0, 0)),
            pl.BlockSpec((1, HEAD_DIM), lambda b, h, i: (0, 0)),
        ],
        out_specs=pl.BlockSpec((tq, qw), lambda b, h, i: (b * nq + i, h)),
        out_shape=jax.ShapeDtypeStruct((N_LAT, ATTN_W), BF16),
        scratch_shapes=[pltpu.VMEM((CTX_LEN + SEQ, HEAD_DIM), BF16),
                        pltpu.VMEM((CTX_LEN + SEQ, HEAD_DIM + LANES), BF16),
                        pltpu.VMEM((2, tq, CTX_LEN + SEQ), F32)],
        compiler_params=_cparams(("arbitrary", "arbitrary", "arbitrary")),
        name="latent_attention",
    )(proj, proj, proj, proj, proj, cos, sin, cos, sin, q_g.reshape(1, HEAD_DIM), k_g.reshape(1, HEAD_DIM))


def _ctx_attn_kernel(q_ref, k_ref, v_ref, qg_ref, kg_ref, o_ref):
    k = _rms(k_ref[...], kg_ref[...]).astype(BF16)
    v = v_ref[...].astype(BF16)
    for g in range(GQA_GROUP):
        lanes = slice(g * HEAD_DIM, (g + 1) * HEAD_DIM)
        q = _rms(q_ref[:, lanes], qg_ref[...]).astype(BF16)
        o_ref[:, lanes] = _softmax_pv(q, k, v).astype(o_ref.dtype)


def _context_attention(proj, q_g, k_g):
    qw = GQA_GROUP * HEAD_DIM
    kcol = ATTN_W // HEAD_DIM
    vcol = (ATTN_W + KV_W) // HEAD_DIM
    ctx_blk = N_LAT // CTX_LEN
    return pl.pallas_call(
        _ctx_attn_kernel,
        grid=(BATCH, N_KV_HEADS),
        in_specs=[
            pl.BlockSpec((CTX_LEN, qw), lambda b, h: (ctx_blk + b, h)),
            pl.BlockSpec((CTX_LEN, HEAD_DIM), lambda b, h: (ctx_blk + b, kcol + h)),
            pl.BlockSpec((CTX_LEN, HEAD_DIM), lambda b, h: (ctx_blk + b, vcol + h)),
            pl.BlockSpec((1, HEAD_DIM), lambda b, h: (0, 0)),
            pl.BlockSpec((1, HEAD_DIM), lambda b, h: (0, 0)),
        ],
        out_specs=pl.BlockSpec((CTX_LEN, qw), lambda b, h: (b, h)),
        out_shape=jax.ShapeDtypeStruct((N_CTX, ATTN_W), BF16),
        compiler_params=_cparams(("arbitrary", "arbitrary")),
        name="context_attention",
    )(proj, proj, proj, q_g.reshape(1, HEAD_DIM), k_g.reshape(1, HEAD_DIM))


def _mixer_kernel(pin_ref, prev_ref, next_ref, su_ref, sv_ref, pw_ref, ps_ref, sg_ref, sw_ref, sb_ref,
                  o_ref, pad_ref):
    tm = MIX_TM
    i = pl.program_id(0)
    is_lat = i < N_LAT // tm
    pos0 = jnp.where(is_lat, (i % (SEQ // tm)) * tm, 0)
    seq_len = jnp.where(is_lat, SEQ, CTX_LEN)
    pad_ref[0:POOL_HALO, :] = jnp.where(pos0 == 0, 0.0, prev_ref[...])
    pad_ref[POOL_HALO:POOL_HALO + tm, :] = pin_ref[...]
    pad_ref[POOL_HALO + tm:, :] = jnp.where(pos0 + tm == seq_len, 0.0, next_ref[...])
    t = pos0 + lax.broadcasted_iota(jnp.int32, (tm, LANES), 0)
    for gi, w in enumerate(POOL_WINDOWS):
        lanes = slice(gi * LANES, (gi + 1) * LANES)
        acc = pad_ref[POOL_HALO - w // 2:POOL_HALO - w // 2 + tm, lanes]
        for d in range(-w // 2 + 1, w // 2):
            acc = acc + pad_ref[POOL_HALO + d:POOL_HALO + d + tm, lanes]
        cnt = (jnp.minimum(t + w // 2, seq_len) - jnp.maximum(t - w // 2, 0)).astype(F32)
        mixed = acc / cnt - pin_ref[:, lanes]
        y = jnp.dot(mixed.astype(BF16), pw_ref[gi], preferred_element_type=F32) * ps_ref[:, lanes]
        o_ref[:, lanes] = y.astype(o_ref.dtype)

    for h in range(N_SGU_HEADS):
        lanes = slice(h * LANES, (h + 1) * LANES)
        gu = _gelu(su_ref[:, lanes])
        vh = _rms(_gelu(sv_ref[:, lanes]), sg_ref[h:h + 1, :]).astype(BF16)
        for n in range(tm // CHUNK):
            rows = slice(n * CHUNK, (n + 1) * CHUNK)
            mixed = jnp.dot(sw_ref[h], vh[rows], preferred_element_type=F32) + sb_ref[h]
            o_ref[rows, POOL_W + h * LANES:POOL_W + (h + 1) * LANES] = (gu[rows] * mixed).astype(o_ref.dtype)


def _mixers(proj, pool_w_bf, pool_scale, sgu_norm_g, sgu_w_bf, sgu_b_full):
    tm = MIX_TM
    per_tile = tm // POOL_HALO
    last_halo = N_TOK // POOL_HALO - 1
    pcol = (ATTN_W + 2 * KV_W) // POOL_W
    return pl.pallas_call(
        _mixer_kernel,
        grid=(N_TOK // tm,),
        in_specs=[
            pl.BlockSpec((tm, POOL_W), lambda i: (i, pcol)),
            pl.BlockSpec((POOL_HALO, POOL_W), lambda i: (jnp.maximum(i * per_tile - 1, 0), pcol)),
            pl.BlockSpec((POOL_HALO, POOL_W), lambda i: (jnp.minimum((i + 1) * per_tile, last_halo), pcol)),
            pl.BlockSpec((tm, SGU_W), lambda i: (i, pcol + 1)),
            pl.BlockSpec((tm, SGU_W), lambda i: (i, pcol + 2)),
            pl.BlockSpec((len(POOL_WINDOWS), LANES, LANES), lambda i: (0, 0, 0)),
            pl.BlockSpec((1, POOL_W), lambda i: (0, 0)),
            pl.BlockSpec((N_SGU_HEADS, LANES), lambda i: (0, 0)),
            pl.BlockSpec((N_SGU_HEADS, CHUNK, CHUNK), lambda i: (0, 0, 0)),
            pl.BlockSpec((N_SGU_HEADS, CHUNK, LANES), lambda i: (0, 0, 0)),
        ],
        out_specs=pl.BlockSpec((tm, POOL_W + SGU_W), lambda i: (i, 0)),
        out_shape=jax.ShapeDtypeStruct((N_TOK, POOL_W + SGU_W), BF16),
        scratch_shapes=[pltpu.VMEM((tm + 2 * POOL_HALO, POOL_W), F32)],
        compiler_params=_cparams(("arbitrary",)),
        name="mixers",
    )(proj, proj, proj, proj, proj, pool_w_bf, pool_scale.reshape(1, POOL_W), sgu_norm_g, sgu_w_bf, sgu_b_full)


def _outproj_kernel(*refs, n_first):
    if n_first is None:
        a_ref, m_ref, w1_ref, w2_ref, x_ref, g_ref, o_ref = refs
        a, x = a_ref[...], x_ref[...]
    else:
        a_ref, ac_ref, m_ref, w1_ref, w2_ref, x_ref, xc_ref, g_ref, o_ref = refs
        is_first = pl.program_id(1) < n_first
        a = jnp.where(is_first, a_ref[...], ac_ref[...])
        x = jnp.where(is_first, x_ref[...], xc_ref[...])
    y = jnp.dot(a, w1_ref[...], preferred_element_type=F32)
    y = y + jnp.dot(m_ref[...], w2_ref[...], preferred_element_type=F32)
    o_ref[...] = x + g_ref[...] * y


def _out_proj(attn, mix, w_bf, src, mod3, n_rows):
    tm, tn = 512, 2048
    nj = D_MODEL // tn
    assert isinstance(attn, tuple) == isinstance(src, tuple)
    if isinstance(src, tuple):
        n_first = N_LAT // tm

        def first(i):
            return jnp.minimum(i, n_first - 1)

        def second(i):
            return jnp.maximum(i - n_first, 0)

        attn_specs = [pl.BlockSpec((tm, ATTN_W), lambda j, i: (first(i), 0)),
                      pl.BlockSpec((tm, ATTN_W), lambda j, i: (second(i), 0))]
        x_specs = [pl.BlockSpec((tm, tn), lambda j, i: (first(i), j)),
                   pl.BlockSpec((tm, tn), lambda j, i: (second(i), j))]
    else:
        n_first, attn, src = None, (attn,), (src,)
        attn_specs = [pl.BlockSpec((tm, ATTN_W), lambda j, i: (i, 0))]
        x_specs = [pl.BlockSpec((tm, tn), lambda j, i: (i, j))]
    return pl.pallas_call(
        functools.partial(_outproj_kernel, n_first=n_first),
        grid=(nj, n_rows // tm),
        in_specs=attn_specs + [
            pl.BlockSpec((tm, POOL_W + SGU_W), lambda j, i: (i, 0)),
            pl.BlockSpec((ATTN_W, tn), lambda j, i: (0, j)),
            pl.BlockSpec((POOL_W + SGU_W, tn), lambda j, i: (1, j)),
        ] + x_specs + [
            pl.BlockSpec((None, 1, tn), lambda j, i: (_mod_row(i, tm), 0, 2 * nj + j)),
        ],
        out_specs=pl.BlockSpec((tm, tn), lambda j, i: (i, j)),
        out_shape=jax.ShapeDtypeStruct((n_rows, D_MODEL), F32),
        compiler_params=_cparams(("arbitrary", "arbitrary")),
        name="out_proj",
    )(*attn, mix, w_bf, w_bf, *src, mod3)


def _router_kernel(x_ref, g_ref, sh_ref, sc_ref, rw_ref, rb_ref, tri_ref,
                   hp_ref, idx_ref, rank_ref, gate_ref, cnt_ref, run_s):
    @pl.when(pl.program_id(0) == 0)
    def _():
        run_s[...] = jnp.zeros_like(run_s)

    h = _rms(x_ref[...], g_ref[...]) * (1.0 + sc_ref[...]) + sh_ref[...]
    h_hi = h.astype(BF16)
    hb = h_hi.astype(F32)
    hi = lax.bitcast_convert_type(hb[:, :HALF_D], U32)
    lo = lax.bitcast_convert_type(hb[:, HALF_D:], U32)
    packed = hi | (lo >> 16)
    for j in range(PACK_TILES):
        hp_ref[pl.ds(j, packed.shape[0], stride=PACK_TILES), :] = packed[:, j * LANES:(j + 1) * LANES]

    h_lo = (h - hb).astype(BF16)
    both = jnp.dot(h_hi, rw_ref[...], preferred_element_type=F32)
    tail = jnp.dot(h_lo, rw_ref[:, 0:LANES], preferred_element_type=F32)
    logits = (both[:, 0:LANES] + both[:, LANES:]) + tail
    lt = (logits + rb_ref[...]).T[0:N_EXPERTS, :]
    expert = lax.broadcasted_iota(jnp.int32, lt.shape, 0).astype(F32)
    vals, idxs = [], []
    for _ in range(TOP_K):
        m = jnp.max(lt, axis=0, keepdims=True)
        idx = jnp.min(jnp.where(lt == m, expert, float(N_EXPERTS)), axis=0, keepdims=True)
        vals.append(m)
        idxs.append(idx)
        lt = jnp.where(expert == idx, -jnp.inf, lt)

    e = [jnp.exp(v - vals[0]) for v in vals]
    den = e[0] + e[1] + e[2] + e[3]
    slot = lax.broadcasted_iota(jnp.int32, (LANES, lt.shape[1]), 0)
    gates = jnp.zeros((LANES, lt.shape[1]), F32)
    for k in range(TOP_K):
        gates = jnp.where(slot == k, e[k] / den, gates)
    gate_ref[...] = gates.T

    base = run_s[...]
    for k in range(TOP_K):
        onehot = jnp.where(expert == idxs[k], 1.0, 0.0)
        before = jnp.dot(onehot.astype(BF16), tri_ref[...], preferred_element_type=F32)
        rank = jnp.sum(onehot * (before + base[:, 0:1]), axis=0, keepdims=True)
        idx_ref[k:k + 1, :] = idxs[k].astype(jnp.int32)
        rank_ref[k:k + 1, :] = rank.astype(jnp.int32)
        base = base + jnp.sum(onehot, axis=1, keepdims=True)
    run_s[...] = base
    cnt_ref[...] = base.astype(jnp.int32)


def _router(tok, norm_g, mod3, rw_pad, rb_pad, tri, n_rows):
    tm = ROUTER_TM
    return pl.pallas_call(
        _router_kernel,
        grid=(n_rows // tm,),
        in_specs=[
            pl.BlockSpec((tm, D_MODEL), lambda i: (i, 0)),
            pl.BlockSpec((1, D_MODEL), lambda i: (0, 0)),
            pl.BlockSpec((None, 1, D_MODEL), lambda i: (_mod_row(i, tm), 0, 3)),
            pl.BlockSpec((None, 1, D_MODEL), lambda i: (_mod_row(i, tm), 0, 4)),
            pl.BlockSpec((D_MODEL, 2 * LANES), lambda i: (0, 0)),
            pl.BlockSpec((1, LANES), lambda i: (0, 0)),
            pl.BlockSpec((tm, tm), lambda i: (0, 0)),
        ],
        out_specs=[
            pl.BlockSpec((tm * PACK_TILES, LANES), lambda i: (i, 0)),
            pl.BlockSpec((TOP_K, tm), lambda i: (0, i)),
            pl.BlockSpec((TOP_K, tm), lambda i: (0, i)),
            pl.BlockSpec((tm, LANES), lambda i: (i, 0)),
            pl.BlockSpec((N_EXPERTS, LANES), lambda i: (0, 0)),
        ],
        out_shape=[
            jax.ShapeDtypeStruct((n_rows * PACK_TILES, LANES), U32),
            jax.ShapeDtypeStruct((TOP_K, n_rows), jnp.int32),
            jax.ShapeDtypeStruct((TOP_K, n_rows), jnp.int32),
            jax.ShapeDtypeStruct((n_rows, LANES), F32),
            jax.ShapeDtypeStruct((N_EXPERTS, LANES), jnp.int32),
        ],
        scratch_shapes=[pltpu.VMEM((N_EXPERTS, LANES), F32)],
        compiler_params=_cparams(("arbitrary",)),
        name="router",
    )(tok, norm_g.reshape(1, D_MODEL), mod3, mod3, rw_pad, rb_pad, tri)


def _dispatch_kernel(dest_ref, pad_ref, hp_ref, xs_hbm, sem, pad_sem, *, rows, n_rows):
    i = pl.program_id(0)

    @pl.when(i == 0)
    def _():
        def per_expert(e, carry):
            lo, hi = pad_ref[e], pad_ref[N_EXPERTS + 1 + e]

            def start(p, c):
                pltpu.make_async_copy(hp_ref.at[0], xs_hbm.at[p], pad_sem).start()
                return c

            def wait(p, c):
                pltpu.make_async_copy(hp_ref.at[0], xs_hbm.at[p], pad_sem).wait()
                return c

            lax.fori_loop(lo, hi, start, 0)
            lax.fori_loop(lo, hi, wait, 0)
            return carry

        lax.fori_loop(0, N_EXPERTS + 1, per_expert, 0)

    base = i * rows

    def issue(r2, carry):
        for u in range(2):
            r = 2 * r2 + u
            for k in range(TOP_K):
                d = dest_ref[k * n_rows + base + r]
                pltpu.make_async_copy(hp_ref.at[r], xs_hbm.at[d], sem).start(priority=k % 2)
        return carry

    lax.fori_loop(0, rows // 2, issue, 0)
    for k in range(TOP_K):
        pltpu.make_async_copy(hp_ref, xs_hbm.at[pl.ds(0, rows)], sem).wait()


def _dispatch(dest_flat, pad_meta, hp, n_pad, n_rows):
    rows = 1024
    return pl.pallas_call(
        functools.partial(_dispatch_kernel, rows=rows, n_rows=n_rows),
        grid_spec=pltpu.PrefetchScalarGridSpec(
            num_scalar_prefetch=2,
            grid=(n_rows // rows,),
            in_specs=[pl.BlockSpec((rows, PACK_TILES, LANES), lambda i, d, p: (i, 0, 0))],
            out_specs=pl.BlockSpec(memory_space=pl.ANY),
            scratch_shapes=[pltpu.SemaphoreType.DMA(()), pltpu.SemaphoreType.DMA(())],
        ),
        out_shape=jax.ShapeDtypeStruct((n_pad, PACK_TILES, LANES), U32),
        compiler_params=_cparams(("arbitrary",)),
        name="dispatch",
    )(dest_flat, pad_meta, hp)


def _is_new_expert(meta_ref, i):
    prev = meta_ref[jnp.maximum(i - 1, 0)]
    return jnp.logical_or(i == 0, meta_ref[i] != prev)


def _cast_rows(src_ref, dst_ref, chunk=256):
    def body(c, carry):
        rows = pl.ds(pl.multiple_of(c * chunk, chunk), chunk)
        dst_ref[rows, :] = src_ref[rows, :].astype(dst_ref.dtype)
        return carry

    lax.fori_loop(0, src_ref.shape[0] // chunk, body, 0)


def _for_valid_rows(valid, fn):
    lo = 0
    for n in EXPERT_ROW_STEPS:
        @pl.when(jnp.logical_and(valid > lo, valid <= n))
        def _(n=n):
            fn(n)

        lo = n


def _unpack_rows(x_ref, n):
    xp = jnp.concatenate([x_ref[pl.ds(j, n, stride=PACK_TILES), :] for j in range(PACK_TILES)], axis=1)
    hi = lax.bitcast_convert_type(xp & jnp.uint32(0xFFFF0000), F32).astype(BF16)
    lo = lax.bitcast_convert_type(xp << 16, F32).astype(BF16)
    return jnp.concatenate([hi, lo], axis=1)


def _expert_gu_kernel(meta_ref, x_ref, w_hbm, bg_ref, bu_ref, o_ref, wg_f, wu_f, wg_s, wu_s, sems,
                      *, n_blocks, layer):
    j, i = pl.program_id(0), pl.program_id(1)
    tn = wg_f.shape[1]
    nj = pl.num_programs(0)

    def fetch(e, jj):
        col = pl.multiple_of(jj * tn, tn)
        return (pltpu.make_async_copy(w_hbm.at[layer, e, :, pl.ds(col, tn)], wg_f, sems.at[0]),
                pltpu.make_async_copy(w_hbm.at[layer, e, :, pl.ds(EXPERT_FF + col, tn)], wu_f, sems.at[1]))

    @pl.when(jnp.logical_and(j == 0, i == 0))
    def _():
        for cp in fetch(meta_ref[0], 0):
            cp.start()

    active = i < meta_ref[n_blocks]

    @pl.when(jnp.logical_and(active, _is_new_expert(meta_ref, i)))
    def _():
        for cp in fetch(meta_ref[i], j):
            cp.wait()
        _cast_rows(wg_f, wg_s)
        _cast_rows(wu_f, wu_s)
        nxt = meta_ref[n_blocks + 1 + i]

        @pl.when(nxt >= 0)
        def _():
            for cp in fetch(nxt, j):
                cp.start()

        @pl.when(jnp.logical_and(nxt < 0, j + 1 < nj))
        def _():
            for cp in fetch(meta_ref[0], j + 1):
                cp.start()

    def compute(n):
        x = _unpack_rows(x_ref, n)
        g = jnp.dot(x, wg_s[...], preferred_element_type=F32) + bg_ref[...]
        u = jnp.dot(x, wu_s[...], preferred_element_type=F32) + bu_ref[...]
        g = jnp.minimum(g, SWIGLU_LIMIT)
        u = jnp.clip(u, -SWIGLU_LIMIT, SWIGLU_LIMIT)
        o_ref[0:n, :] = ((u + 1.0) * (g * _sigmoid(g * SWIGLU_ALPHA))).astype(o_ref.dtype)
        if n < o_ref.shape[0]:
            o_ref[n:, :] = jnp.zeros((o_ref.shape[0] - n, o_ref.shape[1]), o_ref.dtype)

    _for_valid_rows(meta_ref[2 * n_blocks + 1 + i], compute)

    @pl.when(jnp.logical_not(active))
    def _():
        o_ref[...] = jnp.zeros_like(o_ref)


def _active_block(i, m, n_blocks):
    return jnp.minimum(i, m[n_blocks] - 1)


def _expert_gu(meta, xs, w_gu, b_gu, layer, n_blocks):
    tm, tn = EXPERT_TM, 1024
    nj = EXPERT_FF // tn
    b3 = b_gu.reshape(DEPTH, N_EXPERTS, 1, 2 * EXPERT_FF)
    return pl.pallas_call(
        functools.partial(_expert_gu_kernel, n_blocks=n_blocks, layer=layer),
        grid_spec=pltpu.PrefetchScalarGridSpec(
            num_scalar_prefetch=1,
            grid=(nj, n_blocks),
            in_specs=[
                pl.BlockSpec((tm * PACK_TILES, LANES), lambda j, i, m: (_active_block(i, m, n_blocks), 0)),
                pl.BlockSpec(memory_space=pl.ANY),
                pl.BlockSpec((None, None, 1, tn), lambda j, i, m: (layer, m[i], 0, j)),
                pl.BlockSpec((None, None, 1, tn), lambda j, i, m: (layer, m[i], 0, nj + j)),
            ],
            out_specs=pl.BlockSpec((tm, tn), lambda j, i, m: (i, j)),
            scratch_shapes=[pltpu.VMEM((D_MODEL, tn), F32), pltpu.VMEM((D_MODEL, tn), F32),
                            pltpu.VMEM((D_MODEL, tn), BF16), pltpu.VMEM((D_MODEL, tn), BF16),
                            pltpu.SemaphoreType.DMA((2,))],
        ),
        out_shape=jax.ShapeDtypeStruct((n_blocks * tm, EXPERT_FF), BF16),
        compiler_params=_cparams(("arbitrary", "arbitrary")),
        name="expert_gu",
    )(meta, xs, w_gu, b3, b3)


def _expert_down_kernel(meta_ref, a_ref, w_hbm, b_ref, o_ref, w_f, w_s, sem, *, n_blocks, layer):
    i = pl.program_id(0)

    def fetch(e):
        return pltpu.make_async_copy(w_hbm.at[layer, e], w_f, sem)

    @pl.when(i == 0)
    def _():
        fetch(meta_ref[0]).start()

    active = i < meta_ref[n_blocks]

    @pl.when(jnp.logical_and(active, _is_new_expert(meta_ref, i)))
    def _():
        fetch(meta_ref[i]).wait()
        _cast_rows(w_f, w_s)
        nxt = meta_ref[n_blocks + 1 + i]

        @pl.when(nxt >= 0)
        def _():
            fetch(nxt).start()

    def compute(n):
        o_ref[0:n, :] = jnp.dot(a_ref[0:n, :], w_s[...], preferred_element_type=F32) + b_ref[...]
        if n < o_ref.shape[0]:
            o_ref[n:, :] = jnp.zeros((o_ref.shape[0] - n, o_ref.shape[1]), o_ref.dtype)

    _for_valid_rows(meta_ref[2 * n_blocks + 1 + i], compute)

    @pl.when(jnp.logical_not(active))
    def _():
        o_ref[...] = jnp.zeros_like(o_ref)


def _expert_down(meta, act, w_down, b_down, layer, n_blocks):
    tm = EXPERT_TM
    b3 = b_down.reshape(DEPTH, N_EXPERTS, 1, D_MODEL)
    return pl.pallas_call(
        functools.partial(_expert_down_kernel, n_blocks=n_blocks, layer=layer),
        grid_spec=pltpu.PrefetchScalarGridSpec(
            num_scalar_prefetch=1,
            grid=(n_blocks,),
            in_specs=[
                pl.BlockSpec((tm, EXPERT_FF), lambda i, m: (_active_block(i, m, n_blocks), 0)),
                pl.BlockSpec(memory_space=pl.ANY),
                pl.BlockSpec((None, None, 1, D_MODEL), lambda i, m: (layer, m[i], 0, 0)),
            ],
            out_specs=pl.BlockSpec((tm, D_MODEL), lambda i, m: (i, 0)),
            scratch_shapes=[pltpu.VMEM((EXPERT_FF, D_MODEL), F32), pltpu.VMEM((EXPERT_FF, D_MODEL), BF16),
                            pltpu.SemaphoreType.DMA(())],
        ),
        out_shape=jax.ShapeDtypeStruct((n_blocks * tm, D_MODEL), F32),
        compiler_params=_cparams(("arbitrary",)),
        name="expert_down",
    )(meta, act, w_down, b3)


def _combine_kernel(dest_ref, y_hbm, gt_ref, x_ref, g_ref, o_ref, buf, sems, *, rows, n_rows):
    i = pl.program_id(0)
    n_steps = pl.num_programs(0)

    def issue(step, slot):
        base = step * rows

        def body(r8, carry):
            r0 = pl.multiple_of(r8 * 8, 8)
            for u in range(8):
                for k in range(TOP_K):
                    d = dest_ref[k * n_rows + base + r0 + u]
                    pltpu.make_async_copy(y_hbm.at[pl.ds(d, 1), :],
                                          buf.at[slot, k, pl.ds(r0, 8), :].at[pl.ds(u, 1), :],
                                          sems.at[slot]).start(priority=k % 2)
            return carry

        lax.fori_loop(0, rows // 8, body, 0)

    @pl.when(i == 0)
    def _():
        issue(0, 0)

    @pl.when(i + 1 < n_steps)
    def _():
        issue(i + 1, (i + 1) % 2)

    slot = i % 2
    for k in range(TOP_K):
        pltpu.make_async_copy(y_hbm.at[pl.ds(0, rows), :], buf.at[slot, k], sems.at[slot]).wait()
    gt = gt_ref[...]
    moe = gt[:, 0:1] * buf[slot, 0]
    for k in range(1, TOP_K):
        moe = moe + gt[:, k:k + 1] * buf[slot, k]
    o_ref[...] = x_ref[...] + g_ref[...] * moe


def _combine(dest_flat, ys, gates_t, tok, mod3, n_rows):
    rows = 128
    return pl.pallas_call(
        functools.partial(_combine_kernel, rows=rows, n_rows=n_rows),
        grid_spec=pltpu.PrefetchScalarGridSpec(
            num_scalar_prefetch=1,
            grid=(n_rows // rows,),
            in_specs=[
                pl.BlockSpec(memory_space=pl.ANY),
                pl.BlockSpec((rows, LANES), lambda i, d: (i, 0)),
                pl.BlockSpec((rows, D_MODEL), lambda i, d: (i, 0)),
                pl.BlockSpec((None, 1, D_MODEL), lambda i, d: (_mod_row(i, rows), 0, 5)),
            ],
            out_specs=pl.BlockSpec((rows, D_MODEL), lambda i, d: (i, 0)),
            scratch_shapes=[pltpu.VMEM((2, TOP_K, rows, D_MODEL), F32), pltpu.SemaphoreType.DMA((2,))],
        ),
        out_shape=jax.ShapeDtypeStruct((n_rows, D_MODEL), F32),
        compiler_params=_cparams(("arbitrary",)),
        name="combine",
    )(dest_flat, ys, gates_t, tok, mod3)


def _routing(top_idx, rank, counts, n_rows):
    tm = EXPERT_TM
    n_blocks = -(-(TOP_K * n_rows) // tm) + N_EXPERTS
    counts = counts[:, 0]
    padded = (counts + tm - 1) // tm * tm
    padded_end = jnp.cumsum(padded)
    padded_start = padded_end - padded
    experts = jnp.arange(N_EXPERTS, dtype=jnp.int32)
    start_of = jnp.sum(jnp.where(top_idx[:, :, None] == experts, padded_start, 0), axis=-1)
    dest = (start_of + rank).astype(jnp.int32).reshape(TOP_K * n_rows)
    block_start = jnp.arange(n_blocks, dtype=jnp.int32) * tm
    block_expert = jnp.minimum(jnp.sum(padded_end[None, :] <= block_start[:, None], axis=1), N_EXPERTS - 1)
    n_active = padded_end[-1:] // tm
    group_end = jnp.sum(jnp.where(block_expert[:, None] == experts, padded_end // tm, 0), axis=1)
    follower = jnp.sum(jnp.where(group_end[:, None] == jnp.arange(n_blocks)[None, :], block_expert, 0), axis=1)
    next_expert = jnp.where(group_end < n_active, follower, -1)
    is_expert = block_expert[:, None] == experts
    block_count = jnp.sum(jnp.where(is_expert, counts, 0), axis=1)
    block_first = jnp.sum(jnp.where(is_expert, padded_start, 0), axis=1)
    block_valid = jnp.clip(block_count - (block_start - block_first), 0, tm)
    meta = jnp.concatenate([block_expert, n_active, next_expert, block_valid]).astype(jnp.int32)
    n_pad = jnp.full((1,), n_blocks * tm, jnp.int32)
    pad_meta = jnp.concatenate([padded_start + counts, padded_end[-1:], padded_end, n_pad]).astype(jnp.int32)
    return dest, meta, pad_meta, n_blocks


def _rope_tables():
    rows = SEQ // GRID_W
    row = jnp.broadcast_to(jnp.arange(rows, dtype=F32)[:, None], (rows, GRID_W)).reshape(-1)
    col = jnp.broadcast_to(jnp.arange(GRID_W, dtype=F32)[None, :], (rows, GRID_W)).reshape(-1)
    inv_freq = ROPE_THETA ** (-jnp.arange(ROPE_FREQS, dtype=F32) / ROPE_FREQS)
    ang_r = row[:, None] * inv_freq
    ang_c = col[:, None] * inv_freq
    cos = jnp.concatenate([jnp.cos(ang_r), jnp.cos(ang_r), jnp.cos(ang_c), jnp.cos(ang_c)], axis=-1)
    sin = jnp.concatenate([-jnp.sin(ang_r), jnp.sin(ang_r), -jnp.sin(ang_c), jnp.sin(ang_c)], axis=-1)
    return cos, sin


def kernel(x, c, ctx, c_ctx, ada_w, ada_b, norm1_g, norm2_g, w_in, q_norm_g, k_norm_g, pool_w, pool_scale,
           sgu_norm_g, sgu_w, sgu_b, w_out, router_w, router_b, w_gu, b_gu, w_down, b_down):
    cos, sin = _rope_tables()
    tok = (x.reshape(N_LAT, D_MODEL), ctx.reshape(N_CTX, D_MODEL))
    cc = jnp.zeros((MOD_ROWS, D_MODEL), F32).at[:BATCH].set(c).at[BATCH].set(c_ctx)
    mod = _adaln(cc, ada_w, ada_b)
    tri = jnp.triu(jnp.ones((ROUTER_TM, ROUTER_TM), BF16), k=1)
    for l in range(DEPTH):
        last = l == DEPTH - 1
        n_rows = N_LAT if last else N_TOK
        mod3 = mod[l].reshape(MOD_ROWS, 1, 6 * D_MODEL)
        proj = _in_proj(tok, norm1_g[l], mod3, w_in[l].astype(BF16))
        attn = _latent_attention(proj, cos, sin, q_norm_g[l], k_norm_g[l])
        if not last:
            attn = (attn, _context_attention(proj, q_norm_g[l], k_norm_g[l]))
        sgu_b_full = jnp.broadcast_to(sgu_b[l][:, :, None], (N_SGU_HEADS, CHUNK, LANES))
        mix = _mixers(proj, pool_w[l].astype(BF16), pool_scale[l], sgu_norm_g[l], sgu_w[l].astype(BF16), sgu_b_full)
        tok = _out_proj(attn, mix, w_out[l].astype(BF16), tok, mod3, n_rows)
        rw_hi = router_w[l].astype(BF16)
        rw_lo = (router_w[l] - rw_hi.astype(F32)).astype(BF16)
        rw_pad = (jnp.zeros((D_MODEL, 2 * LANES), BF16).at[:, :N_EXPERTS].set(rw_hi)
                  .at[:, LANES:LANES + N_EXPERTS].set(rw_lo))
        rb_pad = jnp.zeros((1, LANES), F32).at[0, :N_EXPERTS].set(router_b[l])
        hp, top_idx, rank, gates_t, counts = _router(tok, norm2_g[l], mod3, rw_pad, rb_pad, tri, n_rows)
        dest, meta, pad_meta, n_blocks = _routing(top_idx, rank, counts, n_rows)
        n_pad = n_blocks * EXPERT_TM
        xs = _dispatch(dest, pad_meta, hp.reshape(n_rows, PACK_TILES, LANES), n_pad, n_rows)
        act = _expert_gu(meta, xs.reshape(n_pad * PACK_TILES, LANES), w_gu, b_gu, l, n_blocks)
        ys = _expert_down(meta, act, w_down, b_down, l, n_blocks)
        tok = _combine(dest, ys, gates_t, tok, mod3, n_rows)
    return tok.reshape(BATCH, SEQ, D_MODEL)
```
